```python
import math
import jax, jax.numpy as jnp
from jax import lax
import numpy as np

D_MODEL = 1024
BATCH = 4
SEQ = 8192
DEPTH = 4

S5_CHANNELS = D_MODEL // 2
S5_GROUP = 16
S5_GROUPS = S5_CHANNELS // S5_GROUP
S5_STATE = 64
DT_MIN = 1e-3
DT_MAX = 1e-1
DIL_WIDTH = D_MODEL // 2
DIL_HEAD_DIM = 64
DIL_HEADS = DIL_WIDTH // DIL_HEAD_DIM
DIL_PATTERNS = ((128, 1), (512, 4), (2048, 16))
BAND_BLOCK = 128
DIFF_HEAD_DIM = 64
DIFF_HEADS = D_MODEL // (2 * DIFF_HEAD_DIM)
DIFF_WIDTH = DIFF_HEADS * 2 * DIFF_HEAD_DIM
QUERY_BLOCK = 128
NUM_BUCKETS = 32
MAX_DISTANCE = 2048
BIAS_HEADS = DIL_HEADS + DIFF_HEADS
D_FF = -(-(8 * D_MODEL) // (3 * 256)) * 256
N_EVEN = (DEPTH + 1) // 2
N_ODD = DEPTH // 2
EVEN_IN = S5_CHANNELS + 3 * DIL_WIDTH
EVEN_OUT = S5_CHANNELS + DIL_WIDTH
EPS = 1e-6
NEG_INF = -1e30

kernel_name = "hybrid_s5_dilated_diffattn_trunk"


def rms_norm(x, g):
    xf = x.astype(jnp.float32)
    inv = lax.rsqrt(jnp.mean(xf * xf, axis=-1, keepdims=True) + EPS)
    return (xf * inv).astype(x.dtype) * g


def rel_bucket(dist):
    max_exact = NUM_BUCKETS // 2
    d = jnp.maximum(dist, 0)
    scaled = (jnp.log(jnp.maximum(d, 1).astype(jnp.float32) / max_exact)
              / math.log(MAX_DISTANCE / max_exact) * (NUM_BUCKETS - max_exact))
    large = jnp.minimum(max_exact + scaled.astype(jnp.int32), NUM_BUCKETS - 1)
    return jnp.where(d < max_exact, d, large)


def swiglu(h, w_gate, w_up, w_down):
    return (jax.nn.silu(h @ w_gate) * (h @ w_up)) @ w_down


def s5_mixer(u, lam_re, lam_im, log_dt, b_re, b_im, c_re, c_im, d_skip, glu_w, glu_b):
    f32 = jnp.float32
    bn, s, _ = u.shape
    uf = u.astype(f32).reshape(bn, s, S5_GROUPS, S5_GROUP)
    lr = lam_re.astype(f32)
    li = lam_im.astype(f32)
    dt = jnp.exp(log_dt.astype(f32))[:, None]
    mag = jnp.exp(lr * dt)
    ar = mag * jnp.cos(li * dt)
    ai = mag * jnp.sin(li * dt)
    nr, ni = ar - 1.0, ai
    den = lr * lr + li * li
    cr = ((nr * lr + ni * li) / den)[..., None]
    ci = ((ni * lr - nr * li) / den)[..., None]
    br, bi = b_re.astype(f32), b_im.astype(f32)
    bbar_r = cr * br - ci * bi
    bbar_i = cr * bi + ci * br
    xr = jnp.einsum('bsgc,gpc->bsgp', uf, bbar_r)
    xi = jnp.einsum('bsgc,gpc->bsgp', uf, bbar_i)
    a_r = jnp.broadcast_to(ar, (1, s, S5_GROUPS, S5_STATE))
    a_i = jnp.broadcast_to(ai, (1, s, S5_GROUPS, S5_STATE))

    def combine(left, right):
        a1r, a1i, b1r, b1i = left
        a2r, a2i, b2r, b2i = right
        return (a2r * a1r - a2i * a1i,
                a2r * a1i + a2i * a1r,
                a2r * b1r - a2i * b1i + b2r,
                a2r * b1i + a2i * b1r + b2i)

    _, _, hr, hi = lax.associative_scan(combine, (a_r, a_i, xr, xi), axis=1)
    y = (jnp.einsum('bsgp,gcp->bsgc', hr, c_re.astype(f32))
         - jnp.einsum('bsgp,gcp->bsgc', hi, c_im.astype(f32)))
    y = y + d_skip.astype(f32).reshape(S5_GROUPS, S5_GROUP) * uf
    y = jax.nn.gelu(y.reshape(bn, s, S5_CHANNELS))
    y = y * jax.nn.sigmoid(y @ glu_w.astype(f32) + glu_b.astype(f32))
    return y.astype(u.dtype)


def dilated_branch(q, k, v, table, window, dilation):
    f32 = jnp.float32
    bn, s, h, hd = q.shape
    length = s // dilation
    n_blk = -(-length // BAND_BLOCK)
    lp = n_blk * BAND_BLOCK
    span = window // dilation

    def to_blocks(t):
        t = t.reshape(bn, length, dilation, h, -1).transpose(0, 2, 1, 3, 4)
        t = jnp.pad(t, ((0, 0), (0, 0), (0, lp - length), (0, 0), (0, 0)))
        return t.reshape(bn, dilation, n_blk, BAND_BLOCK, h, t.shape[-1])

    def with_prev(t):
        prev = jnp.pad(t, ((0, 0), (0, 0), (1, 0), (0, 0), (0, 0), (0, 0)))[:, :, :-1]
        return jnp.concatenate([prev, t], axis=3)

    qb = to_blocks(q)
    kb = with_prev(to_blocks(k))
    vb = with_prev(to_blocks(v))
    logits = jnp.einsum('brnqhe,brnkhe->brnhqk', qb, kb).astype(f32) * (hd ** -0.5)
    qi = jnp.arange(BAND_BLOCK)[:, None]
    kc = jnp.arange(2 * BAND_BLOCK)[None, :]
    steps = BAND_BLOCK + qi - kc
    in_band = (steps >= 0) & (steps <= span)
    exists = (jnp.arange(n_blk)[:, None, None] > 0) | (kc >= BAND_BLOCK)[None]
    mask = in_band[None] & exists
    bucket = rel_bucket(jnp.clip(steps, 0, span) * dilation)
    bias = jnp.transpose(table[bucket], (2, 0, 1)).astype(f32)
    logits = jnp.where(mask[None, None, :, None], logits + bias, NEG_INF)
    m = jnp.max(logits, axis=-1, keepdims=True)
    p = jnp.exp(logits - m)
    den = jnp.sum(p, axis=-1, keepdims=True)
    o = jnp.einsum('brnhqk,brnkhe->brnqhe', (p / den).astype(v.dtype), vb)
    lse = jnp.swapaxes((m + jnp.log(den))[..., 0], -1, -2)

    def from_blocks(t):
        t = t.reshape(bn, dilation, lp, h, -1)[:, :, :length]
        return t.transpose(0, 2, 1, 3, 4).reshape(bn, s, h, -1)

    return from_blocks(o.astype(f32)), from_blocks(lse[..., None])[..., 0]


def dilated_attention(q, k, v, table):
    outs, lses = [], []
    for window, dilation in DIL_PATTERNS:
        o, lse = dilated_branch(q, k, v, table, window, dilation)
        outs.append(o)
        lses.append(lse)
    alpha = jax.nn.softmax(jnp.stack(lses), axis=0)
    o = jnp.sum(alpha[..., None] * jnp.stack(outs), axis=0)
    return o.astype(q.dtype)


def diff_attention(q1, q2, k1, k2, v, table, lam):
    f32 = jnp.float32
    bn, s, h, hd = q1.shape
    n_blk = s // QUERY_BLOCK
    scale = hd ** -0.5
    kpos = jnp.arange(s)

    def to_blocks(t):
        return jnp.moveaxis(t.reshape(bn, n_blk, QUERY_BLOCK, h, hd), 1, 0)

    def one_block(args):
        q1b, q2b, start = args
        dist = (start + jnp.arange(QUERY_BLOCK))[:, None] - kpos[None, :]
        causal = dist >= 0
        bias = jnp.transpose(table[rel_bucket(dist)], (2, 0, 1)).astype(f32)

        def probs(qb, kk):
            lg = jnp.einsum('bqhe,bkhe->bhqk', qb, kk).astype(f32) * scale + bias
            return jax.nn.softmax(jnp.where(causal, lg, NEG_INF), axis=-1)

        a = probs(q1b, k1) - lam * probs(q2b, k2)
        return jnp.einsum('bhqk,bkhe->bqhe', a.astype(v.dtype), v)

    starts = jnp.arange(n_blk) * QUERY_BLOCK
    out = lax.map(one_block, (to_blocks(q1), to_blocks(q2), starts))
    return jnp.moveaxis(out, 0, 1).reshape(bn, s, h, 2 * hd)


def setup_inputs(seed: int = 0) -> dict:
    key = jax.random.key(seed)
    ks = jax.random.split(key, 32)
    f32 = jnp.float32
    nrm = lambda k, shape, scale: jax.random.normal(k, shape, f32) * scale
    n_idx = jnp.arange(S5_STATE, dtype=f32)
    return {
        "x": jax.random.normal(ks[0], (BATCH, SEQ, D_MODEL), f32),
        "rel_bias": nrm(ks[1], (NUM_BUCKETS, BIAS_HEADS), 0.2),
        "norm_mix": 1.0 + nrm(ks[2], (DEPTH, D_MODEL), 0.02),
        "norm_ffn": 1.0 + nrm(ks[3], (DEPTH, D_MODEL), 0.02),
        "norm_final": 1.0 + nrm(ks[4], (D_MODEL,), 0.02),
        "ffn_w_gate": nrm(ks[5], (DEPTH, D_MODEL, D_FF), D_MODEL ** -0.5),
        "ffn_w_up": nrm(ks[6], (DEPTH, D_MODEL, D_FF), D_MODEL ** -0.5),
        "ffn_w_down": nrm(ks[7], (DEPTH, D_FF, D_MODEL), D_FF ** -0.5),
        "even_w_in": nrm(ks[8], (N_EVEN, D_MODEL, EVEN_IN), D_MODEL ** -0.5),
        "even_w_out": nrm(ks[9], (N_EVEN, EVEN_OUT, D_MODEL), EVEN_OUT ** -0.5),
        "s5_lambda_re": -0.5 + nrm(ks[10], (N_EVEN, S5_GROUPS, S5_STATE), 0.01),
        "s5_lambda_im": math.pi * n_idx + nrm(ks[11], (N_EVEN, S5_GROUPS, S5_STATE), 0.01),
        "s5_log_dt": jax.random.uniform(ks[12], (N_EVEN, S5_GROUPS), f32,
                                        math.log(DT_MIN), math.log(DT_MAX)),
        "s5_b_re": nrm(ks[13], (N_EVEN, S5_GROUPS, S5_STATE, S5_GROUP), (2 * S5_GROUP) ** -0.5),
        "s5_b_im": nrm(ks[14], (N_EVEN, S5_GROUPS, S5_STATE, S5_GROUP), (2 * S5_GROUP) ** -0.5),
        "s5_c_re": nrm(ks[15], (N_EVEN, S5_GROUPS, S5_GROUP, S5_STATE), (2 * S5_STATE) ** -0.5),
        "s5_c_im": nrm(ks[16], (N_EVEN, S5_GROUPS, S5_GROUP, S5_STATE), (2 * S5_STATE) ** -0.5),
        "s5_d": nrm(ks[17], (N_EVEN, S5_CHANNELS), 1.0),
        "s5_glu_w": nrm(ks[18], (N_EVEN, S5_CHANNELS, S5_CHANNELS), S5_CHANNELS ** -0.5),
        "s5_glu_b": nrm(ks[19], (N_EVEN, S5_CHANNELS), 0.02),
        "odd_w_in": nrm(ks[20], (N_ODD, D_MODEL, 3 * DIFF_WIDTH), D_MODEL ** -0.5),
        "odd_w_out": nrm(ks[21], (N_ODD, DIFF_WIDTH, D_MODEL), DIFF_WIDTH ** -0.5),
        "diff_lambda_q1": nrm(ks[22], (N_ODD, DIFF_HEAD_DIM), 0.1),
        "diff_lambda_k1": nrm(ks[23], (N_ODD, DIFF_HEAD_DIM), 0.1),
        "diff_lambda_q2": nrm(ks[24], (N_ODD, DIFF_HEAD_DIM), 0.1),
        "diff_lambda_k2": nrm(ks[25], (N_ODD, DIFF_HEAD_DIM), 0.1),
        "diff_subln": 1.0 + nrm(ks[26], (N_ODD, 2 * DIFF_HEAD_DIM), 0.02),
    }


def reference(x, rel_bias, norm_mix, norm_ffn, norm_final, ffn_w_gate, ffn_w_up, ffn_w_down,
              even_w_in, even_w_out, s5_lambda_re, s5_lambda_im, s5_log_dt, s5_b_re, s5_b_im,
              s5_c_re, s5_c_im, s5_d, s5_glu_w, s5_glu_b, odd_w_in, odd_w_out,
              diff_lambda_q1, diff_lambda_k1, diff_lambda_q2, diff_lambda_k2, diff_subln):
    bn, s, _ = x.shape
    table_dil = rel_bias[:, :DIL_HEADS]
    table_diff = rel_bias[:, DIL_HEADS:]
    for layer in range(DEPTH):
        h = rms_norm(x, norm_mix[layer])
        if layer % 2 == 0:
            e = layer // 2
            z = h @ even_w_in[e]
            u = z[..., :S5_CHANNELS]
            q, k, v = jnp.split(z[..., S5_CHANNELS:], 3, axis=-1)
            y_a = s5_mixer(u, s5_lambda_re[e], s5_lambda_im[e], s5_log_dt[e], s5_b_re[e],
                           s5_b_im[e], s5_c_re[e], s5_c_im[e], s5_d[e], s5_glu_w[e], s5_glu_b[e])
            hs = (bn, s, DIL_HEADS, DIL_HEAD_DIM)
            y_b = dilated_attention(q.reshape(hs), k.reshape(hs), v.reshape(hs), table_dil)
            mixed = jnp.concatenate([y_a, y_b.reshape(bn, s, DIL_WIDTH)], axis=-1)
            x = x + mixed @ even_w_out[e]
        else:
            o = layer // 2
            lam_init = 0.8 - 0.6 * math.exp(-0.3 * layer)
            z = h @ odd_w_in[o]
            q, k, v = jnp.split(z, 3, axis=-1)
            q = q.reshape(bn, s, DIFF_HEADS, 2, DIFF_HEAD_DIM)
            k = k.reshape(bn, s, DIFF_HEADS, 2, DIFF_HEAD_DIM)
            v = v.reshape(bn, s, DIFF_HEADS, 2 * DIFF_HEAD_DIM)
            lam = (jnp.exp(jnp.sum(diff_lambda_q1[o].astype(jnp.float32) * diff_lambda_k1[o].astype(jnp.float32)))
                   - jnp.exp(jnp.sum(diff_lambda_q2[o].astype(jnp.float32) * diff_lambda_k2[o].astype(jnp.float32)))
                   + lam_init)
            att = diff_attention(q[..., 0, :], q[..., 1, :], k[..., 0, :], k[..., 1, :], v,
                                 table_diff, lam)
            att = rms_norm(att, diff_subln[o]) * (1.0 - lam_init)
            x = x + att.reshape(bn, s, DIFF_WIDTH) @ odd_w_out[o]
        x = x + swiglu(rms_norm(x, norm_ffn[layer]), ffn_w_gate[layer], ffn_w_up[layer],
                       ffn_w_down[layer])
    return rms_norm(x, norm_final)
```

```python
import functools
import math

import jax
import jax.numpy as jnp
import numpy as np
from jax import lax
from jax.experimental import pallas as pl
from jax.experimental.pallas import tpu as pltpu

F32 = jnp.float32
BF16 = jnp.bfloat16

EPS = 1e-6
NEG_INF = -1e30
LANE = 128
VMEM_LIMIT = 56 * 1024 * 1024

HEAD_DIM = 64
S5_GROUP = 16
S5_STATE = 64
S5_CHUNK = 16
S5_LANE_GROUPS = LANE // S5_GROUP
DIL_PATTERNS = ((128, 1), (512, 4), (2048, 16))
BAND_BLOCK = 128
NUM_BUCKETS = 32
MAX_DISTANCE = 2048

TOKEN_TILE = 512
ATTN_TILE = 256
S5_ROW_TILE = 256
FFN_CHUNK = 256


def _params(*sem):
    return pltpu.CompilerParams(dimension_semantics=sem, vmem_limit_bytes=VMEM_LIMIT)


def _resident(shape):
    return pl.BlockSpec(shape, lambda *_: (0,) * len(shape), pipeline_mode=pl.Buffered(1))


def _rms_scale(x, g):
    inv = lax.rsqrt(jnp.mean(x * x, axis=-1, keepdims=True) + EPS)
    return (x * inv) * g


def _sigmoid(x):
    return 1.0 / (1.0 + jnp.exp(-x))


def _gelu_tanh(x):
    return 0.5 * x * (1.0 + jnp.tanh(math.sqrt(2.0 / math.pi) * (x + 0.044715 * (x * x * x))))


def _bucket_np(dist):
    max_exact = NUM_BUCKETS // 2
    d = np.maximum(dist, 0)
    scaled = (np.log(np.maximum(d, 1).astype(np.float64) / max_exact)
              / math.log(MAX_DISTANCE / max_exact) * (NUM_BUCKETS - max_exact))
    large = np.minimum(max_exact + scaled.astype(np.int64), NUM_BUCKETS - 1)
    return np.where(d < max_exact, d, large)


def _even_in_kernel(x_ref, g_ref, w_ref, u_ref, q_ref, k_ref, v_ref, *, width, scale):
    h = _rms_scale(x_ref[...], g_ref[...]).astype(BF16)
    z = jnp.dot(h, w_ref[:, 0:width], preferred_element_type=F32)
    for j in range(width // LANE):
        u_ref[j] = z[:, j * LANE:(j + 1) * LANE]
    q_ref[...] = (jnp.dot(h, w_ref[:, width:2 * width], preferred_element_type=F32) * scale).astype(BF16)
    k_ref[...] = jnp.dot(h, w_ref[:, 2 * width:3 * width], preferred_element_type=F32).astype(BF16)
    v_ref[...] = jnp.dot(h, w_ref[:, 3 * width:4 * width], preferred_element_type=F32).astype(BF16)


def _even_in_proj(x, g, w):
    n, d = x.shape
    width = w.shape[1] // 4
    tm = TOKEN_TILE
    nb = width // LANE
    row = lambda i: (i, 0)
    return pl.pallas_call(
        functools.partial(_even_in_kernel, width=width, scale=HEAD_DIM ** -0.5),
        grid=(n // tm,),
        in_specs=[pl.BlockSpec((tm, d), row), _resident((1, d)), _resident(w.shape)],
        out_specs=[pl.BlockSpec((nb, tm, LANE), lambda i: (0, i, 0)),
                   pl.BlockSpec((tm, width), row), pl.BlockSpec((tm, width), row),
                   pl.BlockSpec((tm, width), row)],
        out_shape=[jax.ShapeDtypeStruct((nb, n, LANE), F32),
                   jax.ShapeDtypeStruct((n, width), BF16),
                   jax.ShapeDtypeStruct((n, width), BF16),
                   jax.ShapeDtypeStruct((n, width), BF16)],
        compiler_params=_params("parallel"),
        name="even_in_proj",
    )(x, g, w)


def _odd_in_kernel(x_ref, g_ref, w_ref, q_ref, k_ref, v_ref, *, width, scale):
    h = _rms_scale(x_ref[...], g_ref[...]).astype(BF16)
    q_ref[...] = (jnp.dot(h, w_ref[:, 0:width], preferred_element_type=F32) * scale).astype(BF16)
    k_ref[...] = jnp.dot(h, w_ref[:, width:2 * width], preferred_element_type=F32).astype(BF16)
    v_ref[...] = jnp.dot(h, w_ref[:, 2 * width:3 * width], preferred_element_type=F32).astype(BF16)


def _odd_in_proj(x, g, w):
    n, d = x.shape
    width = w.shape[1] // 3
    tm = TOKEN_TILE
    row = lambda i: (i, 0)
    return pl.pallas_call(
        functools.partial(_odd_in_kernel, width=width, scale=HEAD_DIM ** -0.5),
        grid=(n // tm,),
        in_specs=[pl.BlockSpec((tm, d), row), _resident((1, d)), _resident(w.shape)],
        out_specs=[pl.BlockSpec((tm, width), row)] * 3,
        out_shape=[jax.ShapeDtypeStruct((n, width), BF16)] * 3,
        compiler_params=_params("parallel"),
        name="odd_in_proj",
    )(x, g, w)


def _s5_matrices(lam_re, lam_im, log_dt, b_re, b_im, c_re, c_im, d_skip):
    hi = lax.Precision.HIGHEST
    L = S5_CHUNK
    G, P = lam_re.shape
    nj = G // S5_LANE_GROUPS
    dt = jnp.exp(log_dt)[:, None]
    steps = jnp.arange(L + 1, dtype=F32)[:, None, None]
    mag = jnp.exp(lam_re * dt * steps)
    ang = lam_im * dt * steps
    pr, pi = mag * jnp.cos(ang), mag * jnp.sin(ang)
    nr, ni = pr[1] - 1.0, pi[1]
    den = lam_re * lam_re + lam_im * lam_im
    cr = ((nr * lam_re + ni * lam_im) / den)[..., None]
    ci = ((ni * lam_re - nr * lam_im) / den)[..., None]
    bbr = cr * b_re - ci * b_im
    bbi = cr * b_im + ci * b_re
    car = c_re[None] * pr[:, :, None, :] - c_im[None] * pi[:, :, None, :]
    cai = c_re[None] * pi[:, :, None, :] + c_im[None] * pr[:, :, None, :]
    eye = jnp.eye(S5_LANE_GROUPS, dtype=F32)

    kg = (jnp.einsum('dgcp,gpe->dgce', car[:L], bbr, precision=hi)
          - jnp.einsum('dgcp,gpe->dgce', cai[:L], bbi, precision=hi))
    lag = np.arange(L)[None, :] - np.arange(L)[:, None]
    kt = jnp.where((lag >= 0)[:, :, None, None, None], kg[np.maximum(lag, 0)], 0.0)
    kt = kt.reshape(L, L, nj, S5_LANE_GROUPS, S5_GROUP, S5_GROUP)
    t_mat = jnp.einsum('stjgce,gh->jshetgc', kt, eye).reshape(nj, L * LANE, L * LANE)

    rev = np.arange(L - 1, -1, -1)
    abr = pr[rev][..., None] * bbr[None] - pi[rev][..., None] * bbi[None]
    abi = pr[rev][..., None] * bbi[None] + pi[rev][..., None] * bbr[None]
    ab = jnp.stack([abr, abi]).reshape(2, L, nj, S5_LANE_GROUPS, P, S5_GROUP)
    wst = jnp.einsum('asjgpe,gh->jsheagp', ab, eye).reshape(nj, L * LANE, 2 * S5_LANE_GROUPS * P)

    wo = jnp.stack([car[1:], -cai[1:]]).reshape(2, L, nj, S5_LANE_GROUPS, S5_GROUP, P)
    wo = jnp.einsum('atjgcp,gh->jahptgc', wo, eye).reshape(nj, 2 * S5_LANE_GROUPS * P, L * LANE)

    a_chunk = jnp.stack([pr[L], pi[L]]).reshape(2, nj, S5_LANE_GROUPS * P)
    a_chunk = a_chunk.transpose(1, 0, 2).reshape(nj, 1, 2 * S5_LANE_GROUPS * P)
    skip = jnp.tile(d_skip.reshape(nj, 1, LANE), (1, 1, L))
    return t_mat.astype(BF16), wst.astype(BF16), wo.astype(BF16), a_chunk, skip


def _s5_kernel(u_ref, t_ref, wst_ref, wo_ref, a_ref, skip_ref, y_ref, s_scr, h_scr, carry_scr, *, rows, half):
    @pl.when(pl.program_id(2) == 0)
    def _():
        carry_scr[...] = jnp.zeros_like(carry_scr)

    xf = u_ref[0]
    xb = xf.astype(BF16)
    s_scr[...] = jnp.dot(xb, wst_ref[0], preferred_element_type=F32)
    ar = a_ref[0, :, 0:half]
    ai = a_ref[0, :, half:2 * half]

    def step(i, carry):
        hr, hi = carry
        h_scr[pl.ds(i, 1), 0:half] = hr
        h_scr[pl.ds(i, 1), half:2 * half] = hi
        sr = s_scr[pl.ds(i, 1), 0:half]
        si = s_scr[pl.ds(i, 1), half:2 * half]
        return ar * hr - ai * hi + sr, ar * hi + ai * hr + si

    hr, hi = lax.fori_loop(0, rows, step, (carry_scr[:, 0:half], carry_scr[:, half:2 * half]), unroll=8)
    carry_scr[:, 0:half] = hr
    carry_scr[:, half:2 * half] = hi

    y = (jnp.dot(xb, t_ref[0], preferred_element_type=F32)
         + jnp.dot(h_scr[...].astype(BF16), wo_ref[0], preferred_element_type=F32)
         + skip_ref[0] * xf)
    y_ref[0] = _gelu_tanh(y)


def _s5_mixer(u4, mats, batch):
    t_mat, wst, wo, a_chunk, skip = mats
    nj, n, _ = u4.shape
    cols = S5_CHUNK * LANE
    chunks = n // S5_CHUNK
    rows = S5_ROW_TILE
    per_seq = chunks // batch // rows
    nstate = wst.shape[2]
    uc = u4.reshape(nj, chunks, cols)
    tile = lambda j, b, k: (j, b * per_seq + k, 0)
    per_j = lambda j, b, k: (j, 0, 0)
    y = pl.pallas_call(
        functools.partial(_s5_kernel, rows=rows, half=nstate // 2),
        grid=(nj, batch, per_seq),
        in_specs=[pl.BlockSpec((1, rows, cols), tile),
                  pl.BlockSpec((1, cols, cols), per_j, pipeline_mode=pl.Buffered(1)),
                  pl.BlockSpec((1, cols, nstate), per_j, pipeline_mode=pl.Buffered(1)),
                  pl.BlockSpec((1, nstate, cols), per_j, pipeline_mode=pl.Buffered(1)),
                  pl.BlockSpec((1, 1, nstate), per_j),
                  pl.BlockSpec((1, 1, cols), per_j)],
        out_specs=pl.BlockSpec((1, rows, cols), tile),
        out_shape=jax.ShapeDtypeStruct((nj, chunks, cols), F32),
        scratch_shapes=[pltpu.VMEM((rows, nstate), F32), pltpu.VMEM((rows, nstate), F32),
                        pltpu.VMEM((1, nstate), F32)],
        compiler_params=_params("arbitrary", "arbitrary", "arbitrary"),
        name="s5_mixer",
    )(uc, t_mat, wst, wo, a_chunk, skip)
    return y.reshape(nj, n, LANE)


def _dilated_bias(table, window, dilation):
    span = window // dilation
    qi = np.arange(BAND_BLOCK)[:, None]
    kc = np.arange(2 * BAND_BLOCK)[None, :]
    steps = BAND_BLOCK + qi - kc
    in_band = (steps >= 0) & (steps <= span)
    bucket = _bucket_np(np.clip(steps, 0, span) * dilation)
    bias = jnp.transpose(table[bucket], (2, 0, 1)).astype(F32)
    rest = jnp.where(in_band[None], bias, NEG_INF)
    first = jnp.where((in_band & (kc >= BAND_BLOCK))[None], bias, NEG_INF)
    return jnp.stack([first, rest])


def _dilated_kernel(q_ref, kp_ref, kc_ref, vp_ref, vc_ref, bias_ref, o_ref, lse_ref, *, heads):
    q = q_ref[0]
    k = jnp.concatenate([kp_ref[0], kc_ref[0]], axis=0)
    v = jnp.concatenate([vp_ref[0], vc_ref[0]], axis=0)
    lane = lax.broadcasted_iota(jnp.int32, (q.shape[0], LANE), 1)
    lse = jnp.zeros((q.shape[0], LANE), F32)
    outs = []
    for h in range(heads):
        sl = slice(h * HEAD_DIM, (h + 1) * HEAD_DIM)
        s = lax.dot_general(q[:, sl], k[:, sl], (((1,), (1,)), ((), ())),
                            preferred_element_type=F32) + bias_ref[0, h]
        m = jnp.max(s, axis=1, keepdims=True)
        p = jnp.exp(s - m)
        den = jnp.sum(p, axis=1, keepdims=True)
        outs.append(jnp.dot(p.astype(BF16), v[:, sl], preferred_element_type=F32) / den)
        lse = jnp.where(lane == h, m + jnp.log(den), lse)
    o_ref[0] = jnp.concatenate(outs, axis=1)
    lse_ref[0] = lse


def _dilated_pattern(q, k, v, bias, batch, dilation):
    n, w = q.shape
    heads = w // HEAD_DIM
    length = n // batch // dilation
    nblk = length // BAND_BLOCK
    view = lambda t: t.reshape(batch, length, dilation * w)
    cur = lambda b, r, i: (b, i, r)
    prev = lambda b, r, i: (b, jnp.maximum(i - 1, 0), r)
    blk = (1, BAND_BLOCK, w)
    o, lse = pl.pallas_call(
        functools.partial(_dilated_kernel, heads=heads),
        grid=(batch, dilation, nblk),
        in_specs=[pl.BlockSpec(blk, cur), pl.BlockSpec(blk, prev), pl.BlockSpec(blk, cur),
                  pl.BlockSpec(blk, prev), pl.BlockSpec(blk, cur),
                  pl.BlockSpec((1, heads, BAND_BLOCK, 2 * BAND_BLOCK),
                               lambda b, r, i: (jnp.minimum(i, 1), 0, 0, 0))],
        out_specs=[pl.BlockSpec(blk, cur), pl.BlockSpec((1, BAND_BLOCK, LANE), cur)],
        out_shape=[jax.ShapeDtypeStruct((batch, length, dilation * w), F32),
                   jax.ShapeDtypeStruct((batch, length, dilation * LANE), F32)],
        compiler_params=_params("parallel", "parallel", "parallel"),
        name=f"dilated_attn_d{dilation}",
    )(view(q), view(k), view(k), view(v), view(v), bias)
    return o.reshape(n, w), lse.reshape(n, LANE)


def _even_out_kernel(x_ref, ya_ref, o0_ref, o1_ref, o2_ref, l0_ref, l1_ref, l2_ref,
                     gw_ref, gb_ref, w_ref, out_ref, *, heads):
    ya = jnp.concatenate([ya_ref[j] for j in range(ya_ref.shape[0])], axis=1)
    gate = jnp.dot(ya.astype(BF16), gw_ref[...], preferred_element_type=F32) + gb_ref[...]
    ya = ya * _sigmoid(gate)

    l0, l1, l2 = l0_ref[...], l1_ref[...], l2_ref[...]
    m = jnp.maximum(jnp.maximum(l0, l1), l2)
    e0, e1, e2 = jnp.exp(l0 - m), jnp.exp(l1 - m), jnp.exp(l2 - m)
    inv = 1.0 / (e0 + e1 + e2)
    alphas = (e0 * inv, e1 * inv, e2 * inv)
    width = o0_ref.shape[1]
    head_of_lane = lax.broadcasted_iota(jnp.int32, (x_ref.shape[0], width), 1) // HEAD_DIM
    yb = jnp.zeros((x_ref.shape[0], width), F32)
    for a, o_ref in zip(alphas, (o0_ref, o1_ref, o2_ref)):
        full = jnp.zeros_like(yb)
        for h in range(heads):
            full = jnp.where(head_of_lane == h, a[:, h:h + 1], full)
        yb = yb + full * o_ref[...]

    mixed = jnp.concatenate([ya, yb], axis=1).astype(BF16)
    out_ref[...] = x_ref[...] + jnp.dot(mixed, w_ref[...], preferred_element_type=F32)


def _even_out_proj(x, ya4, outs, lses, glu_w, glu_b, w_out):
    n, d = x.shape
    nj = ya4.shape[0]
    width = outs[0].shape[1]
    tm = TOKEN_TILE
    row = lambda i: (i, 0)
    return pl.pallas_call(
        functools.partial(_even_out_kernel, heads=width // HEAD_DIM),
        grid=(n // tm,),
        in_specs=[pl.BlockSpec((tm, d), row),
                  pl.BlockSpec((nj, tm, LANE), lambda i: (0, i, 0)),
                  pl.BlockSpec((tm, width), row), pl.BlockSpec((tm, width), row),
                  pl.BlockSpec((tm, width), row),
                  pl.BlockSpec((tm, LANE), row), pl.BlockSpec((tm, LANE), row),
                  pl.BlockSpec((tm, LANE), row),
                  _resident(glu_w.shape), _resident(glu_b.shape), _resident(w_out.shape)],
        out_specs=pl.BlockSpec((tm, d), row),
        out_shape=jax.ShapeDtypeStruct((n, d), F32),
        compiler_params=_params("parallel"),
        name="even_out_proj",
    )(x, ya4, *outs, *lses, glu_w, glu_b, w_out)


def _diff_bias(table, tile):
    first_const = int(np.argmax(_bucket_np(np.arange(4 * MAX_DISTANCE)) == NUM_BUCKETS - 1))
    n_off = -(-(first_const + tile - 1) // tile) + 1
    r = np.arange(tile)[:, None]
    c = np.arange(tile)[None, :]
    dist = np.arange(n_off)[:, None, None] * tile + r[None] - c[None]
    assert _bucket_np(dist[-1]).min() == NUM_BUCKETS - 1
    bias = jnp.transpose(table[_bucket_np(dist)], (3, 0, 1, 2)).astype(F32)
    return jnp.where((dist >= 0)[None], bias, NEG_INF)


def _diff_attn_kernel(lam_ref, q_ref, k_ref, v_ref, bias_ref, g_ref, o_ref,
                      m_scr, l_scr, acc_scr, *, tile, n_off, out_scale):
    qi = pl.program_id(2)
    q = q_ref[0]
    qs = (q[:, 0:HEAD_DIM], q[:, HEAD_DIM:2 * HEAD_DIM])
    m_scr[...] = jnp.full_like(m_scr, 2.0 * NEG_INF)
    l_scr[...] = jnp.zeros_like(l_scr)
    acc_scr[...] = jnp.zeros_like(acc_scr)

    def body(j, carry):
        start = pl.multiple_of(j * tile, tile)
        kb = k_ref[0, pl.ds(start, tile), :]
        vb = v_ref[0, pl.ds(start, tile), :]
        bias = bias_ref[0, jnp.minimum(qi - j, n_off - 1)]
        for a in range(2):
            s = lax.dot_general(qs[a], kb[:, a * HEAD_DIM:(a + 1) * HEAD_DIM],
                                (((1,), (1,)), ((), ())), preferred_element_type=F32) + bias
            m_prev = m_scr[a]
            m_new = jnp.maximum(m_prev, jnp.max(s, axis=1, keepdims=True))
            alpha = jnp.exp(m_prev - m_new)
            p = jnp.exp(s - m_new)
            l_scr[a] = alpha * l_scr[a] + jnp.sum(p, axis=1, keepdims=True)
            acc_scr[a] = alpha * acc_scr[a] + jnp.dot(p.astype(BF16), vb, preferred_element_type=F32)
            m_scr[a] = m_new
        return carry

    lax.fori_loop(0, qi + 1, body, 0)
    att = acc_scr[0] / l_scr[0] - lam_ref[...] * (acc_scr[1] / l_scr[1])
    o_ref[0] = (_rms_scale(att, g_ref[...]) * out_scale).astype(o_ref.dtype)


def _diff_attention(q, k, v, bias, lam, subln, batch, out_scale):
    n, w = q.shape
    seq = n // batch
    heads = w // LANE
    tile = ATTN_TILE
    n_off = bias.shape[1]
    view = lambda t: t.reshape(batch, seq, w)
    qmap = lambda b, h, i: (b, i, h)
    kvmap = lambda b, h, i: (b, 0, h)
    const = lambda b, h, i: (0, 0)
    out = pl.pallas_call(
        functools.partial(_diff_attn_kernel, tile=tile, n_off=n_off, out_scale=out_scale),
        grid=(batch, heads, seq // tile),
        in_specs=[pl.BlockSpec((1, LANE), const),
                  pl.BlockSpec((1, tile, LANE), qmap),
                  pl.BlockSpec((1, seq, LANE), kvmap),
                  pl.BlockSpec((1, seq, LANE), kvmap),
                  pl.BlockSpec((1, n_off, tile, tile), lambda b, h, i: (h, 0, 0, 0)),
                  pl.BlockSpec((1, LANE), const)],
        out_specs=pl.BlockSpec((1, tile, LANE), qmap),
        out_shape=jax.ShapeDtypeStruct((batch, seq, w), BF16),
        scratch_shapes=[pltpu.VMEM((2, tile, 1), F32), pltpu.VMEM((2, tile, 1), F32),
                        pltpu.VMEM((2, tile, LANE), F32)],
        compiler_params=_params("parallel", "parallel", "arbitrary"),
        name="diff_attention",
    )(lam, view(q), view(k), view(v), bias, subln)
    return out.reshape(n, w)


def _odd_out_kernel(x_ref, a_ref, w_ref, out_ref):
    out_ref[...] = x_ref[...] + jnp.dot(a_ref[...], w_ref[...], preferred_element_type=F32)


def _odd_out_proj(x, att, w_out):
    n, d = x.shape
    tm = TOKEN_TILE
    row = lambda i: (i, 0)
    return pl.pallas_call(
        _odd_out_kernel,
        grid=(n // tm,),
        in_specs=[pl.BlockSpec((tm, d), row), pl.BlockSpec((tm, att.shape[1]), row),
                  _resident(w_out.shape)],
        out_specs=pl.BlockSpec((tm, d), row),
        out_shape=jax.ShapeDtypeStruct((n, d), F32),
        compiler_params=_params("parallel"),
        name="odd_out_proj",
    )(x, att, w_out)


def _ffn_kernel(x_ref, g_ref, wg_ref, wu_ref, wd_ref, gf_ref, out_ref, act_scr, *, chunk, final_norm):
    x = x_ref[...]
    h = _rms_scale(x, g_ref[...]).astype(BF16)
    for c in range(0, wg_ref.shape[1], chunk):
        gate = jnp.dot(h, wg_ref[:, c:c + chunk], preferred_element_type=F32)
        up = jnp.dot(h, wu_ref[:, c:c + chunk], preferred_element_type=F32)
        act_scr[:, c:c + chunk] = ((gate * _sigmoid(gate)) * up).astype(BF16)
    y = x + jnp.dot(act_scr[...], wd_ref[...], preferred_element_type=F32)
    if final_norm:
        y = _rms_scale(y, gf_ref[...])
    out_ref[...] = y


def _ffn(x, g, w_gate, w_up, w_down, g_final, final_norm):
    n, d = x.shape
    d_ff = w_gate.shape[1]
    tm = TOKEN_TILE
    row = lambda i: (i, 0)
    return pl.pallas_call(
        functools.partial(_ffn_kernel, chunk=FFN_CHUNK, final_norm=final_norm),
        grid=(n // tm,),
        in_specs=[pl.BlockSpec((tm, d), row), _resident((1, d)), _resident(w_gate.shape),
                  _resident(w_up.shape), _resident(w_down.shape), _resident((1, d))],
        out_specs=pl.BlockSpec((tm, d), row),
        out_shape=jax.ShapeDtypeStruct((n, d), F32),
        scratch_shapes=[pltpu.VMEM((tm, d_ff), BF16)],
        compiler_params=_params("parallel"),
        name="ffn_final" if final_norm else "ffn",
    )(x, g, w_gate, w_up, w_down, g_final)


def kernel(x, rel_bias, norm_mix, norm_ffn, norm_final, ffn_w_gate, ffn_w_up, ffn_w_down, even_w_in, even_w_out, s5_lambda_re, s5_lambda_im, s5_log_dt, s5_b_re, s5_b_im, s5_c_re, s5_c_im, s5_d, s5_glu_w, s5_glu_b, odd_w_in, odd_w_out, diff_lambda_q1, diff_lambda_k1, diff_lambda_q2, diff_lambda_k2, diff_subln):
    batch, seq, d_model = x.shape
    depth = norm_mix.shape[0]
    dil_heads = even_w_in.shape[2] // 4 // HEAD_DIM
    table_dil = rel_bias[:, :dil_heads]
    table_diff = rel_bias[:, dil_heads:]
    dil_biases = [_dilated_bias(table_dil, w, r) for w, r in DIL_PATTERNS]
    diff_bias = _diff_bias(table_diff, ATTN_TILE)

    xs = x.reshape(batch * seq, d_model)
    for layer in range(depth):
        g_mix = norm_mix[layer][None, :]
        if layer % 2 == 0:
            e = layer // 2
            u4, q, k, v = _even_in_proj(xs, g_mix, even_w_in[e].astype(BF16))
            mats = _s5_matrices(s5_lambda_re[e], s5_lambda_im[e], s5_log_dt[e], s5_b_re[e], s5_b_im[e],
                                s5_c_re[e], s5_c_im[e], s5_d[e])
            ya4 = _s5_mixer(u4, mats, batch)
            outs, lses = [], []
            for (_, r), bias in zip(DIL_PATTERNS, dil_biases):
                o, lse = _dilated_pattern(q, k, v, bias, batch, r)
                outs.append(o)
                lses.append(lse)
            xs = _even_out_proj(xs, ya4, outs, lses, s5_glu_w[e].astype(BF16), s5_glu_b[e][None, :],
                                even_w_out[e].astype(BF16))
        else:
            o = layer // 2
            lam_init = 0.8 - 0.6 * math.exp(-0.3 * layer)
            lam = (jnp.exp(jnp.sum(diff_lambda_q1[o] * diff_lambda_k1[o]))
                   - jnp.exp(jnp.sum(diff_lambda_q2[o] * diff_lambda_k2[o])) + lam_init)
            q, k, v = _odd_in_proj(xs, g_mix, odd_w_in[o].astype(BF16))
            att = _diff_attention(q, k, v, diff_bias, jnp.full((1, LANE), lam, F32),
                                  diff_subln[o][None, :], batch, 1.0 - lam_init)
            xs = _odd_out_proj(xs, att, odd_w_out[o].astype(BF16))
        xs = _ffn(xs, norm_ffn[layer][None, :], ffn_w_gate[layer].astype(BF16),
                  ffn_w_up[layer].astype(BF16), ffn_w_down[layer].astype(BF16),
                  norm_final[None, :], layer == depth - 1)
    return xs.reshape(batch, seq, d_model)
```

```python
import functools
import math

import jax
import jax.numpy as jnp
import numpy as np
from jax import lax
from jax.experimental import pallas as pl
from jax.experimental.pallas import tpu as pltpu

F32 = jnp.float32
BF16 = jnp.bfloat16

EPS = 1e-6
NEG_INF = -1e30
LANE = 128
VMEM_LIMIT = 56 * 1024 * 1024

HEAD_DIM = 64
S5_GROUP = 16
S5_STATE = 64
S5_CHUNK = 16
S5_LANE_GROUPS = LANE // S5_GROUP
DIL_PATTERNS = ((128, 1), (512, 4), (2048, 16))
BAND_BLOCK = 128
NUM_BUCKETS = 32
MAX_DISTANCE = 2048

TOKEN_TILE = 512
ATTN_TILE = 512
ONES_ROWS = 16
LOG2E = math.log2(math.e)
S5_ROW_TILE = 256
FFN_CHUNK = 256


def _params(*sem):
    return pltpu.CompilerParams(dimension_semantics=sem, vmem_limit_bytes=VMEM_LIMIT)


def _resident(shape):
    return pl.BlockSpec(shape, lambda *_: (0,) * len(shape), pipeline_mode=pl.Buffered(1))


def _rms_scale(x, g):
    inv = lax.rsqrt(jnp.mean(x * x, axis=-1, keepdims=True) + EPS)
    return (x * inv) * g


def _sigmoid(x):
    return 1.0 / (1.0 + jnp.exp(-x))


def _gelu_tanh(x):
    return 0.5 * x * (1.0 + jnp.tanh(math.sqrt(2.0 / math.pi) * (x + 0.044715 * (x * x * x))))


def _bucket_np(dist):
    max_exact = NUM_BUCKETS // 2
    d = np.maximum(dist, 0)
    scaled = (np.log(np.maximum(d, 1).astype(np.float64) / max_exact)
              / math.log(MAX_DISTANCE / max_exact) * (NUM_BUCKETS - max_exact))
    large = np.minimum(max_exact + scaled.astype(np.int64), NUM_BUCKETS - 1)
    return np.where(d < max_exact, d, large)


def _lookup(table, bucket):
    idx = jnp.asarray(bucket.astype(np.int8))[None]
    expand = (slice(None),) + (None,) * bucket.ndim
    out = jnp.zeros((table.shape[1],) + bucket.shape, F32)
    for b in np.unique(bucket):
        out = jnp.where(idx == b, table[int(b)].astype(F32)[expand], out)
    return out


def _even_in_kernel(x_ref, g_ref, w_ref, u_ref, q_ref, k_ref, v_ref, *, width, scale):
    h = _rms_scale(x_ref[...], g_ref[...]).astype(BF16)
    z = jnp.dot(h, w_ref[:, 0:width], preferred_element_type=F32)
    for j in range(width // LANE):
        u_ref[j] = z[:, j * LANE:(j + 1) * LANE]
    q_ref[...] = (jnp.dot(h, w_ref[:, width:2 * width], preferred_element_type=F32) * scale).astype(BF16)
    k_ref[...] = jnp.dot(h, w_ref[:, 2 * width:3 * width], preferred_element_type=F32).astype(BF16)
    v_ref[...] = jnp.dot(h, w_ref[:, 3 * width:4 * width], preferred_element_type=F32).astype(BF16)


def _even_in_proj(x, g, w):
    n, d = x.shape
    width = w.shape[1] // 4
    tm = TOKEN_TILE
    nb = width // LANE
    row = lambda i: (i, 0)
    return pl.pallas_call(
        functools.partial(_even_in_kernel, width=width, scale=HEAD_DIM ** -0.5),
        grid=(n // tm,),
        in_specs=[pl.BlockSpec((tm, d), row), _resident((1, d)), _resident(w.shape)],
        out_specs=[pl.BlockSpec((nb, tm, LANE), lambda i: (0, i, 0)),
                   pl.BlockSpec((tm, width), row), pl.BlockSpec((tm, width), row),
                   pl.BlockSpec((tm, width), row)],
        out_shape=[jax.ShapeDtypeStruct((nb, n, LANE), F32),
                   jax.ShapeDtypeStruct((n, width), BF16),
                   jax.ShapeDtypeStruct((n, width), BF16),
                   jax.ShapeDtypeStruct((n, width), BF16)],
        compiler_params=_params("parallel"),
        name="even_in_proj",
    )(x, g, w)


_NT = (((1,), (1,)), ((), ()))


def _odd_in_kernel(x_ref, g_ref, wq_ref, wk_ref, wv_ref, qt_ref, k_ref, vt_ref, *, scale):
    h = _rms_scale(x_ref[...], g_ref[...]).astype(BF16)
    heads = qt_ref.shape[1]
    tm = h.shape[0]
    qt = lax.dot_general(wq_ref[...], h, _NT, preferred_element_type=F32) * scale
    qt_ref[0, :, 0] = qt.astype(BF16).reshape(heads, LANE, tm)
    k_ref[...] = jnp.dot(h, wk_ref[...], preferred_element_type=F32).astype(BF16)
    vt = lax.dot_general(wv_ref[...], h, _NT, preferred_element_type=F32)
    vt_ref[0, :, 0, 0:LANE, :] = vt.astype(BF16).reshape(heads, LANE, tm)
    vt_ref[0, :, 0, LANE:, :] = jnp.ones((heads, vt_ref.shape[3] - LANE, tm), BF16)


def _odd_in_proj(x, g, w, batch):
    n, d = x.shape
    width = w.shape[1] // 3
    heads = width // LANE
    tm = ATTN_TILE
    per_seq = n // batch // tm
    wq_t = w[:, 0:width].T
    wk = w[:, width:2 * width]
    wv_t = w[:, 2 * width:3 * width].T
    row = lambda i: (i, 0)
    tmap = lambda i: (i // per_seq, 0, i % per_seq, 0, 0)
    tshape = lambda rows: jax.ShapeDtypeStruct((batch, heads, per_seq, rows, tm), BF16)
    vrows = LANE + ONES_ROWS
    return pl.pallas_call(
        functools.partial(_odd_in_kernel, scale=HEAD_DIM ** -0.5 * LOG2E),
        grid=(n // tm,),
        in_specs=[pl.BlockSpec((tm, d), row), _resident((1, d)), _resident(wq_t.shape),
                  _resident(wk.shape), _resident(wv_t.shape)],
        out_specs=[pl.BlockSpec((1, heads, 1, LANE, tm), tmap), pl.BlockSpec((tm, width), row),
                   pl.BlockSpec((1, heads, 1, vrows, tm), tmap)],
        out_shape=[tshape(LANE), jax.ShapeDtypeStruct((n, width), BF16), tshape(vrows)],
        compiler_params=_params("parallel"),
        name="odd_in_proj",
    )(x, g, wq_t, wk, wv_t)


def _s5_matrices(lam_re, lam_im, log_dt, b_re, b_im, c_re, c_im, d_skip):
    hi = lax.Precision.HIGHEST
    L = S5_CHUNK
    G, P = lam_re.shape
    nj = G // S5_LANE_GROUPS
    dt = jnp.exp(log_dt)[:, None]
    steps = jnp.arange(L + 1, dtype=F32)[:, None, None]
    mag = jnp.exp(lam_re * dt * steps)
    ang = lam_im * dt * steps
    pr, pi = mag * jnp.cos(ang), mag * jnp.sin(ang)
    nr, ni = pr[1] - 1.0, pi[1]
    den = lam_re * lam_re + lam_im * lam_im
    cr = ((nr * lam_re + ni * lam_im) / den)[..., None]
    ci = ((ni * lam_re - nr * lam_im) / den)[..., None]
    bbr = cr * b_re - ci * b_im
    bbi = cr * b_im + ci * b_re
    car = c_re[None] * pr[:, :, None, :] - c_im[None] * pi[:, :, None, :]
    cai = c_re[None] * pi[:, :, None, :] + c_im[None] * pr[:, :, None, :]
    eye = jnp.eye(S5_LANE_GROUPS, dtype=F32)

    kg = (jnp.einsum('dgcp,gpe->dgce', car[:L], bbr, precision=hi)
          - jnp.einsum('dgcp,gpe->dgce', cai[:L], bbi, precision=hi))
    lag = np.arange(L)[None, :] - np.arange(L)[:, None]
    kt = jnp.where((lag >= 0)[:, :, None, None, None], kg[np.maximum(lag, 0)], 0.0)
    kt = kt.reshape(L, L, nj, S5_LANE_GROUPS, S5_GROUP, S5_GROUP)
    t_mat = jnp.einsum('stjgce,gh->jshetgc', kt, eye).reshape(nj, L * LANE, L * LANE)

    rev = np.arange(L - 1, -1, -1)
    abr = pr[rev][..., None] * bbr[None] - pi[rev][..., None] * bbi[None]
    abi = pr[rev][..., None] * bbi[None] + pi[rev][..., None] * bbr[None]
    ab = jnp.stack([abr, abi]).reshape(2, L, nj, S5_LANE_GROUPS, P, S5_GROUP)
    wst = jnp.einsum('asjgpe,gh->jsheagp', ab, eye).reshape(nj, L * LANE, 2 * S5_LANE_GROUPS * P)

    wo = jnp.stack([car[1:], -cai[1:]]).reshape(2, L, nj, S5_LANE_GROUPS, S5_GROUP, P)
    wo = jnp.einsum('atjgcp,gh->jahptgc', wo, eye).reshape(nj, 2 * S5_LANE_GROUPS * P, L * LANE)

    a_chunk = jnp.stack([pr[L], pi[L]]).reshape(2, nj, S5_LANE_GROUPS * P)
    a_chunk = a_chunk.transpose(1, 0, 2).reshape(nj, 1, 2 * S5_LANE_GROUPS * P)
    skip = jnp.tile(d_skip.reshape(nj, 1, LANE), (1, 1, L))
    return t_mat.astype(BF16), wst.astype(BF16), wo.astype(BF16), a_chunk, skip


def _s5_kernel(u_ref, t_ref, wst_ref, wo_ref, a_ref, skip_ref, y_ref, s_scr, h_scr, carry_scr, *, rows, half):
    @pl.when(pl.program_id(2) == 0)
    def _():
        carry_scr[...] = jnp.zeros_like(carry_scr)

    xf = u_ref[0]
    xb = xf.astype(BF16)
    s_scr[...] = jnp.dot(xb, wst_ref[0], preferred_element_type=F32)
    ar = a_ref[0, :, 0:half]
    ai = a_ref[0, :, half:2 * half]

    def step(i, carry):
        hr, hi = carry
        h_scr[pl.ds(i, 1), 0:half] = hr
        h_scr[pl.ds(i, 1), half:2 * half] = hi
        sr = s_scr[pl.ds(i, 1), 0:half]
        si = s_scr[pl.ds(i, 1), half:2 * half]
        return ar * hr - ai * hi + sr, ar * hi + ai * hr + si

    hr, hi = lax.fori_loop(0, rows, step, (carry_scr[:, 0:half], carry_scr[:, half:2 * half]), unroll=8)
    carry_scr[:, 0:half] = hr
    carry_scr[:, half:2 * half] = hi

    y = (jnp.dot(xb, t_ref[0], preferred_element_type=F32)
         + jnp.dot(h_scr[...].astype(BF16), wo_ref[0], preferred_element_type=F32)
         + skip_ref[0] * xf)
    y_ref[0] = _gelu_tanh(y)


def _s5_mixer(u4, mats, batch):
    t_mat, wst, wo, a_chunk, skip = mats
    nj, n, _ = u4.shape
    cols = S5_CHUNK * LANE
    chunks = n // S5_CHUNK
    rows = S5_ROW_TILE
    per_seq = chunks // batch // rows
    nstate = wst.shape[2]
    uc = u4.reshape(nj, chunks, cols)
    tile = lambda j, b, k: (j, b * per_seq + k, 0)
    per_j = lambda j, b, k: (j, 0, 0)
    y = pl.pallas_call(
        functools.partial(_s5_kernel, rows=rows, half=nstate // 2),
        grid=(nj, batch, per_seq),
        in_specs=[pl.BlockSpec((1, rows, cols), tile),
                  pl.BlockSpec((1, cols, cols), per_j, pipeline_mode=pl.Buffered(1)),
                  pl.BlockSpec((1, cols, nstate), per_j, pipeline_mode=pl.Buffered(1)),
                  pl.BlockSpec((1, nstate, cols), per_j, pipeline_mode=pl.Buffered(1)),
                  pl.BlockSpec((1, 1, nstate), per_j),
                  pl.BlockSpec((1, 1, cols), per_j)],
        out_specs=pl.BlockSpec((1, rows, cols), tile),
        out_shape=jax.ShapeDtypeStruct((nj, chunks, cols), F32),
        scratch_shapes=[pltpu.VMEM((rows, nstate), F32), pltpu.VMEM((rows, nstate), F32),
                        pltpu.VMEM((1, nstate), F32)],
        compiler_params=_params("arbitrary", "arbitrary", "arbitrary"),
        name="s5_mixer",
    )(uc, t_mat, wst, wo, a_chunk, skip)
    return y.reshape(nj, n, LANE)


def _dilated_bias(table, window, dilation):
    span = window // dilation
    qi = np.arange(BAND_BLOCK)[:, None]
    kc = np.arange(2 * BAND_BLOCK)[None, :]
    steps = BAND_BLOCK + qi - kc
    in_band = (steps >= 0) & (steps <= span)
    bias = _lookup(table, _bucket_np(np.clip(steps, 0, span) * dilation))
    rest = jnp.where(jnp.asarray(in_band)[None], bias, NEG_INF)
    first = jnp.where(jnp.asarray(in_band & (kc >= BAND_BLOCK))[None], bias, NEG_INF)
    return jnp.stack([first, rest])


def _dilated_kernel(q_ref, kp_ref, kc_ref, vp_ref, vc_ref, bias_ref, o_ref, lse_ref, *, heads):
    q = q_ref[0]
    k = jnp.concatenate([kp_ref[0], kc_ref[0]], axis=0)
    v = jnp.concatenate([vp_ref[0], vc_ref[0]], axis=0)
    lane = lax.broadcasted_iota(jnp.int32, (q.shape[0], LANE), 1)
    lse = jnp.zeros((q.shape[0], LANE), F32)
    outs = []
    for h in range(heads):
        sl = slice(h * HEAD_DIM, (h + 1) * HEAD_DIM)
        s = lax.dot_general(q[:, sl], k[:, sl], (((1,), (1,)), ((), ())),
                            preferred_element_type=F32) + bias_ref[0, h]
        m = jnp.max(s, axis=1, keepdims=True)
        p = jnp.exp(s - m)
        den = jnp.sum(p, axis=1, keepdims=True)
        outs.append(jnp.dot(p.astype(BF16), v[:, sl], preferred_element_type=F32) / den)
        lse = jnp.where(lane == h, m + jnp.log(den), lse)
    o_ref[0] = jnp.concatenate(outs, axis=1)
    lse_ref[0] = lse


def _dilated_pattern(q, k, v, bias, batch, dilation):
    n, w = q.shape
    heads = w // HEAD_DIM
    length = n // batch // dilation
    nblk = length // BAND_BLOCK
    view = lambda t: t.reshape(batch, length, dilation * w)
    cur = lambda b, r, i: (b, i, r)
    prev = lambda b, r, i: (b, jnp.maximum(i - 1, 0), r)
    blk = (1, BAND_BLOCK, w)
    o, lse = pl.pallas_call(
        functools.partial(_dilated_kernel, heads=heads),
        grid=(batch, dilation, nblk),
        in_specs=[pl.BlockSpec(blk, cur), pl.BlockSpec(blk, prev), pl.BlockSpec(blk, cur),
                  pl.BlockSpec(blk, prev), pl.BlockSpec(blk, cur),
                  pl.BlockSpec((1, heads, BAND_BLOCK, 2 * BAND_BLOCK),
                               lambda b, r, i: (jnp.minimum(i, 1), 0, 0, 0))],
        out_specs=[pl.BlockSpec(blk, cur), pl.BlockSpec((1, BAND_BLOCK, LANE), cur)],
        out_shape=[jax.ShapeDtypeStruct((batch, length, dilation * w), F32),
                   jax.ShapeDtypeStruct((batch, length, dilation * LANE), F32)],
        compiler_params=_params("parallel", "parallel", "parallel"),
        name=f"dilated_attn_d{dilation}",
    )(view(q), view(k), view(k), view(v), view(v), bias)
    return o.reshape(n, w), lse.reshape(n, LANE)


def _even_out_kernel(x_ref, ya_ref, o0_ref, o1_ref, o2_ref, l0_ref, l1_ref, l2_ref,
                     gw_ref, gb_ref, w_ref, out_ref, *, heads):
    ya = jnp.concatenate([ya_ref[j] for j in range(ya_ref.shape[0])], axis=1)
    gate = jnp.dot(ya.astype(BF16), gw_ref[...], preferred_element_type=F32) + gb_ref[...]
    ya = ya * _sigmoid(gate)

    l0, l1, l2 = l0_ref[...], l1_ref[...], l2_ref[...]
    m = jnp.maximum(jnp.maximum(l0, l1), l2)
    e0, e1, e2 = jnp.exp(l0 - m), jnp.exp(l1 - m), jnp.exp(l2 - m)
    inv = 1.0 / (e0 + e1 + e2)
    alphas = (e0 * inv, e1 * inv, e2 * inv)
    width = o0_ref.shape[1]
    head_of_lane = lax.broadcasted_iota(jnp.int32, (x_ref.shape[0], width), 1) // HEAD_DIM
    yb = jnp.zeros((x_ref.shape[0], width), F32)
    for a, o_ref in zip(alphas, (o0_ref, o1_ref, o2_ref)):
        full = jnp.zeros_like(yb)
        for h in range(heads):
            full = jnp.where(head_of_lane == h, a[:, h:h + 1], full)
        yb = yb + full * o_ref[...]

    mixed = jnp.concatenate([ya, yb], axis=1).astype(BF16)
    out_ref[...] = x_ref[...] + jnp.dot(mixed, w_ref[...], preferred_element_type=F32)


def _even_out_proj(x, ya4, outs, lses, glu_w, glu_b, w_out):
    n, d = x.shape
    nj = ya4.shape[0]
    width = outs[0].shape[1]
    tm = TOKEN_TILE
    row = lambda i: (i, 0)
    return pl.pallas_call(
        functools.partial(_even_out_kernel, heads=width // HEAD_DIM),
        grid=(n // tm,),
        in_specs=[pl.BlockSpec((tm, d), row),
                  pl.BlockSpec((nj, tm, LANE), lambda i: (0, i, 0)),
                  pl.BlockSpec((tm, width), row), pl.BlockSpec((tm, width), row),
                  pl.BlockSpec((tm, width), row),
                  pl.BlockSpec((tm, LANE), row), pl.BlockSpec((tm, LANE), row),
                  pl.BlockSpec((tm, LANE), row),
                  _resident(glu_w.shape), _resident(glu_b.shape), _resident(w_out.shape)],
        out_specs=pl.BlockSpec((tm, d), row),
        out_shape=jax.ShapeDtypeStruct((n, d), F32),
        compiler_params=_params("parallel"),
        name="even_out_proj",
    )(x, ya4, *outs, *lses, glu_w, glu_b, w_out)


def _diff_bias(table, tile):
    first_const = int(np.argmax(_bucket_np(np.arange(4 * MAX_DISTANCE)) == NUM_BUCKETS - 1))
    n_near = -(-(first_const + tile - 1) // tile)
    kr = np.arange(tile)[:, None]
    qc = np.arange(tile)[None, :]
    dist = np.arange(n_near)[:, None, None] * tile + qc[None] - kr[None]
    assert n_near * tile - (tile - 1) >= first_const
    rel = (table - table[NUM_BUCKETS - 1][None, :]) * LOG2E
    return jnp.where(jnp.asarray(dist >= 0)[None], _lookup(rel, _bucket_np(dist)), NEG_INF)


def _diff_attn_kernel(lam_ref, qt_ref, k_ref, vt_ref, bias_ref, g_ref, o_ref, acc_scr,
                      *, tile, n_near_max, out_scale):
    qi = pl.program_id(2)
    qt = qt_ref[0, 0, 0]
    row = lax.broadcasted_iota(jnp.int32, qt.shape, 0)
    zero = jnp.zeros_like(qt)
    q_both = jnp.concatenate([jnp.where(row < HEAD_DIM, qt, zero),
                              jnp.where(row >= HEAD_DIM, qt, zero)], axis=1)
    acc_scr[...] = jnp.zeros_like(acc_scr)
    m0 = jnp.full((1, tile), 2.0 * NEG_INF, F32)
    l0 = jnp.zeros((1, tile), F32)

    def step(j, carry, with_bias):
        stats = list(carry)
        start = pl.multiple_of(j * tile, tile)
        kb = k_ref[0, pl.ds(start, tile), :]
        vtb = vt_ref[0, 0, j]
        s = jnp.dot(kb, q_both, preferred_element_type=F32)
        for a in range(2):
            sa = s[:, a * tile:(a + 1) * tile]
            if with_bias:
                sa = sa + bias_ref[0, qi - j]
            m_prev, l_prev = stats[2 * a], stats[2 * a + 1]
            m_new = jnp.maximum(m_prev, jnp.max(sa, axis=0, keepdims=True))
            alpha = jnp.exp2(m_prev - m_new)
            p = jnp.exp2(sa - m_new).astype(BF16)
            r = jnp.dot(vtb, p, preferred_element_type=F32)
            stats[2 * a] = m_new
            stats[2 * a + 1] = alpha * l_prev + r[LANE:LANE + 1]
            acc_scr[a] = alpha * acc_scr[a] + r[0:LANE]
        return tuple(stats)

    n_near = jnp.minimum(qi + 1, n_near_max)
    carry = lax.fori_loop(0, qi + 1 - n_near, functools.partial(step, with_bias=False), (m0, l0, m0, l0))
    _, l1, _, l2 = lax.fori_loop(qi + 1 - n_near, qi + 1, functools.partial(step, with_bias=True), carry)
    att = acc_scr[0] * (1.0 / l1) - lam_ref[...] * (acc_scr[1] * (1.0 / l2))
    inv = lax.rsqrt(jnp.mean(att * att, axis=0, keepdims=True) + EPS)
    out_t = ((att * inv) * g_ref[...]) * out_scale
    o_ref[0] = out_t.T.astype(o_ref.dtype)


def _diff_attention(qt, k, vt, bias, lam, subln, out_scale):
    batch, heads, per_seq, _, tile = qt.shape
    n, w = k.shape
    seq = n // batch
    n_near = bias.shape[1]
    const = lambda b, h, i: (0, 0)
    out = pl.pallas_call(
        functools.partial(_diff_attn_kernel, tile=tile, n_near_max=n_near, out_scale=out_scale),
        grid=(batch, heads, per_seq),
        in_specs=[pl.BlockSpec((1, tile), const),
                  pl.BlockSpec((1, 1, 1, LANE, tile), lambda b, h, i: (b, h, i, 0, 0)),
                  pl.BlockSpec((1, seq, LANE), lambda b, h, i: (b, 0, h)),
                  pl.BlockSpec((1, 1, per_seq, vt.shape[3], tile), lambda b, h, i: (b, h, 0, 0, 0)),
                  pl.BlockSpec((1, n_near, tile, tile), lambda b, h, i: (h, 0, 0, 0)),
                  pl.BlockSpec((LANE, tile), const)],
        out_specs=pl.BlockSpec((1, tile, LANE), lambda b, h, i: (b, i, h)),
        out_shape=jax.ShapeDtypeStruct((batch, seq, w), BF16),
        scratch_shapes=[pltpu.VMEM((2, LANE, tile), F32)],
        compiler_params=_params("parallel", "parallel", "arbitrary"),
        name="diff_attention",
    )(jnp.full((1, tile), lam, F32), qt, k.reshape(batch, seq, w), vt, bias,
      jnp.broadcast_to(subln[:, None], (LANE, tile)))
    return out.reshape(n, w)


def _odd_out_kernel(x_ref, a_ref, w_ref, out_ref):
    out_ref[...] = x_ref[...] + jnp.dot(a_ref[...], w_ref[...], preferred_element_type=F32)


def _odd_out_proj(x, att, w_out):
    n, d = x.shape
    tm = TOKEN_TILE
    row = lambda i: (i, 0)
    return pl.pallas_call(
        _odd_out_kernel,
        grid=(n // tm,),
        in_specs=[pl.BlockSpec((tm, d), row), pl.BlockSpec((tm, att.shape[1]), row),
                  _resident(w_out.shape)],
        out_specs=pl.BlockSpec((tm, d), row),
        out_shape=jax.ShapeDtypeStruct((n, d), F32),
        compiler_params=_params("parallel"),
        name="odd_out_proj",
    )(x, att, w_out)


def _ffn_kernel(x_ref, g_ref, wg_ref, wu_ref, wd_ref, gf_ref, out_ref, act_scr, *, chunk, final_norm):
    x = x_ref[...]
    h = _rms_scale(x, g_ref[...]).astype(BF16)
    for c in range(0, wg_ref.shape[1], chunk):
        gate = jnp.dot(h, wg_ref[:, c:c + chunk], preferred_element_type=F32)
        up = jnp.dot(h, wu_ref[:, c:c + chunk], preferred_element_type=F32)
        act_scr[:, c:c + chunk] = ((gate * _sigmoid(gate)) * up).astype(BF16)
    y = x + jnp.dot(act_scr[...], wd_ref[...], preferred_element_type=F32)
    if final_norm:
        y = _rms_scale(y, gf_ref[...])
    out_ref[...] = y


def _ffn(x, g, w_gate, w_up, w_down, g_final, final_norm):
    n, d = x.shape
    d_ff = w_gate.shape[1]
    tm = TOKEN_TILE
    row = lambda i: (i, 0)
    return pl.pallas_call(
        functools.partial(_ffn_kernel, chunk=FFN_CHUNK, final_norm=final_norm),
        grid=(n // tm,),
        in_specs=[pl.BlockSpec((tm, d), row), _resident((1, d)), _resident(w_gate.shape),
                  _resident(w_up.shape), _resident(w_down.shape), _resident((1, d))],
        out_specs=pl.BlockSpec((tm, d), row),
        out_shape=jax.ShapeDtypeStruct((n, d), F32),
        scratch_shapes=[pltpu.VMEM((tm, d_ff), BF16)],
        compiler_params=_params("parallel"),
        name="ffn_final" if final_norm else "ffn",
    )(x, g, w_gate, w_up, w_down, g_final)


def kernel(x, rel_bias, norm_mix, norm_ffn, norm_final, ffn_w_gate, ffn_w_up, ffn_w_down, even_w_in, even_w_out, s5_lambda_re, s5_lambda_im, s5_log_dt, s5_b_re, s5_b_im, s5_c_re, s5_c_im, s5_d, s5_glu_w, s5_glu_b, odd_w_in, odd_w_out, diff_lambda_q1, diff_lambda_k1, diff_lambda_q2, diff_lambda_k2, diff_subln):
    batch, seq, d_model = x.shape
    depth = norm_mix.shape[0]
    dil_heads = even_w_in.shape[2] // 4 // HEAD_DIM
    table_dil = rel_bias[:, :dil_heads]
    table_diff = rel_bias[:, dil_heads:]
    dil_biases = [_dilated_bias(table_dil, w, r) for w, r in DIL_PATTERNS]
    diff_bias = _diff_bias(table_diff, ATTN_TILE)

    xs = x.reshape(batch * seq, d_model)
    for layer in range(depth):
        g_mix = norm_mix[layer][None, :]
        if layer % 2 == 0:
            e = layer // 2
            u4, q, k, v = _even_in_proj(xs, g_mix, even_w_in[e].astype(BF16))
            mats = _s5_matrices(s5_lambda_re[e], s5_lambda_im[e], s5_log_dt[e], s5_b_re[e], s5_b_im[e],
                                s5_c_re[e], s5_c_im[e], s5_d[e])
            ya4 = _s5_mixer(u4, mats, batch)
            outs, lses = [], []
            for (_, r), bias in zip(DIL_PATTERNS, dil_biases):
                o, lse = _dilated_pattern(q, k, v, bias, batch, r)
                outs.append(o)
                lses.append(lse)
            xs = _even_out_proj(xs, ya4, outs, lses, s5_glu_w[e].astype(BF16), s5_glu_b[e][None, :],
                                even_w_out[e].astype(BF16))
        else:
            o = layer // 2
            lam_init = 0.8 - 0.6 * math.exp(-0.3 * layer)
            lam = (jnp.exp(jnp.sum(diff_lambda_q1[o] * diff_lambda_k1[o]))
                   - jnp.exp(jnp.sum(diff_lambda_q2[o] * diff_lambda_k2[o])) + lam_init)
            qt, k, vt = _odd_in_proj(xs, g_mix, odd_w_in[o].astype(BF16), batch)
            att = _diff_attention(qt, k, vt, diff_bias, lam, diff_subln[o], 1.0 - lam_init)
            xs = _odd_out_proj(xs, att, odd_w_out[o].astype(BF16))
        xs = _ffn(xs, norm_ffn[layer][None, :], ffn_w_gate[layer].astype(BF16),
                  ffn_w_up[layer].astype(BF16), ffn_w_down[layer].astype(BF16),
                  norm_final[None, :], layer == depth - 1)
    return xs.reshape(batch, seq, d_model)
```

```python
import functools
import math

import jax
import jax.numpy as jnp
import numpy as np
from jax import lax
from jax.experimental import pallas as pl
from jax.experimental.pallas import tpu as pltpu

F32 = jnp.float32
BF16 = jnp.bfloat16

EPS = 1e-6
NEG_INF = -1e30
LANE = 128
VMEM_LIMIT = 56 * 1024 * 1024

HEAD_DIM = 64
S5_GROUP = 16
S5_STATE = 64
S5_CHUNK = 16
S5_LANE_GROUPS = LANE // S5_GROUP
DIL_PATTERNS = ((128, 1), (512, 4), (2048, 16))
BAND_BLOCK = 128
NUM_BUCKETS = 32
MAX_DISTANCE = 2048

TOKEN_TILE = 512
ATTN_TILE = 512
ATTN_HEADS_PER_STEP = 2
ONES_ROWS = 16
LOG2E = math.log2(math.e)
S5_ROW_TILE = 256
FFN_CHUNK = 256


def _params(*sem):
    return pltpu.CompilerParams(dimension_semantics=sem, vmem_limit_bytes=VMEM_LIMIT)


def _resident(shape):
    return pl.BlockSpec(shape, lambda *_: (0,) * len(shape), pipeline_mode=pl.Buffered(1))


def _rms_scale(x, g):
    inv = lax.rsqrt(jnp.mean(x * x, axis=-1, keepdims=True) + EPS)
    return (x * inv) * g


def _sigmoid(x):
    return 1.0 / (1.0 + jnp.exp(-x))


def _gelu_tanh(x):
    return 0.5 * x * (1.0 + jnp.tanh(math.sqrt(2.0 / math.pi) * (x + 0.044715 * (x * x * x))))


def _bucket_np(dist):
    max_exact = NUM_BUCKETS // 2
    d = np.maximum(dist, 0)
    scaled = (np.log(np.maximum(d, 1).astype(np.float64) / max_exact)
              / math.log(MAX_DISTANCE / max_exact) * (NUM_BUCKETS - max_exact))
    large = np.minimum(max_exact + scaled.astype(np.int64), NUM_BUCKETS - 1)
    return np.where(d < max_exact, d, large)


def _lookup(table, bucket):
    idx = jnp.asarray(bucket.astype(np.int8))[None]
    expand = (slice(None),) + (None,) * bucket.ndim
    out = jnp.zeros((table.shape[1],) + bucket.shape, F32)
    for b in np.unique(bucket):
        out = jnp.where(idx == b, table[int(b)].astype(F32)[expand], out)
    return out


def _even_in_kernel(x_ref, g_ref, w_ref, u_ref, q_ref, k_ref, v_ref, *, width, scale):
    h = _rms_scale(x_ref[...], g_ref[...]).astype(BF16)
    u_ref[...] = jnp.dot(h, w_ref[:, 0:width], preferred_element_type=F32)
    q_ref[...] = (jnp.dot(h, w_ref[:, width:2 * width], preferred_element_type=F32) * scale).astype(BF16)
    k_ref[...] = jnp.dot(h, w_ref[:, 2 * width:3 * width], preferred_element_type=F32).astype(BF16)
    v_ref[...] = jnp.dot(h, w_ref[:, 3 * width:4 * width], preferred_element_type=F32).astype(BF16)


def _even_in_proj(x, g, w):
    n, d = x.shape
    width = w.shape[1] // 4
    tm = TOKEN_TILE
    row = lambda i: (i, 0)
    return pl.pallas_call(
        functools.partial(_even_in_kernel, width=width, scale=HEAD_DIM ** -0.5),
        grid=(n // tm,),
        in_specs=[pl.BlockSpec((tm, d), row), _resident((1, d)), _resident(w.shape)],
        out_specs=[pl.BlockSpec((tm, width), row),
                   pl.BlockSpec((tm, width), row), pl.BlockSpec((tm, width), row),
                   pl.BlockSpec((tm, width), row)],
        out_shape=[jax.ShapeDtypeStruct((n, width), F32),
                   jax.ShapeDtypeStruct((n, width), BF16),
                   jax.ShapeDtypeStruct((n, width), BF16),
                   jax.ShapeDtypeStruct((n, width), BF16)],
        compiler_params=_params("parallel"),
        name="even_in_proj",
    )(x, g, w)


_NT = (((1,), (1,)), ((), ()))


def _odd_in_kernel(x_ref, g_ref, wq_ref, wk_ref, wv_ref, qt_ref, k_ref, vt_ref, *, scale):
    h = _rms_scale(x_ref[...], g_ref[...]).astype(BF16)
    heads = qt_ref.shape[1]
    tm = h.shape[0]
    qt = lax.dot_general(wq_ref[...], h, _NT, preferred_element_type=F32) * scale
    qt_ref[0, :, 0] = qt.astype(BF16).reshape(heads, LANE, tm)
    k_ref[...] = jnp.dot(h, wk_ref[...], preferred_element_type=F32).astype(BF16)
    vt = lax.dot_general(wv_ref[...], h, _NT, preferred_element_type=F32)
    vt_ref[0, :, 0, 0:LANE, :] = vt.astype(BF16).reshape(heads, LANE, tm)
    vt_ref[0, :, 0, LANE:, :] = jnp.ones((heads, vt_ref.shape[3] - LANE, tm), BF16)


def _odd_in_proj(x, g, w, batch):
    n, d = x.shape
    width = w.shape[1] // 3
    heads = width // LANE
    tm = ATTN_TILE
    per_seq = n // batch // tm
    wq_t = w[:, 0:width].T
    wk = w[:, width:2 * width]
    wv_t = w[:, 2 * width:3 * width].T
    row = lambda i: (i, 0)
    tmap = lambda i: (i // per_seq, 0, i % per_seq, 0, 0)
    tshape = lambda rows: jax.ShapeDtypeStruct((batch, heads, per_seq, rows, tm), BF16)
    vrows = LANE + ONES_ROWS
    return pl.pallas_call(
        functools.partial(_odd_in_kernel, scale=HEAD_DIM ** -0.5 * LOG2E),
        grid=(n // tm,),
        in_specs=[pl.BlockSpec((tm, d), row), _resident((1, d)), _resident(wq_t.shape),
                  _resident(wk.shape), _resident(wv_t.shape)],
        out_specs=[pl.BlockSpec((1, heads, 1, LANE, tm), tmap), pl.BlockSpec((tm, width), row),
                   pl.BlockSpec((1, heads, 1, vrows, tm), tmap)],
        out_shape=[tshape(LANE), jax.ShapeDtypeStruct((n, width), BF16), tshape(vrows)],
        compiler_params=_params("parallel"),
        name="odd_in_proj",
    )(x, g, wq_t, wk, wv_t)


def _s5_matrices(lam_re, lam_im, log_dt, b_re, b_im, c_re, c_im, d_skip):
    hi = lax.Precision.HIGHEST
    L = S5_CHUNK
    G, P = lam_re.shape
    nj = G // S5_LANE_GROUPS
    dt = jnp.exp(log_dt)[:, None]
    steps = jnp.arange(L + 1, dtype=F32)[:, None, None]
    mag = jnp.exp(lam_re * dt * steps)
    ang = lam_im * dt * steps
    pr, pi = mag * jnp.cos(ang), mag * jnp.sin(ang)
    nr, ni = pr[1] - 1.0, pi[1]
    den = lam_re * lam_re + lam_im * lam_im
    cr = ((nr * lam_re + ni * lam_im) / den)[..., None]
    ci = ((ni * lam_re - nr * lam_im) / den)[..., None]
    bbr = cr * b_re - ci * b_im
    bbi = cr * b_im + ci * b_re
    car = c_re[None] * pr[:, :, None, :] - c_im[None] * pi[:, :, None, :]
    cai = c_re[None] * pi[:, :, None, :] + c_im[None] * pr[:, :, None, :]
    eye = jnp.eye(S5_LANE_GROUPS, dtype=F32)

    kg = (jnp.einsum('dgcp,gpe->dgce', car[:L], bbr, precision=hi)
          - jnp.einsum('dgcp,gpe->dgce', cai[:L], bbi, precision=hi))
    lag = np.arange(L)[None, :] - np.arange(L)[:, None]
    kt = jnp.where((lag >= 0)[:, :, None, None, None], kg[np.maximum(lag, 0)], 0.0)
    kt = kt.reshape(L, L, nj, S5_LANE_GROUPS, S5_GROUP, S5_GROUP)
    t_mat = jnp.einsum('stjgce,gh->jshetgc', kt, eye).reshape(nj, L * LANE, L * LANE)

    rev = np.arange(L - 1, -1, -1)
    abr = pr[rev][..., None] * bbr[None] - pi[rev][..., None] * bbi[None]
    abi = pr[rev][..., None] * bbi[None] + pi[rev][..., None] * bbr[None]
    ab = jnp.stack([abr, abi]).reshape(2, L, nj, S5_LANE_GROUPS, P, S5_GROUP)
    wst = jnp.einsum('asjgpe,gh->jsheagp', ab, eye).reshape(nj, L * LANE, 2 * S5_LANE_GROUPS * P)

    wo = jnp.stack([car[1:], -cai[1:]]).reshape(2, L, nj, S5_LANE_GROUPS, S5_GROUP, P)
    wo = jnp.einsum('atjgcp,gh->jahptgc', wo, eye).reshape(nj, 2 * S5_LANE_GROUPS * P, L * LANE)

    a_chunk = jnp.stack([pr[L], pi[L]]).reshape(2, nj, S5_LANE_GROUPS * P)
    a_chunk = a_chunk.transpose(1, 0, 2).reshape(nj, 1, 2 * S5_LANE_GROUPS * P)
    skip = jnp.tile(d_skip.reshape(nj, 1, LANE), (1, 1, L))
    return t_mat.astype(BF16), wst.astype(BF16), wo.astype(BF16), a_chunk, skip


def _s5_kernel(u_ref, t_ref, wst_ref, wo_ref, a_ref, skip_ref, y_ref, s_scr, h_scr, carry_scr, *, rows, half):
    @pl.when(pl.program_id(2) == 0)
    def _():
        carry_scr[...] = jnp.zeros_like(carry_scr)

    xf = jnp.concatenate([u_ref[pl.ds(t, rows, stride=S5_CHUNK), :] for t in range(S5_CHUNK)], axis=1)
    xb = xf.astype(BF16)
    s_scr[...] = jnp.dot(xb, wst_ref[0], preferred_element_type=F32)
    ar = a_ref[0, :, 0:half]
    ai = a_ref[0, :, half:2 * half]

    def step(i, carry):
        hr, hi = carry
        h_scr[pl.ds(i, 1), 0:half] = hr
        h_scr[pl.ds(i, 1), half:2 * half] = hi
        sr = s_scr[pl.ds(i, 1), 0:half]
        si = s_scr[pl.ds(i, 1), half:2 * half]
        return ar * hr - ai * hi + sr, ar * hi + ai * hr + si

    hr, hi = lax.fori_loop(0, rows, step, (carry_scr[:, 0:half], carry_scr[:, half:2 * half]), unroll=8)
    carry_scr[:, 0:half] = hr
    carry_scr[:, half:2 * half] = hi

    y = _gelu_tanh(jnp.dot(xb, t_ref[0], preferred_element_type=F32)
                   + jnp.dot(h_scr[...].astype(BF16), wo_ref[0], preferred_element_type=F32)
                   + skip_ref[0] * xf)
    for t in range(S5_CHUNK):
        y_ref[pl.ds(t, rows, stride=S5_CHUNK), :] = y[:, t * LANE:(t + 1) * LANE]


def _s5_mixer(u, mats, batch):
    t_mat, wst, wo, a_chunk, skip = mats
    n, width = u.shape
    nj = width // LANE
    cols = S5_CHUNK * LANE
    rows = S5_ROW_TILE
    tokens = rows * S5_CHUNK
    per_seq = n // batch // tokens
    nstate = wst.shape[2]
    tile = lambda j, b, k: (b * per_seq + k, j)
    per_j = lambda j, b, k: (j, 0, 0)
    return pl.pallas_call(
        functools.partial(_s5_kernel, rows=rows, half=nstate // 2),
        grid=(nj, batch, per_seq),
        in_specs=[pl.BlockSpec((tokens, LANE), tile),
                  pl.BlockSpec((1, cols, cols), per_j, pipeline_mode=pl.Buffered(1)),
                  pl.BlockSpec((1, cols, nstate), per_j, pipeline_mode=pl.Buffered(1)),
                  pl.BlockSpec((1, nstate, cols), per_j, pipeline_mode=pl.Buffered(1)),
                  pl.BlockSpec((1, 1, nstate), per_j),
                  pl.BlockSpec((1, 1, cols), per_j)],
        out_specs=pl.BlockSpec((tokens, LANE), tile),
        out_shape=jax.ShapeDtypeStruct((n, width), F32),
        scratch_shapes=[pltpu.VMEM((rows, nstate), F32), pltpu.VMEM((rows, nstate), F32),
                        pltpu.VMEM((1, nstate), F32)],
        compiler_params=_params("arbitrary", "arbitrary", "arbitrary"),
        name="s5_mixer",
    )(u, t_mat, wst, wo, a_chunk, skip)


def _dilated_bias(table, window, dilation):
    span = window // dilation
    qi = np.arange(BAND_BLOCK)[:, None]
    kc = np.arange(2 * BAND_BLOCK)[None, :]
    steps = BAND_BLOCK + qi - kc
    in_band = (steps >= 0) & (steps <= span)
    bias = _lookup(table, _bucket_np(np.clip(steps, 0, span) * dilation))
    rest = jnp.where(jnp.asarray(in_band)[None], bias, NEG_INF)
    first = jnp.where(jnp.asarray(in_band & (kc >= BAND_BLOCK))[None], bias, NEG_INF)
    return jnp.stack([first, rest])


def _dilated_kernel(q_ref, kp_ref, kc_ref, vp_ref, vc_ref, bias_ref, o_ref, lse_ref, *, heads):
    q = q_ref[0]
    k = jnp.concatenate([kp_ref[0], kc_ref[0]], axis=0)
    v = jnp.concatenate([vp_ref[0], vc_ref[0]], axis=0)
    lane = lax.broadcasted_iota(jnp.int32, (q.shape[0], LANE), 1)
    lse = jnp.zeros((q.shape[0], LANE), F32)
    outs = []
    for h in range(heads):
        sl = slice(h * HEAD_DIM, (h + 1) * HEAD_DIM)
        s = lax.dot_general(q[:, sl], k[:, sl], (((1,), (1,)), ((), ())),
                            preferred_element_type=F32) + bias_ref[0, h]
        m = jnp.max(s, axis=1, keepdims=True)
        p = jnp.exp(s - m)
        den = jnp.sum(p, axis=1, keepdims=True)
        outs.append(jnp.dot(p.astype(BF16), v[:, sl], preferred_element_type=F32) / den)
        lse = jnp.where(lane == h, m + jnp.log(den), lse)
    o_ref[0] = jnp.concatenate(outs, axis=1)
    lse_ref[0] = lse


def _dilated_pattern(q, k, v, bias, batch, dilation):
    n, w = q.shape
    heads = w // HEAD_DIM
    length = n // batch // dilation
    nblk = length // BAND_BLOCK
    view = lambda t: t.reshape(batch, length, dilation * w)
    cur = lambda b, r, i: (b, i, r)
    prev = lambda b, r, i: (b, jnp.maximum(i - 1, 0), r)
    blk = (1, BAND_BLOCK, w)
    o, lse = pl.pallas_call(
        functools.partial(_dilated_kernel, heads=heads),
        grid=(batch, dilation, nblk),
        in_specs=[pl.BlockSpec(blk, cur), pl.BlockSpec(blk, prev), pl.BlockSpec(blk, cur),
                  pl.BlockSpec(blk, prev), pl.BlockSpec(blk, cur),
                  pl.BlockSpec((1, heads, BAND_BLOCK, 2 * BAND_BLOCK),
                               lambda b, r, i: (jnp.minimum(i, 1), 0, 0, 0))],
        out_specs=[pl.BlockSpec(blk, cur), pl.BlockSpec((1, BAND_BLOCK, LANE), cur)],
        out_shape=[jax.ShapeDtypeStruct((batch, length, dilation * w), F32),
                   jax.ShapeDtypeStruct((batch, length, dilation * LANE), F32)],
        compiler_params=_params("parallel", "parallel", "parallel"),
        name=f"dilated_attn_d{dilation}",
    )(view(q), view(k), view(k), view(v), view(v), bias)
    return o.reshape(n, w), lse.reshape(n, LANE)


def _even_out_kernel(x_ref, ya_ref, o0_ref, o1_ref, o2_ref, l0_ref, l1_ref, l2_ref,
                     gw_ref, gb_ref, w_ref, out_ref, *, heads):
    ya = ya_ref[...]
    gate = jnp.dot(ya.astype(BF16), gw_ref[...], preferred_element_type=F32) + gb_ref[...]
    ya = ya * _sigmoid(gate)

    l0, l1, l2 = l0_ref[...], l1_ref[...], l2_ref[...]
    m = jnp.maximum(jnp.maximum(l0, l1), l2)
    e0, e1, e2 = jnp.exp(l0 - m), jnp.exp(l1 - m), jnp.exp(l2 - m)
    inv = 1.0 / (e0 + e1 + e2)
    alphas = (e0 * inv, e1 * inv, e2 * inv)
    width = o0_ref.shape[1]
    head_of_lane = lax.broadcasted_iota(jnp.int32, (x_ref.shape[0], width), 1) // HEAD_DIM
    yb = jnp.zeros((x_ref.shape[0], width), F32)
    for a, o_ref in zip(alphas, (o0_ref, o1_ref, o2_ref)):
        full = jnp.zeros_like(yb)
        for h in range(heads):
            full = jnp.where(head_of_lane == h, a[:, h:h + 1], full)
        yb = yb + full * o_ref[...]

    mixed = jnp.concatenate([ya, yb], axis=1).astype(BF16)
    out_ref[...] = x_ref[...] + jnp.dot(mixed, w_ref[...], preferred_element_type=F32)


def _even_out_proj(x, ya, outs, lses, glu_w, glu_b, w_out):
    n, d = x.shape
    width = outs[0].shape[1]
    tm = TOKEN_TILE
    row = lambda i: (i, 0)
    return pl.pallas_call(
        functools.partial(_even_out_kernel, heads=width // HEAD_DIM),
        grid=(n // tm,),
        in_specs=[pl.BlockSpec((tm, d), row),
                  pl.BlockSpec((tm, ya.shape[1]), row),
                  pl.BlockSpec((tm, width), row), pl.BlockSpec((tm, width), row),
                  pl.BlockSpec((tm, width), row),
                  pl.BlockSpec((tm, LANE), row), pl.BlockSpec((tm, LANE), row),
                  pl.BlockSpec((tm, LANE), row),
                  _resident(glu_w.shape), _resident(glu_b.shape), _resident(w_out.shape)],
        out_specs=pl.BlockSpec((tm, d), row),
        out_shape=jax.ShapeDtypeStruct((n, d), F32),
        compiler_params=_params("parallel"),
        name="even_out_proj",
    )(x, ya, *outs, *lses, glu_w, glu_b, w_out)


def _diff_bias(table, tile):
    first_const = int(np.argmax(_bucket_np(np.arange(4 * MAX_DISTANCE)) == NUM_BUCKETS - 1))
    n_near = -(-(first_const + tile - 1) // tile)
    kr = np.arange(tile)[:, None]
    qc = np.arange(tile)[None, :]
    dist = np.arange(n_near)[:, None, None] * tile + qc[None] - kr[None]
    assert n_near * tile - (tile - 1) >= first_const
    rel = (table - table[NUM_BUCKETS - 1][None, :]) * LOG2E
    return jnp.where(jnp.asarray(dist >= 0)[None], _lookup(rel, _bucket_np(dist)), NEG_INF)


def _diff_attn_kernel(lam_ref, qt_ref, k_ref, vt_ref, bias_ref, g_ref, o_ref, acc_scr, s_scr, p_scr,
                      *, tile, n_near_max, out_scale):
    qi = pl.program_id(2)
    heads = qt_ref.shape[1]
    row = lax.broadcasted_iota(jnp.int32, (LANE, tile), 0)
    q_maps = []
    for h in range(heads):
        qt = qt_ref[0, h, 0]
        zero = jnp.zeros_like(qt)
        q_maps.append((jnp.where(row < HEAD_DIM, qt, zero), jnp.where(row >= HEAD_DIM, qt, zero)))

    def keys(h, j):
        return k_ref[0, pl.ds(pl.multiple_of(j * tile, tile), tile), h * LANE:(h + 1) * LANE]

    def softmax(sa, m_prev):
        m_new = jnp.maximum(m_prev, jnp.max(sa, axis=0, keepdims=True))
        return m_new, jnp.exp2(m_prev - m_new), jnp.exp2(sa - m_new).astype(BF16)

    def accumulate(h, a, j, p, alpha, l_prev):
        r = jnp.dot(vt_ref[0, h, j], p, preferred_element_type=F32)
        acc_scr[h, a] = alpha * acc_scr[h, a] + r[0:LANE]
        return alpha * l_prev + r[LANE:LANE + 1]

    def step(j, carry, with_bias):
        out = []
        for h in range(heads):
            m1, l1, m2, l2, alpha2 = carry[5 * h:5 * h + 5]
            l2 = accumulate(h, 1, jnp.maximum(j - 1, 0), p_scr[h], alpha2, l2)
            s2 = jnp.dot(keys(h, j), q_maps[h][1], preferred_element_type=F32)
            s1 = s_scr[h]
            if with_bias:
                bias = bias_ref[h, qi - j]
                s1 = s1 + bias
                s2 = s2 + bias
            m1, alpha1, p1 = softmax(s1, m1)
            l1 = accumulate(h, 0, j, p1, alpha1, l1)
            m2, alpha2, p2 = softmax(s2, m2)
            p_scr[h] = p2
            s_scr[h] = jnp.dot(keys(h, jnp.minimum(j + 1, qi)), q_maps[h][0], preferred_element_type=F32)
            out += [m1, l1, m2, l2, alpha2]
        return tuple(out)

    acc_scr[...] = jnp.zeros_like(acc_scr)
    p_scr[...] = jnp.zeros_like(p_scr)
    for h in range(heads):
        s_scr[h] = jnp.dot(keys(h, 0), q_maps[h][0], preferred_element_type=F32)
    m0 = jnp.full((1, tile), 2.0 * NEG_INF, F32)
    l0 = jnp.zeros((1, tile), F32)
    carry = (m0, l0, m0, l0, jnp.ones((1, tile), F32)) * heads
    n_near = jnp.minimum(qi + 1, n_near_max)
    carry = lax.fori_loop(0, qi + 1 - n_near, functools.partial(step, with_bias=False), carry)
    carry = lax.fori_loop(qi + 1 - n_near, qi + 1, functools.partial(step, with_bias=True), carry)
    for h in range(heads):
        _, l1, _, l2, alpha2 = carry[5 * h:5 * h + 5]
        l2 = accumulate(h, 1, qi, p_scr[h], alpha2, l2)
        att = acc_scr[h, 0] * (1.0 / l1) - lam_ref[...] * (acc_scr[h, 1] * (1.0 / l2))
        inv = lax.rsqrt(jnp.mean(att * att, axis=0, keepdims=True) + EPS)
        out_t = ((att * inv) * g_ref[...]) * out_scale
        o_ref[0, :, h * LANE:(h + 1) * LANE] = out_t.T.astype(o_ref.dtype)


def _diff_attention(qt, k, vt, bias, lam, subln, out_scale):
    batch, heads, per_seq, _, tile = qt.shape
    n, w = k.shape
    seq = n // batch
    n_near = bias.shape[1]
    hp = ATTN_HEADS_PER_STEP
    const = lambda b, h, i: (0, 0)
    out = pl.pallas_call(
        functools.partial(_diff_attn_kernel, tile=tile, n_near_max=n_near, out_scale=out_scale),
        grid=(batch, heads // hp, per_seq),
        in_specs=[pl.BlockSpec((1, tile), const),
                  pl.BlockSpec((1, hp, 1, LANE, tile), lambda b, h, i: (b, h, i, 0, 0)),
                  pl.BlockSpec((1, seq, hp * LANE), lambda b, h, i: (b, 0, h)),
                  pl.BlockSpec((1, hp, per_seq, vt.shape[3], tile), lambda b, h, i: (b, h, 0, 0, 0)),
                  pl.BlockSpec((hp, n_near, tile, tile), lambda b, h, i: (h, 0, 0, 0)),
                  pl.BlockSpec((LANE, tile), const)],
        out_specs=pl.BlockSpec((1, tile, hp * LANE), lambda b, h, i: (b, i, h)),
        out_shape=jax.ShapeDtypeStruct((batch, seq, w), BF16),
        scratch_shapes=[pltpu.VMEM((hp, 2, LANE, tile), F32), pltpu.VMEM((hp, tile, tile), F32),
                        pltpu.VMEM((hp, tile, tile), BF16)],
        compiler_params=_params("parallel", "parallel", "arbitrary"),
        name="diff_attention",
    )(jnp.full((1, tile), lam, F32), qt, k.reshape(batch, seq, w), vt, bias,
      jnp.broadcast_to(subln[:, None], (LANE, tile)))
    return out.reshape(n, w)


def _odd_out_kernel(x_ref, a_ref, w_ref, out_ref):
    out_ref[...] = x_ref[...] + jnp.dot(a_ref[...], w_ref[...], preferred_element_type=F32)


def _odd_out_proj(x, att, w_out):
    n, d = x.shape
    tm = TOKEN_TILE
    row = lambda i: (i, 0)
    return pl.pallas_call(
        _odd_out_kernel,
        grid=(n // tm,),
        in_specs=[pl.BlockSpec((tm, d), row), pl.BlockSpec((tm, att.shape[1]), row),
                  _resident(w_out.shape)],
        out_specs=pl.BlockSpec((tm, d), row),
        out_shape=jax.ShapeDtypeStruct((n, d), F32),
        compiler_params=_params("parallel"),
        name="odd_out_proj",
    )(x, att, w_out)


def _ffn_kernel(x_ref, g_ref, wg_ref, wu_ref, wd_ref, gf_ref, out_ref, act_scr, *, chunk, final_norm):
    x = x_ref[...]
    h = _rms_scale(x, g_ref[...]).astype(BF16)
    for c in range(0, wg_ref.shape[1], chunk):
        gate = jnp.dot(h, wg_ref[:, c:c + chunk], preferred_element_type=F32)
        up = jnp.dot(h, wu_ref[:, c:c + chunk], preferred_element_type=F32)
        act_scr[:, c:c + chunk] = ((gate * _sigmoid(gate)) * up).astype(BF16)
    y = x + jnp.dot(act_scr[...], wd_ref[...], preferred_element_type=F32)
    if final_norm:
        y = _rms_scale(y, gf_ref[...])
    out_ref[...] = y


def _ffn(x, g, w_gate, w_up, w_down, g_final, final_norm):
    n, d = x.shape
    d_ff = w_gate.shape[1]
    tm = TOKEN_TILE
    row = lambda i: (i, 0)
    return pl.pallas_call(
        functools.partial(_ffn_kernel, chunk=FFN_CHUNK, final_norm=final_norm),
        grid=(n // tm,),
        in_specs=[pl.BlockSpec((tm, d), row), _resident((1, d)), _resident(w_gate.shape),
                  _resident(w_up.shape), _resident(w_down.shape), _resident((1, d))],
        out_specs=pl.BlockSpec((tm, d), row),
        out_shape=jax.ShapeDtypeStruct((n, d), F32),
        scratch_shapes=[pltpu.VMEM((tm, d_ff), BF16)],
        compiler_params=_params("parallel"),
        name="ffn_final" if final_norm else "ffn",
    )(x, g, w_gate, w_up, w_down, g_final)


def kernel(x, rel_bias, norm_mix, norm_ffn, norm_final, ffn_w_gate, ffn_w_up, ffn_w_down, even_w_in, even_w_out, s5_lambda_re, s5_lambda_im, s5_log_dt, s5_b_re, s5_b_im, s5_c_re, s5_c_im, s5_d, s5_glu_w, s5_glu_b, odd_w_in, odd_w_out, diff_lambda_q1, diff_lambda_k1, diff_lambda_q2, diff_lambda_k2, diff_subln):
    batch, seq, d_model = x.shape
    depth = norm_mix.shape[0]
    dil_heads = even_w_in.shape[2] // 4 // HEAD_DIM
    table_dil = rel_bias[:, :dil_heads]
    table_diff = rel_bias[:, dil_heads:]
    dil_biases = [_dilated_bias(table_dil, w, r) for w, r in DIL_PATTERNS]
    diff_bias = _diff_bias(table_diff, ATTN_TILE)

    xs = x.reshape(batch * seq, d_model)
    for layer in range(depth):
        g_mix = norm_mix[layer][None, :]
        if layer % 2 == 0:
            e = layer // 2
            u, q, k, v = _even_in_proj(xs, g_mix, even_w_in[e].astype(BF16))
            mats = _s5_matrices(s5_lambda_re[e], s5_lambda_im[e], s5_log_dt[e], s5_b_re[e], s5_b_im[e],
                                s5_c_re[e], s5_c_im[e], s5_d[e])
            ya = _s5_mixer(u, mats, batch)
            outs, lses = [], []
            for (_, r), bias in zip(DIL_PATTERNS, dil_biases):
                o, lse = _dilated_pattern(q, k, v, bias, batch, r)
                outs.append(o)
                lses.append(lse)
            xs = _even_out_proj(xs, ya, outs, lses, s5_glu_w[e].astype(BF16), s5_glu_b[e][None, :],
                                even_w_out[e].astype(BF16))
        else:
            o = layer // 2
            lam_init = 0.8 - 0.6 * math.exp(-0.3 * layer)
            lam = (jnp.exp(jnp.sum(diff_lambda_q1[o] * diff_lambda_k1[o]))
                   - jnp.exp(jnp.sum(diff_lambda_q2[o] * diff_lambda_k2[o])) + lam_init)
            qt, k, vt = _odd_in_proj(xs, g_mix, odd_w_in[o].astype(BF16), batch)
            att = _diff_attention(qt, k, vt, diff_bias, lam, diff_subln[o], 1.0 - lam_init)
            xs = _odd_out_proj(xs, att, odd_w_out[o].astype(BF16))
        xs = _ffn(xs, norm_ffn[layer][None, :], ffn_w_gate[layer].astype(BF16),
                  ffn_w_up[layer].astype(BF16), ffn_w_down[layer].astype(BF16),
                  norm_final[None, :], layer == depth - 1)
    return xs.reshape(batch, seq, d_model)
```

```python
import functools
import math

import jax
import jax.numpy as jnp
import numpy as np
from jax import lax
from jax.experimental import pallas as pl
from jax.experimental.pallas import tpu as pltpu

F32 = jnp.float32
BF16 = jnp.bfloat16

EPS = 1e-6
NEG_INF = -1e30
LANE = 128
VMEM_LIMIT = 56 * 1024 * 1024

HEAD_DIM = 64
S5_GROUP = 16
S5_STATE = 64
S5_CHUNK = 16
S5_LANE_GROUPS = LANE // S5_GROUP
DIL_PATTERNS = ((128, 1), (512, 4), (2048, 16))
BAND_BLOCK = 128
DIL_BLOCKS_PER_STEP = 4
NUM_BUCKETS = 32
MAX_DISTANCE = 2048

TOKEN_TILE = 512
ATTN_TILE = 512
ATTN_HEADS_PER_STEP = 2
ONES_ROWS = 16
LOG2E = math.log2(math.e)
S5_ROW_TILE = 256
FFN_CHUNK = 256


def _params(*sem):
    return pltpu.CompilerParams(dimension_semantics=sem, vmem_limit_bytes=VMEM_LIMIT)


def _resident(shape):
    return pl.BlockSpec(shape, lambda *_: (0,) * len(shape), pipeline_mode=pl.Buffered(1))


def _rms_scale(x, g):
    inv = lax.rsqrt(jnp.mean(x * x, axis=-1, keepdims=True) + EPS)
    return (x * inv) * g


def _sigmoid(x):
    return 1.0 / (1.0 + jnp.exp(-x))


def _gelu_tanh(x):
    return 0.5 * x * (1.0 + jnp.tanh(math.sqrt(2.0 / math.pi) * (x + 0.044715 * (x * x * x))))


def _bucket_np(dist):
    max_exact = NUM_BUCKETS // 2
    d = np.maximum(dist, 0)
    scaled = (np.log(np.maximum(d, 1).astype(np.float64) / max_exact)
              / math.log(MAX_DISTANCE / max_exact) * (NUM_BUCKETS - max_exact))
    large = np.minimum(max_exact + scaled.astype(np.int64), NUM_BUCKETS - 1)
    return np.where(d < max_exact, d, large)


def _lookup(table, bucket):
    idx = jnp.asarray(bucket.astype(np.int8))[None]
    expand = (slice(None),) + (None,) * bucket.ndim
    out = jnp.zeros((table.shape[1],) + bucket.shape, F32)
    for b in np.unique(bucket):
        out = jnp.where(idx == b, table[int(b)].astype(F32)[expand], out)
    return out


def _even_in_kernel(x_ref, g_ref, w_ref, u_ref, *rest, width, scale, dilations):
    qkv_refs, z_scr = rest[:-1], rest[-1]
    h = _rms_scale(x_ref[...], g_ref[...]).astype(BF16)
    u_ref[...] = jnp.dot(h, w_ref[:, 0:width], preferred_element_type=F32)
    zq = jnp.dot(h, w_ref[:, width:2 * width], preferred_element_type=F32) * scale
    zkv = jnp.dot(h, w_ref[:, 2 * width:4 * width], preferred_element_type=F32)
    nb = width // LANE
    for c in range(nb):
        z_scr[c] = zq[:, c * LANE:(c + 1) * LANE]
    for c in range(2 * nb):
        z_scr[nb + c] = zkv[:, c * LANE:(c + 1) * LANE]
    tm = x_ref.shape[0]
    for p, r in enumerate(dilations):
        for res in range(r):
            rows = pl.ds(res, tm // r, stride=r) if r > 1 else slice(None)
            for c in range(3 * nb):
                qkv_refs[3 * p + c // nb][0, res, :, (c % nb) * LANE:(c % nb + 1) * LANE] = (
                    z_scr[c, rows, :].astype(BF16))


def _even_in_proj(x, g, w, batch, dilations):
    n, d = x.shape
    width = w.shape[1] // 4
    tm = TOKEN_TILE
    seq = n // batch
    per_seq = seq // tm
    row = lambda i: (i, 0)
    out_specs = [pl.BlockSpec((tm, width), row)]
    out_shape = [jax.ShapeDtypeStruct((n, width), F32)]
    for r in dilations:
        out_specs += [pl.BlockSpec((1, r, tm // r, width), lambda i: (i // per_seq, 0, i % per_seq, 0))] * 3
        out_shape += [jax.ShapeDtypeStruct((batch, r, seq // r, width), BF16)] * 3
    return pl.pallas_call(
        functools.partial(_even_in_kernel, width=width, scale=HEAD_DIM ** -0.5 * LOG2E, dilations=dilations),
        grid=(n // tm,),
        in_specs=[pl.BlockSpec((tm, d), row), _resident((1, d)), _resident(w.shape)],
        out_specs=out_specs,
        out_shape=out_shape,
        scratch_shapes=[pltpu.VMEM((3 * width // LANE, tm, LANE), F32)],
        compiler_params=_params("parallel"),
        name="even_in_proj",
    )(x, g, w)


_NT = (((1,), (1,)), ((), ()))


def _odd_in_kernel(x_ref, g_ref, wq_ref, wk_ref, wv_ref, qt_ref, k_ref, vt_ref, *, scale):
    h = _rms_scale(x_ref[...], g_ref[...]).astype(BF16)
    heads = qt_ref.shape[1]
    tm = h.shape[0]
    qt = lax.dot_general(wq_ref[...], h, _NT, preferred_element_type=F32) * scale
    qt_ref[0, :, 0] = qt.astype(BF16).reshape(heads, LANE, tm)
    k_ref[...] = jnp.dot(h, wk_ref[...], preferred_element_type=F32).astype(BF16)
    vt = lax.dot_general(wv_ref[...], h, _NT, preferred_element_type=F32)
    vt_ref[0, :, 0, 0:LANE, :] = vt.astype(BF16).reshape(heads, LANE, tm)
    vt_ref[0, :, 0, LANE:, :] = jnp.ones((heads, vt_ref.shape[3] - LANE, tm), BF16)


def _odd_in_proj(x, g, w, batch):
    n, d = x.shape
    width = w.shape[1] // 3
    heads = width // LANE
    tm = ATTN_TILE
    per_seq = n // batch // tm
    wq_t = w[:, 0:width].T
    wk = w[:, width:2 * width]
    wv_t = w[:, 2 * width:3 * width].T
    row = lambda i: (i, 0)
    tmap = lambda i: (i // per_seq, 0, i % per_seq, 0, 0)
    tshape = lambda rows: jax.ShapeDtypeStruct((batch, heads, per_seq, rows, tm), BF16)
    vrows = LANE + ONES_ROWS
    return pl.pallas_call(
        functools.partial(_odd_in_kernel, scale=HEAD_DIM ** -0.5 * LOG2E),
        grid=(n // tm,),
        in_specs=[pl.BlockSpec((tm, d), row), _resident((1, d)), _resident(wq_t.shape),
                  _resident(wk.shape), _resident(wv_t.shape)],
        out_specs=[pl.BlockSpec((1, heads, 1, LANE, tm), tmap), pl.BlockSpec((tm, width), row),
                   pl.BlockSpec((1, heads, 1, vrows, tm), tmap)],
        out_shape=[tshape(LANE), jax.ShapeDtypeStruct((n, width), BF16), tshape(vrows)],
        compiler_params=_params("parallel"),
        name="odd_in_proj",
    )(x, g, wq_t, wk, wv_t)


def _s5_matrices(lam_re, lam_im, log_dt, b_re, b_im, c_re, c_im, d_skip):
    hi = lax.Precision.HIGHEST
    L = S5_CHUNK
    G, P = lam_re.shape
    nj = G // S5_LANE_GROUPS
    dt = jnp.exp(log_dt)[:, None]
    steps = jnp.arange(L + 1, dtype=F32)[:, None, None]
    mag = jnp.exp(lam_re * dt * steps)
    ang = lam_im * dt * steps
    pr, pi = mag * jnp.cos(ang), mag * jnp.sin(ang)
    nr, ni = pr[1] - 1.0, pi[1]
    den = lam_re * lam_re + lam_im * lam_im
    cr = ((nr * lam_re + ni * lam_im) / den)[..., None]
    ci = ((ni * lam_re - nr * lam_im) / den)[..., None]
    bbr = cr * b_re - ci * b_im
    bbi = cr * b_im + ci * b_re
    car = c_re[None] * pr[:, :, None, :] - c_im[None] * pi[:, :, None, :]
    cai = c_re[None] * pi[:, :, None, :] + c_im[None] * pr[:, :, None, :]
    lg = S5_LANE_GROUPS
    ncol, nst = L * LANE, 2 * lg * P
    row_group = np.arange(lg)[:, None]
    same_col = jnp.asarray(row_group == ((np.arange(ncol) // S5_GROUP) % lg)[None, :])
    same_st = jnp.asarray(row_group == ((np.arange(nst) // P) % lg)[None, :])

    kg = (jnp.einsum('dgcp,gpe->dgce', car[:L], bbr, precision=hi)
          - jnp.einsum('dgcp,gpe->dgce', cai[:L], bbi, precision=hi))
    lag = np.arange(L)[None, :] - np.arange(L)[:, None]
    kt = jnp.where((lag >= 0)[:, :, None, None, None], kg[np.maximum(lag, 0)], 0.0)
    kt = kt.reshape(L, L, nj, lg, S5_GROUP, S5_GROUP).transpose(2, 0, 5, 1, 3, 4).reshape(nj, L, S5_GROUP, ncol)
    t_mat = jnp.where(same_col[None, None, :, None, :], kt[:, :, None, :, :], 0.0).astype(BF16)
    t_mat = t_mat.reshape(nj, ncol, ncol)

    rev = np.arange(L - 1, -1, -1)
    abr = pr[rev][..., None] * bbr[None] - pi[rev][..., None] * bbi[None]
    abi = pr[rev][..., None] * bbi[None] + pi[rev][..., None] * bbr[None]
    ab = jnp.stack([abr, abi]).reshape(2, L, nj, lg, P, S5_GROUP).transpose(2, 1, 5, 0, 3, 4)
    ab = ab.reshape(nj, L, S5_GROUP, nst)
    wst = jnp.where(same_st[None, None, :, None, :], ab[:, :, None, :, :], 0.0).astype(BF16)
    wst = wst.reshape(nj, ncol, nst)

    wo = jnp.stack([car[1:], -cai[1:]]).reshape(2, L, nj, lg, S5_GROUP, P).transpose(2, 0, 5, 1, 3, 4)
    wo = wo.reshape(nj, 2, P, ncol)
    wo = jnp.where(same_col[None, None, :, None, :], wo[:, :, None, :, :], 0.0).astype(BF16)
    wo = wo.reshape(nj, nst, ncol)

    a_chunk = jnp.stack([pr[L], pi[L]]).reshape(2, nj, S5_LANE_GROUPS * P)
    a_chunk = a_chunk.transpose(1, 0, 2).reshape(nj, 1, 2 * S5_LANE_GROUPS * P)
    skip = jnp.tile(d_skip.reshape(nj, 1, LANE), (1, 1, L))
    return t_mat, wst, wo, a_chunk, skip


def _s5_kernel(u_ref, t_ref, wst_ref, wo_ref, a_ref, skip_ref, y_ref, s_scr, h_scr, carry_scr, *, rows, half):
    @pl.when(pl.program_id(2) == 0)
    def _():
        carry_scr[...] = jnp.zeros_like(carry_scr)

    xf = jnp.concatenate([u_ref[pl.ds(t, rows, stride=S5_CHUNK), :] for t in range(S5_CHUNK)], axis=1)
    xb = xf.astype(BF16)
    s_scr[...] = jnp.dot(xb, wst_ref[0], preferred_element_type=F32)
    ar = a_ref[0, :, 0:half]
    ai = a_ref[0, :, half:2 * half]

    def step(i, carry):
        hr, hi = carry
        h_scr[pl.ds(i, 1), 0:half] = hr
        h_scr[pl.ds(i, 1), half:2 * half] = hi
        sr = s_scr[pl.ds(i, 1), 0:half]
        si = s_scr[pl.ds(i, 1), half:2 * half]
        return ar * hr - ai * hi + sr, ar * hi + ai * hr + si

    hr, hi = lax.fori_loop(0, rows, step, (carry_scr[:, 0:half], carry_scr[:, half:2 * half]), unroll=8)
    carry_scr[:, 0:half] = hr
    carry_scr[:, half:2 * half] = hi

    y = _gelu_tanh(jnp.dot(xb, t_ref[0], preferred_element_type=F32)
                   + jnp.dot(h_scr[...].astype(BF16), wo_ref[0], preferred_element_type=F32)
                   + skip_ref[0] * xf)
    for t in range(S5_CHUNK):
        y_ref[pl.ds(t, rows, stride=S5_CHUNK), :] = y[:, t * LANE:(t + 1) * LANE]


def _s5_mixer(u, mats, batch):
    t_mat, wst, wo, a_chunk, skip = mats
    n, width = u.shape
    nj = width // LANE
    cols = S5_CHUNK * LANE
    rows = S5_ROW_TILE
    tokens = rows * S5_CHUNK
    per_seq = n // batch // tokens
    nstate = wst.shape[2]
    tile = lambda j, b, k: (b * per_seq + k, j)
    per_j = lambda j, b, k: (j, 0, 0)
    return pl.pallas_call(
        functools.partial(_s5_kernel, rows=rows, half=nstate // 2),
        grid=(nj, batch, per_seq),
        in_specs=[pl.BlockSpec((tokens, LANE), tile),
                  pl.BlockSpec((1, cols, cols), per_j, pipeline_mode=pl.Buffered(1)),
                  pl.BlockSpec((1, cols, nstate), per_j, pipeline_mode=pl.Buffered(1)),
                  pl.BlockSpec((1, nstate, cols), per_j, pipeline_mode=pl.Buffered(1)),
                  pl.BlockSpec((1, 1, nstate), per_j),
                  pl.BlockSpec((1, 1, cols), per_j)],
        out_specs=pl.BlockSpec((tokens, LANE), tile),
        out_shape=jax.ShapeDtypeStruct((n, width), F32),
        scratch_shapes=[pltpu.VMEM((rows, nstate), F32), pltpu.VMEM((rows, nstate), F32),
                        pltpu.VMEM((1, nstate), F32)],
        compiler_params=_params("arbitrary", "arbitrary", "arbitrary"),
        name="s5_mixer",
    )(u, t_mat, wst, wo, a_chunk, skip)


def _dilated_bias(table, window, dilation):
    span = window // dilation
    kc = np.arange(2 * BAND_BLOCK)[:, None]
    qi = np.arange(BAND_BLOCK)[None, :]
    steps = BAND_BLOCK + qi - kc
    in_band = (steps >= 0) & (steps <= span)
    bias = _lookup(table * LOG2E, _bucket_np(np.clip(steps, 0, span) * dilation))
    rest = jnp.where(jnp.asarray(in_band)[None], bias, NEG_INF)
    first = jnp.where(jnp.asarray(in_band & (kc >= BAND_BLOCK))[None], bias, NEG_INF)
    tiles = jnp.stack([first, rest])
    return jnp.concatenate([tiles[:, 0::2], tiles[:, 1::2]], axis=3)


def _dilated_kernel(q_ref, kp_ref, kc_ref, vp_ref, vc_ref, bias_ref, o_ref, lse_ref, vt_scr, ot_scr, lt_scr,
                    s_scr, *, group, blocks_per_residue):
    first = (pl.program_id(1) * group) % blocks_per_residue == 0
    first_variant = jnp.where(first, 0, 1)
    kfull = jnp.concatenate([kp_ref[0], kc_ref[0]], axis=0)
    vfull = jnp.concatenate([vp_ref[0], vc_ref[0]], axis=0)
    vt_scr[...] = vfull.astype(F32).T.astype(BF16)
    lt_scr[...] = jnp.zeros_like(lt_scr)
    low = lax.broadcasted_iota(jnp.int32, (BAND_BLOCK, LANE), 1) < HEAD_DIM
    pairs = q_ref.shape[2] // LANE
    units = [(g, hp) for g in range(group) for hp in range(pairs)]

    def logits(g, hp):
        rows = slice(g * BAND_BLOCK, (g + 1) * BAND_BLOCK)
        cols = slice(hp * LANE, (hp + 1) * LANE)
        qq = q_ref[0, rows, cols]
        zero = jnp.zeros_like(qq)
        wt = jnp.concatenate([jnp.where(low, qq, zero), jnp.where(low, zero, qq)], axis=0)
        return lax.dot_general(kfull[g * BAND_BLOCK:(g + 2) * BAND_BLOCK, cols], wt, _NT,
                               preferred_element_type=F32)

    s_scr[0] = logits(*units[0])
    for i, (g, hp) in enumerate(units):
        if i + 1 < len(units):
            s_scr[(i + 1) % 2] = logits(*units[i + 1])
        rows = slice(g * BAND_BLOCK, (g + 1) * BAND_BLOCK)
        s = s_scr[i % 2] + bias_ref[first_variant if g == 0 else 1, hp]
        m = jnp.max(s, axis=0, keepdims=True)
        p = jnp.exp2(s - m)
        den = jnp.sum(p, axis=0, keepdims=True)
        ot = jnp.dot(vt_scr[hp * LANE:(hp + 1) * LANE, g * BAND_BLOCK:(g + 2) * BAND_BLOCK], p.astype(BF16),
                     preferred_element_type=F32)
        inv = 1.0 / den
        ot_scr[hp * LANE:hp * LANE + HEAD_DIM, rows] = ot[0:HEAD_DIM, 0:BAND_BLOCK] * inv[:, 0:BAND_BLOCK]
        ot_scr[hp * LANE + HEAD_DIM:(hp + 1) * LANE, rows] = ot[HEAD_DIM:, BAND_BLOCK:] * inv[:, BAND_BLOCK:]
        lse2 = m + jnp.log2(den)
        lt_scr[2 * hp:2 * hp + 1, rows] = lse2[:, 0:BAND_BLOCK]
        lt_scr[2 * hp + 1:2 * hp + 2, rows] = lse2[:, BAND_BLOCK:]
    o_ref[0] = ot_scr[...].T.astype(o_ref.dtype)
    lse_ref[0] = lt_scr[...].T


def _dilated_pattern(q, k, v, bias):
    batch, dilation, length, w = q.shape
    group = DIL_BLOCKS_PER_STEP
    rows = group * BAND_BLOCK
    flat = lambda t: t.reshape(batch, dilation * length, w)
    cur = lambda b, i: (b, i, 0)
    prev = lambda b, i: (b, jnp.maximum(i * group - 1, 0), 0)
    o, lse = pl.pallas_call(
        functools.partial(_dilated_kernel, group=group, blocks_per_residue=length // BAND_BLOCK),
        grid=(batch, dilation * length // rows),
        in_specs=[pl.BlockSpec((1, rows, w), cur),
                  pl.BlockSpec((1, BAND_BLOCK, w), prev), pl.BlockSpec((1, rows, w), cur),
                  pl.BlockSpec((1, BAND_BLOCK, w), prev), pl.BlockSpec((1, rows, w), cur),
                  _resident(bias.shape)],
        out_specs=[pl.BlockSpec((1, rows, w), cur), pl.BlockSpec((1, rows, LANE), cur)],
        out_shape=[jax.ShapeDtypeStruct((batch, dilation * length, w), BF16),
                   jax.ShapeDtypeStruct((batch, dilation * length, LANE), F32)],
        scratch_shapes=[pltpu.VMEM((w, rows + BAND_BLOCK), BF16), pltpu.VMEM((w, rows), F32),
                        pltpu.VMEM((LANE, rows), F32),
                        pltpu.VMEM((2, 2 * BAND_BLOCK, 2 * BAND_BLOCK), F32)],
        compiler_params=_params("parallel", "parallel"),
        name=f"dilated_attn_d{dilation}",
    )(flat(q), flat(k), flat(k), flat(v), flat(v), bias)
    return o.reshape(batch, dilation, length, w), lse.reshape(batch, dilation, length, LANE)


def _even_out_kernel(x_ref, ya_ref, o0_ref, o1_ref, o2_ref, l0_ref, l1_ref, l2_ref,
                     gw_ref, gb_ref, w_ref, out_ref, o_scr, l_scr, *, heads):
    ya = ya_ref[...]
    gate = jnp.dot(ya.astype(BF16), gw_ref[...], preferred_element_type=F32) + gb_ref[...]
    ya = ya * _sigmoid(gate)

    tm = x_ref.shape[0]
    nb = o0_ref.shape[3] // LANE
    for p, (o_ref, l_ref) in enumerate(((o0_ref, l0_ref), (o1_ref, l1_ref), (o2_ref, l2_ref))):
        r = o_ref.shape[1]
        for res in range(r):
            dst = pl.ds(res, tm // r, stride=r) if r > 1 else slice(None)
            for c in range(nb):
                o_scr[p * nb + c, dst, :] = o_ref[0, res, :, c * LANE:(c + 1) * LANE].astype(F32)
            l_scr[p, dst, :] = l_ref[0, res]

    l0, l1, l2 = l_scr[0], l_scr[1], l_scr[2]
    m = jnp.maximum(jnp.maximum(l0, l1), l2)
    e0, e1, e2 = jnp.exp2(l0 - m), jnp.exp2(l1 - m), jnp.exp2(l2 - m)
    inv = 1.0 / (e0 + e1 + e2)
    alphas = (e0 * inv, e1 * inv, e2 * inv)
    low = lax.broadcasted_iota(jnp.int32, (tm, LANE), 1) < HEAD_DIM
    cols = []
    for c in range(nb):
        acc = jnp.zeros((tm, LANE), F32)
        for p, a in enumerate(alphas):
            weight = jnp.where(low, a[:, 2 * c:2 * c + 1], a[:, 2 * c + 1:2 * c + 2])
            acc = acc + weight * o_scr[p * nb + c]
        cols.append(acc)

    mixed = jnp.concatenate([ya] + cols, axis=1).astype(BF16)
    out_ref[...] = x_ref[...] + jnp.dot(mixed, w_ref[...], preferred_element_type=F32)


def _even_out_proj(x, ya, outs, lses, glu_w, glu_b, w_out):
    n, d = x.shape
    batch = outs[0].shape[0]
    width = outs[0].shape[3]
    tm = TOKEN_TILE
    per_seq = n // batch // tm
    row = lambda i: (i, 0)
    grouped = lambda t: pl.BlockSpec((1, t.shape[1], tm // t.shape[1], t.shape[3]),
                                     lambda i: (i // per_seq, 0, i % per_seq, 0))
    return pl.pallas_call(
        functools.partial(_even_out_kernel, heads=width // HEAD_DIM),
        grid=(n // tm,),
        in_specs=[pl.BlockSpec((tm, d), row),
                  pl.BlockSpec((tm, ya.shape[1]), row),
                  *[grouped(t) for t in outs], *[grouped(t) for t in lses],
                  _resident(glu_w.shape), _resident(glu_b.shape), _resident(w_out.shape)],
        out_specs=pl.BlockSpec((tm, d), row),
        out_shape=jax.ShapeDtypeStruct((n, d), F32),
        scratch_shapes=[pltpu.VMEM((len(outs) * width // LANE, tm, LANE), F32),
                        pltpu.VMEM((len(outs), tm, LANE), F32)],
        compiler_params=_params("parallel"),
        name="even_out_proj",
    )(x, ya, *outs, *lses, glu_w, glu_b, w_out)


def _diff_bias(table, tile):
    first_const = int(np.argmax(_bucket_np(np.arange(4 * MAX_DISTANCE)) == NUM_BUCKETS - 1))
    n_near = -(-(first_const + tile - 1) // tile)
    kr = np.arange(tile)[:, None]
    qc = np.arange(tile)[None, :]
    dist = np.arange(n_near)[:, None, None] * tile + qc[None] - kr[None]
    assert n_near * tile - (tile - 1) >= first_const
    rel = (table - table[NUM_BUCKETS - 1][None, :]) * LOG2E
    return jnp.where(jnp.asarray(dist >= 0)[None], _lookup(rel, _bucket_np(dist)), NEG_INF)


def _diff_attn_kernel(lam_ref, qt_ref, k_ref, vt_ref, bias_ref, g_ref, o_ref, acc_scr, s_scr, p_scr,
                      *, tile, n_near_max, out_scale):
    qi = pl.program_id(2)
    heads = qt_ref.shape[1]
    row = lax.broadcasted_iota(jnp.int32, (LANE, tile), 0)
    q_maps = []
    for h in range(heads):
        qt = qt_ref[0, h, 0]
        zero = jnp.zeros_like(qt)
        q_maps.append((jnp.where(row < HEAD_DIM, qt, zero), jnp.where(row >= HEAD_DIM, qt, zero)))

    def keys(h, j):
        return k_ref[0, pl.ds(pl.multiple_of(j * tile, tile), tile), h * LANE:(h + 1) * LANE]

    def softmax(sa, m_prev):
        m_new = jnp.maximum(m_prev, jnp.max(sa, axis=0, keepdims=True))
        return m_new, jnp.exp2(m_prev - m_new), jnp.exp2(sa - m_new).astype(BF16)

    def accumulate(h, a, j, p, alpha, l_prev):
        r = jnp.dot(vt_ref[0, h, j], p, preferred_element_type=F32)
        acc_scr[h, a] = alpha * acc_scr[h, a] + r[0:LANE]
        return alpha * l_prev + r[LANE:LANE + 1]

    def step(j, carry, with_bias):
        out = []
        for h in range(heads):
            m1, l1, m2, l2, alpha2 = carry[5 * h:5 * h + 5]
            l2 = accumulate(h, 1, jnp.maximum(j - 1, 0), p_scr[h], alpha2, l2)
            s2 = jnp.dot(keys(h, j), q_maps[h][1], preferred_element_type=F32)
            s1 = s_scr[h]
            if with_bias:
                bias = bias_ref[h, qi - j]
                s1 = s1 + bias
                s2 = s2 + bias
            m1, alpha1, p1 = softmax(s1, m1)
            l1 = accumulate(h, 0, j, p1, alpha1, l1)
            m2, alpha2, p2 = softmax(s2, m2)
            p_scr[h] = p2
            s_scr[h] = jnp.dot(keys(h, jnp.minimum(j + 1, qi)), q_maps[h][0], preferred_element_type=F32)
            out += [m1, l1, m2, l2, alpha2]
        return tuple(out)

    acc_scr[...] = jnp.zeros_like(acc_scr)
    p_scr[...] = jnp.zeros_like(p_scr)
    for h in range(heads):
        s_scr[h] = jnp.dot(keys(h, 0), q_maps[h][0], preferred_element_type=F32)
    m0 = jnp.full((1, tile), 2.0 * NEG_INF, F32)
    l0 = jnp.zeros((1, tile), F32)
    carry = (m0, l0, m0, l0, jnp.ones((1, tile), F32)) * heads
    n_near = jnp.minimum(qi + 1, n_near_max)
    carry = lax.fori_loop(0, qi + 1 - n_near, functools.partial(step, with_bias=False), carry)
    carry = lax.fori_loop(qi + 1 - n_near, qi + 1, functools.partial(step, with_bias=True), carry)
    for h in range(heads):
        _, l1, _, l2, alpha2 = carry[5 * h:5 * h + 5]
        l2 = accumulate(h, 1, qi, p_scr[h], alpha2, l2)
        att = acc_scr[h, 0] * (1.0 / l1) - lam_ref[...] * (acc_scr[h, 1] * (1.0 / l2))
        inv = lax.rsqrt(jnp.mean(att * att, axis=0, keepdims=True) + EPS)
        out_t = ((att * inv) * g_ref[...]) * out_scale
        o_ref[0, :, h * LANE:(h + 1) * LANE] = out_t.T.astype(o_ref.dtype)


def _diff_attention(qt, k, vt, bias, lam, subln, out_scale):
    batch, heads, per_seq, _, tile = qt.shape
    n, w = k.shape
    seq = n // batch
    n_near = bias.shape[1]
    hp = ATTN_HEADS_PER_STEP
    const = lambda b, h, i: (0, 0)
    out = pl.pallas_call(
        functools.partial(_diff_attn_kernel, tile=tile, n_near_max=n_near, out_scale=out_scale),
        grid=(batch, heads // hp, per_seq),
        in_specs=[pl.BlockSpec((1, tile), const),
                  pl.BlockSpec((1, hp, 1, LANE, tile), lambda b, h, i: (b, h, i, 0, 0)),
                  pl.BlockSpec((1, seq, hp * LANE), lambda b, h, i: (b, 0, h)),
                  pl.BlockSpec((1, hp, per_seq, vt.shape[3], tile), lambda b, h, i: (b, h, 0, 0, 0)),
                  pl.BlockSpec((hp, n_near, tile, tile), lambda b, h, i: (h, 0, 0, 0)),
                  pl.BlockSpec((LANE, tile), const)],
        out_specs=pl.BlockSpec((1, tile, hp * LANE), lambda b, h, i: (b, i, h)),
        out_shape=jax.ShapeDtypeStruct((batch, seq, w), BF16),
        scratch_shapes=[pltpu.VMEM((hp, 2, LANE, tile), F32), pltpu.VMEM((hp, tile, tile), F32),
                        pltpu.VMEM((hp, tile, tile), BF16)],
        compiler_params=_params("parallel", "parallel", "arbitrary"),
        name="diff_attention",
    )(jnp.full((1, tile), lam, F32), qt, k.reshape(batch, seq, w), vt, bias,
      jnp.broadcast_to(subln[:, None], (LANE, tile)))
    return out.reshape(n, w)


def _odd_out_kernel(x_ref, a_ref, w_ref, out_ref):
    out_ref[...] = x_ref[...] + jnp.dot(a_ref[...], w_ref[...], preferred_element_type=F32)


def _odd_out_proj(x, att, w_out):
    n, d = x.shape
    tm = TOKEN_TILE
    row = lambda i: (i, 0)
    return pl.pallas_call(
        _odd_out_kernel,
        grid=(n // tm,),
        in_specs=[pl.BlockSpec((tm, d), row), pl.BlockSpec((tm, att.shape[1]), row),
                  _resident(w_out.shape)],
        out_specs=pl.BlockSpec((tm, d), row),
        out_shape=jax.ShapeDtypeStruct((n, d), F32),
        compiler_params=_params("parallel"),
        name="odd_out_proj",
    )(x, att, w_out)


def _ffn_kernel(x_ref, g_ref, wg_ref, wu_ref, wd_ref, gf_ref, out_ref, act_scr, *, chunk, final_norm):
    x = x_ref[...]
    h = _rms_scale(x, g_ref[...]).astype(BF16)
    for c in range(0, wg_ref.shape[1], chunk):
        gate = jnp.dot(h, wg_ref[:, c:c + chunk], preferred_element_type=F32)
        up = jnp.dot(h, wu_ref[:, c:c + chunk], preferred_element_type=F32)
        act_scr[:, c:c + chunk] = ((gate * _sigmoid(gate)) * up).astype(BF16)
    y = x + jnp.dot(act_scr[...], wd_ref[...], preferred_element_type=F32)
    if final_norm:
        y = _rms_scale(y, gf_ref[...])
    out_ref[...] = y


def _ffn(x, g, w_gate, w_up, w_down, g_final, final_norm):
    n, d = x.shape
    d_ff = w_gate.shape[1]
    tm = TOKEN_TILE
    row = lambda i: (i, 0)
    return pl.pallas_call(
        functools.partial(_ffn_kernel, chunk=FFN_CHUNK, final_norm=final_norm),
        grid=(n // tm,),
        in_specs=[pl.BlockSpec((tm, d), row), _resident((1, d)), _resident(w_gate.shape),
                  _resident(w_up.shape), _resident(w_down.shape), _resident((1, d))],
        out_specs=pl.BlockSpec((tm, d), row),
        out_shape=jax.ShapeDtypeStruct((n, d), F32),
        scratch_shapes=[pltpu.VMEM((tm, d_ff), BF16)],
        compiler_params=_params("parallel"),
        name="ffn_final" if final_norm else "ffn",
    )(x, g, w_gate, w_up, w_down, g_final)


def kernel(x, rel_bias, norm_mix, norm_ffn, norm_final, ffn_w_gate, ffn_w_up, ffn_w_down, even_w_in, even_w_out, s5_lambda_re, s5_lambda_im, s5_log_dt, s5_b_re, s5_b_im, s5_c_re, s5_c_im, s5_d, s5_glu_w, s5_glu_b, odd_w_in, odd_w_out, diff_lambda_q1, diff_lambda_k1, diff_lambda_q2, diff_lambda_k2, diff_subln):
    batch, seq, d_model = x.shape
    depth = norm_mix.shape[0]
    dil_heads = even_w_in.shape[2] // 4 // HEAD_DIM
    table_dil = rel_bias[:, :dil_heads]
    table_diff = rel_bias[:, dil_heads:]
    dil_biases = [_dilated_bias(table_dil, w, r) for w, r in DIL_PATTERNS]
    diff_bias = _diff_bias(table_diff, ATTN_TILE)

    xs = x.reshape(batch * seq, d_model)
    for layer in range(depth):
        g_mix = norm_mix[layer][None, :]
        if layer % 2 == 0:
            e = layer // 2
            u, *qkv = _even_in_proj(xs, g_mix, even_w_in[e].astype(BF16), batch,
                                    tuple(r for _, r in DIL_PATTERNS))
            mats = _s5_matrices(s5_lambda_re[e], s5_lambda_im[e], s5_log_dt[e], s5_b_re[e], s5_b_im[e],
                                s5_c_re[e], s5_c_im[e], s5_d[e])
            ya = _s5_mixer(u, mats, batch)
            outs, lses = [], []
            for p, bias in enumerate(dil_biases):
                o, lse = _dilated_pattern(*qkv[3 * p:3 * p + 3], bias)
                outs.append(o)
                lses.append(lse)
            xs = _even_out_proj(xs, ya, outs, lses, s5_glu_w[e].astype(BF16), s5_glu_b[e][None, :],
                                even_w_out[e].astype(BF16))
        else:
            o = layer // 2
            lam_init = 0.8 - 0.6 * math.exp(-0.3 * layer)
            lam = (jnp.exp(jnp.sum(diff_lambda_q1[o] * diff_lambda_k1[o]))
                   - jnp.exp(jnp.sum(diff_lambda_q2[o] * diff_lambda_k2[o])) + lam_init)
            qt, k, vt = _odd_in_proj(xs, g_mix, odd_w_in[o].astype(BF16), batch)
            att = _diff_attention(qt, k, vt, diff_bias, lam, diff_subln[o], 1.0 - lam_init)
            xs = _odd_out_proj(xs, att, odd_w_out[o].astype(BF16))
        xs = _ffn(xs, norm_ffn[layer][None, :], ffn_w_gate[layer].astype(BF16),
                  ffn_w_up[layer].astype(BF16), ffn_w_down[layer].astype(BF16),
                  norm_final[None, :], layer == depth - 1)
    return xs.reshape(batch, seq, d_model)
```

```python
import functools
import math

import jax
import jax.numpy as jnp
import numpy as np
from jax import lax
from jax.experimental import pallas as pl
from jax.experimental.pallas import tpu as pltpu

F32 = jnp.float32
BF16 = jnp.bfloat16

EPS = 1e-6
NEG_INF = -1e30
LANE = 128
VMEM_LIMIT = 56 * 1024 * 1024

HEAD_DIM = 64
S5_GROUP = 16
S5_STATE = 64
S5_CHUNK = 16
S5_LANE_GROUPS = LANE // S5_GROUP
DIL_PATTERNS = ((128, 1), (512, 4), (2048, 16))
BAND_BLOCK = 128
DIL_BLOCKS_PER_STEP = 4
NUM_BUCKETS = 32
MAX_DISTANCE = 2048

TOKEN_TILE = 512
ATTN_TILE = 512
ATTN_HEADS_PER_STEP = 4
ONES_ROWS = 16
LOG2E = math.log2(math.e)
S5_ROW_TILE = 256
FFN_CHUNK = 256


def _params(*sem):
    return pltpu.CompilerParams(dimension_semantics=sem, vmem_limit_bytes=VMEM_LIMIT)


def _resident(shape):
    return pl.BlockSpec(shape, lambda *_: (0,) * len(shape), pipeline_mode=pl.Buffered(1))


def _rms_scale(x, g):
    inv = lax.rsqrt(jnp.mean(x * x, axis=-1, keepdims=True) + EPS)
    return (x * inv) * g


def _sigmoid(x):
    return 1.0 / (1.0 + jnp.exp(-x))


def _gelu_tanh(x):
    return 0.5 * x * (1.0 + jnp.tanh(math.sqrt(2.0 / math.pi) * (x + 0.044715 * (x * x * x))))


def _bucket_np(dist):
    max_exact = NUM_BUCKETS // 2
    d = np.maximum(dist, 0)
    scaled = (np.log(np.maximum(d, 1).astype(np.float64) / max_exact)
              / math.log(MAX_DISTANCE / max_exact) * (NUM_BUCKETS - max_exact))
    large = np.minimum(max_exact + scaled.astype(np.int64), NUM_BUCKETS - 1)
    return np.where(d < max_exact, d, large)


def _lookup(table, bucket):
    idx = jnp.asarray(bucket.astype(np.int8))[None]
    expand = (slice(None),) + (None,) * bucket.ndim
    out = jnp.zeros((table.shape[1],) + bucket.shape, F32)
    for b in np.unique(bucket):
        out = jnp.where(idx == b, table[int(b)].astype(F32)[expand], out)
    return out


def _even_in_kernel(x_ref, g_ref, w_ref, u_ref, *rest, width, scale, dilations):
    qkv_refs, z_scr = rest[:-1], rest[-1]
    h = _rms_scale(x_ref[...], g_ref[...]).astype(BF16)
    u_ref[...] = jnp.dot(h, w_ref[:, 0:width], preferred_element_type=F32)
    zq = jnp.dot(h, w_ref[:, width:2 * width], preferred_element_type=F32) * scale
    zkv = jnp.dot(h, w_ref[:, 2 * width:4 * width], preferred_element_type=F32)
    nb = width // LANE
    for c in range(nb):
        z_scr[c] = zq[:, c * LANE:(c + 1) * LANE]
    for c in range(2 * nb):
        z_scr[nb + c] = zkv[:, c * LANE:(c + 1) * LANE]
    tm = x_ref.shape[0]
    for p, r in enumerate(dilations):
        for res in range(r):
            rows = pl.ds(res, tm // r, stride=r) if r > 1 else slice(None)
            for c in range(3 * nb):
                qkv_refs[3 * p + c // nb][0, res, :, (c % nb) * LANE:(c % nb + 1) * LANE] = (
                    z_scr[c, rows, :].astype(BF16))


def _even_in_proj(x, g, w, batch, dilations):
    n, d = x.shape
    width = w.shape[1] // 4
    tm = TOKEN_TILE
    seq = n // batch
    per_seq = seq // tm
    row = lambda i: (i, 0)
    out_specs = [pl.BlockSpec((tm, width), row)]
    out_shape = [jax.ShapeDtypeStruct((n, width), F32)]
    for r in dilations:
        out_specs += [pl.BlockSpec((1, r, tm // r, width), lambda i: (i // per_seq, 0, i % per_seq, 0))] * 3
        out_shape += [jax.ShapeDtypeStruct((batch, r, seq // r, width), BF16)] * 3
    return pl.pallas_call(
        functools.partial(_even_in_kernel, width=width, scale=HEAD_DIM ** -0.5 * LOG2E, dilations=dilations),
        grid=(n // tm,),
        in_specs=[pl.BlockSpec((tm, d), row), _resident((1, d)), _resident(w.shape)],
        out_specs=out_specs,
        out_shape=out_shape,
        scratch_shapes=[pltpu.VMEM((3 * width // LANE, tm, LANE), F32)],
        compiler_params=_params("parallel"),
        name="even_in_proj",
    )(x, g, w)


_NT = (((1,), (1,)), ((), ()))


def _odd_in_kernel(x_ref, g_ref, wq_ref, wk_ref, wv_ref, qt_ref, k_ref, vt_ref, *, scale):
    h = _rms_scale(x_ref[...], g_ref[...]).astype(BF16)
    heads = qt_ref.shape[1]
    tm = h.shape[0]
    qt = lax.dot_general(wq_ref[...], h, _NT, preferred_element_type=F32) * scale
    qt_ref[0, :, 0] = qt.astype(BF16).reshape(heads, LANE, tm)
    k_ref[...] = jnp.dot(h, wk_ref[...], preferred_element_type=F32).astype(BF16)
    vt = lax.dot_general(wv_ref[...], h, _NT, preferred_element_type=F32)
    vt_ref[0, :, 0, 0:LANE, :] = vt.astype(BF16).reshape(heads, LANE, tm)
    vt_ref[0, :, 0, LANE:, :] = jnp.ones((heads, vt_ref.shape[3] - LANE, tm), BF16)


def _odd_in_proj(x, g, w, batch):
    n, d = x.shape
    width = w.shape[1] // 3
    heads = width // LANE
    tm = ATTN_TILE
    per_seq = n // batch // tm
    wq_t = w[:, 0:width].T
    wk = w[:, width:2 * width]
    wv_t = w[:, 2 * width:3 * width].T
    row = lambda i: (i, 0)
    tmap = lambda i: (i // per_seq, 0, i % per_seq, 0, 0)
    tshape = lambda rows: jax.ShapeDtypeStruct((batch, heads, per_seq, rows, tm), BF16)
    vrows = LANE + ONES_ROWS
    return pl.pallas_call(
        functools.partial(_odd_in_kernel, scale=HEAD_DIM ** -0.5 * LOG2E),
        grid=(n // tm,),
        in_specs=[pl.BlockSpec((tm, d), row), _resident((1, d)), _resident(wq_t.shape),
                  _resident(wk.shape), _resident(wv_t.shape)],
        out_specs=[pl.BlockSpec((1, heads, 1, LANE, tm), tmap), pl.BlockSpec((tm, width), row),
                   pl.BlockSpec((1, heads, 1, vrows, tm), tmap)],
        out_shape=[tshape(LANE), jax.ShapeDtypeStruct((n, width), BF16), tshape(vrows)],
        compiler_params=_params("parallel"),
        name="odd_in_proj",
    )(x, g, wq_t, wk, wv_t)


def _s5_matrices(lam_re, lam_im, log_dt, b_re, b_im, c_re, c_im, d_skip):
    hi = lax.Precision.HIGHEST
    L = S5_CHUNK
    G, P = lam_re.shape
    nj = G // S5_LANE_GROUPS
    dt = jnp.exp(log_dt)[:, None]
    steps = jnp.arange(L + 1, dtype=F32)[:, None, None]
    mag = jnp.exp(lam_re * dt * steps)
    ang = lam_im * dt * steps
    pr, pi = mag * jnp.cos(ang), mag * jnp.sin(ang)
    nr, ni = pr[1] - 1.0, pi[1]
    den = lam_re * lam_re + lam_im * lam_im
    cr = ((nr * lam_re + ni * lam_im) / den)[..., None]
    ci = ((ni * lam_re - nr * lam_im) / den)[..., None]
    bbr = cr * b_re - ci * b_im
    bbi = cr * b_im + ci * b_re
    car = c_re[None] * pr[:, :, None, :] - c_im[None] * pi[:, :, None, :]
    cai = c_re[None] * pi[:, :, None, :] + c_im[None] * pr[:, :, None, :]
    lg = S5_LANE_GROUPS
    ncol, nst = L * LANE, 2 * lg * P
    row_group = np.arange(lg)[:, None]
    same_col = jnp.asarray(row_group == ((np.arange(ncol) // S5_GROUP) % lg)[None, :])
    same_st = jnp.asarray(row_group == ((np.arange(nst) // P) % lg)[None, :])

    kg = (jnp.einsum('dgcp,gpe->dgce', car[:L], bbr, precision=hi)
          - jnp.einsum('dgcp,gpe->dgce', cai[:L], bbi, precision=hi))
    lag = np.arange(L)[None, :] - np.arange(L)[:, None]
    kt = jnp.where((lag >= 0)[:, :, None, None, None], kg[np.maximum(lag, 0)], 0.0)
    kt = kt.reshape(L, L, nj, lg, S5_GROUP, S5_GROUP).transpose(2, 0, 5, 1, 3, 4).reshape(nj, L, S5_GROUP, ncol)
    t_mat = jnp.where(same_col[None, None, :, None, :], kt[:, :, None, :, :], 0.0).astype(BF16)
    t_mat = t_mat.reshape(nj, ncol, ncol)

    rev = np.arange(L - 1, -1, -1)
    abr = pr[rev][..., None] * bbr[None] - pi[rev][..., None] * bbi[None]
    abi = pr[rev][..., None] * bbi[None] + pi[rev][..., None] * bbr[None]
    ab = jnp.stack([abr, abi]).reshape(2, L, nj, lg, P, S5_GROUP).transpose(2, 1, 5, 0, 3, 4)
    ab = ab.reshape(nj, L, S5_GROUP, nst)
    wst = jnp.where(same_st[None, None, :, None, :], ab[:, :, None, :, :], 0.0).astype(BF16)
    wst = wst.reshape(nj, ncol, nst)

    wo = jnp.stack([car[1:], -cai[1:]]).reshape(2, L, nj, lg, S5_GROUP, P).transpose(2, 0, 5, 1, 3, 4)
    wo = wo.reshape(nj, 2, P, ncol)
    wo = jnp.where(same_col[None, None, :, None, :], wo[:, :, None, :, :], 0.0).astype(BF16)
    wo = wo.reshape(nj, nst, ncol)

    a_chunk = jnp.stack([pr[L], pi[L]]).reshape(2, nj, S5_LANE_GROUPS * P)
    a_chunk = a_chunk.transpose(1, 0, 2).reshape(nj, 1, 2 * S5_LANE_GROUPS * P)
    skip = jnp.tile(d_skip.reshape(nj, 1, LANE), (1, 1, L))
    return t_mat, wst, wo, a_chunk, skip


def _s5_kernel(u_ref, t_ref, wst_ref, wo_ref, a_ref, skip_ref, y_ref, s_scr, h_scr, carry_scr, *, rows, half):
    @pl.when(pl.program_id(2) == 0)
    def _():
        carry_scr[...] = jnp.zeros_like(carry_scr)

    xf = jnp.concatenate([u_ref[pl.ds(t, rows, stride=S5_CHUNK), :] for t in range(S5_CHUNK)], axis=1)
    xb = xf.astype(BF16)
    s_scr[...] = jnp.dot(xb, wst_ref[0], preferred_element_type=F32)
    ar = a_ref[0, :, 0:half]
    ai = a_ref[0, :, half:2 * half]

    def step(i, carry):
        hr, hi = carry
        h_scr[pl.ds(i, 1), 0:half] = hr
        h_scr[pl.ds(i, 1), half:2 * half] = hi
        sr = s_scr[pl.ds(i, 1), 0:half]
        si = s_scr[pl.ds(i, 1), half:2 * half]
        return ar * hr - ai * hi + sr, ar * hi + ai * hr + si

    hr, hi = lax.fori_loop(0, rows, step, (carry_scr[:, 0:half], carry_scr[:, half:2 * half]), unroll=8)
    carry_scr[:, 0:half] = hr
    carry_scr[:, half:2 * half] = hi

    y = _gelu_tanh(jnp.dot(xb, t_ref[0], preferred_element_type=F32)
                   + jnp.dot(h_scr[...].astype(BF16), wo_ref[0], preferred_element_type=F32)
                   + skip_ref[0] * xf)
    for t in range(S5_CHUNK):
        y_ref[pl.ds(t, rows, stride=S5_CHUNK), :] = y[:, t * LANE:(t + 1) * LANE]


def _s5_mixer(u, mats, batch):
    t_mat, wst, wo, a_chunk, skip = mats
    n, width = u.shape
    nj = width // LANE
    cols = S5_CHUNK * LANE
    rows = S5_ROW_TILE
    tokens = rows * S5_CHUNK
    per_seq = n // batch // tokens
    nstate = wst.shape[2]
    tile = lambda j, b, k: (b * per_seq + k, j)
    per_j = lambda j, b, k: (j, 0, 0)
    return pl.pallas_call(
        functools.partial(_s5_kernel, rows=rows, half=nstate // 2),
        grid=(nj, batch, per_seq),
        in_specs=[pl.BlockSpec((tokens, LANE), tile),
                  pl.BlockSpec((1, cols, cols), per_j, pipeline_mode=pl.Buffered(1)),
                  pl.BlockSpec((1, cols, nstate), per_j, pipeline_mode=pl.Buffered(1)),
                  pl.BlockSpec((1, nstate, cols), per_j, pipeline_mode=pl.Buffered(1)),
                  pl.BlockSpec((1, 1, nstate), per_j),
                  pl.BlockSpec((1, 1, cols), per_j)],
        out_specs=pl.BlockSpec((tokens, LANE), tile),
        out_shape=jax.ShapeDtypeStruct((n, width), F32),
        scratch_shapes=[pltpu.VMEM((rows, nstate), F32), pltpu.VMEM((rows, nstate), F32),
                        pltpu.VMEM((1, nstate), F32)],
        compiler_params=_params("arbitrary", "arbitrary", "arbitrary"),
        name="s5_mixer",
    )(u, t_mat, wst, wo, a_chunk, skip)


def _dilated_bias(table, window, dilation):
    span = window // dilation
    kc = np.arange(2 * BAND_BLOCK)[:, None]
    qi = np.arange(BAND_BLOCK)[None, :]
    steps = BAND_BLOCK + qi - kc
    in_band = (steps >= 0) & (steps <= span)
    bias = _lookup(table * LOG2E, _bucket_np(np.clip(steps, 0, span) * dilation))
    rest = jnp.where(jnp.asarray(in_band)[None], bias, NEG_INF)
    first = jnp.where(jnp.asarray(in_band & (kc >= BAND_BLOCK))[None], bias, NEG_INF)
    tiles = jnp.stack([first, rest])
    return jnp.concatenate([tiles[:, 0::2], tiles[:, 1::2]], axis=3)


def _dilated_kernel(q_ref, kp_ref, kc_ref, vp_ref, vc_ref, bias_ref, o_ref, lse_ref, vt_scr, ot_scr, lt_scr,
                    s_scr, *, group, blocks_per_residue):
    first = (pl.program_id(1) * group) % blocks_per_residue == 0
    first_variant = jnp.where(first, 0, 1)
    kfull = jnp.concatenate([kp_ref[0], kc_ref[0]], axis=0)
    vfull = jnp.concatenate([vp_ref[0], vc_ref[0]], axis=0)
    vt_scr[...] = vfull.astype(F32).T.astype(BF16)
    lt_scr[...] = jnp.zeros_like(lt_scr)
    low = lax.broadcasted_iota(jnp.int32, (BAND_BLOCK, LANE), 1) < HEAD_DIM
    pairs = q_ref.shape[2] // LANE
    units = [(g, hp) for g in range(group) for hp in range(pairs)]

    def logits(g, hp):
        rows = slice(g * BAND_BLOCK, (g + 1) * BAND_BLOCK)
        cols = slice(hp * LANE, (hp + 1) * LANE)
        qq = q_ref[0, rows, cols]
        zero = jnp.zeros_like(qq)
        wt = jnp.concatenate([jnp.where(low, qq, zero), jnp.where(low, zero, qq)], axis=0)
        return lax.dot_general(kfull[g * BAND_BLOCK:(g + 2) * BAND_BLOCK, cols], wt, _NT,
                               preferred_element_type=F32)

    s_scr[0] = logits(*units[0])
    for i, (g, hp) in enumerate(units):
        if i + 1 < len(units):
            s_scr[(i + 1) % 2] = logits(*units[i + 1])
        rows = slice(g * BAND_BLOCK, (g + 1) * BAND_BLOCK)
        s = s_scr[i % 2] + bias_ref[first_variant if g == 0 else 1, hp]
        m = jnp.max(s, axis=0, keepdims=True)
        p = jnp.exp2(s - m)
        den = jnp.sum(p, axis=0, keepdims=True)
        ot = jnp.dot(vt_scr[hp * LANE:(hp + 1) * LANE, g * BAND_BLOCK:(g + 2) * BAND_BLOCK], p.astype(BF16),
                     preferred_element_type=F32)
        inv = 1.0 / den
        ot_scr[hp * LANE:hp * LANE + HEAD_DIM, rows] = ot[0:HEAD_DIM, 0:BAND_BLOCK] * inv[:, 0:BAND_BLOCK]
        ot_scr[hp * LANE + HEAD_DIM:(hp + 1) * LANE, rows] = ot[HEAD_DIM:, BAND_BLOCK:] * inv[:, BAND_BLOCK:]
        lse2 = m + jnp.log2(den)
        lt_scr[2 * hp:2 * hp + 1, rows] = lse2[:, 0:BAND_BLOCK]
        lt_scr[2 * hp + 1:2 * hp + 2, rows] = lse2[:, BAND_BLOCK:]
    o_ref[0] = ot_scr[...].T.astype(o_ref.dtype)
    lse_ref[0] = lt_scr[...].T


def _dilated_pattern(q, k, v, bias):
    batch, dilation, length, w = q.shape
    group = DIL_BLOCKS_PER_STEP
    rows = group * BAND_BLOCK
    flat = lambda t: t.reshape(batch, dilation * length, w)
    cur = lambda b, i: (b, i, 0)
    prev = lambda b, i: (b, jnp.maximum(i * group - 1, 0), 0)
    o, lse = pl.pallas_call(
        functools.partial(_dilated_kernel, group=group, blocks_per_residue=length // BAND_BLOCK),
        grid=(batch, dilation * length // rows),
        in_specs=[pl.BlockSpec((1, rows, w), cur),
                  pl.BlockSpec((1, BAND_BLOCK, w), prev), pl.BlockSpec((1, rows, w), cur),
                  pl.BlockSpec((1, BAND_BLOCK, w), prev), pl.BlockSpec((1, rows, w), cur),
                  _resident(bias.shape)],
        out_specs=[pl.BlockSpec((1, rows, w), cur), pl.BlockSpec((1, rows, LANE), cur)],
        out_shape=[jax.ShapeDtypeStruct((batch, dilation * length, w), BF16),
                   jax.ShapeDtypeStruct((batch, dilation * length, LANE), F32)],
        scratch_shapes=[pltpu.VMEM((w, rows + BAND_BLOCK), BF16), pltpu.VMEM((w, rows), F32),
                        pltpu.VMEM((LANE, rows), F32),
                        pltpu.VMEM((2, 2 * BAND_BLOCK, 2 * BAND_BLOCK), F32)],
        compiler_params=_params("parallel", "parallel"),
        name=f"dilated_attn_d{dilation}",
    )(flat(q), flat(k), flat(k), flat(v), flat(v), bias)
    return o.reshape(batch, dilation, length, w), lse.reshape(batch, dilation, length, LANE)


def _even_out_kernel(x_ref, ya_ref, o0_ref, o1_ref, o2_ref, l0_ref, l1_ref, l2_ref,
                     gw_ref, gb_ref, w_ref, out_ref, o_scr, l_scr):
    ya = ya_ref[...]
    gate = jnp.dot(ya.astype(BF16), gw_ref[...], preferred_element_type=F32) + gb_ref[...]
    ya = ya * _sigmoid(gate)

    tm = x_ref.shape[0]
    nb = o0_ref.shape[3] // LANE
    for p, (o_ref, l_ref) in enumerate(((o0_ref, l0_ref), (o1_ref, l1_ref), (o2_ref, l2_ref))):
        r = o_ref.shape[1]
        for res in range(r):
            dst = pl.ds(res, tm // r, stride=r) if r > 1 else slice(None)
            for c in range(nb):
                o_scr[p * nb + c, dst, :] = o_ref[0, res, :, c * LANE:(c + 1) * LANE].astype(F32)
            l_scr[p, dst, :] = l_ref[0, res]

    l0, l1, l2 = l_scr[0], l_scr[1], l_scr[2]
    m = jnp.maximum(jnp.maximum(l0, l1), l2)
    e0, e1, e2 = jnp.exp2(l0 - m), jnp.exp2(l1 - m), jnp.exp2(l2 - m)
    inv = 1.0 / (e0 + e1 + e2)
    alphas = (e0 * inv, e1 * inv, e2 * inv)
    low = lax.broadcasted_iota(jnp.int32, (tm, LANE), 1) < HEAD_DIM
    cols = []
    for c in range(nb):
        acc = jnp.zeros((tm, LANE), F32)
        for p, a in enumerate(alphas):
            weight = jnp.where(low, a[:, 2 * c:2 * c + 1], a[:, 2 * c + 1:2 * c + 2])
            acc = acc + weight * o_scr[p * nb + c]
        cols.append(acc)

    mixed = jnp.concatenate([ya] + cols, axis=1).astype(BF16)
    out_ref[...] = x_ref[...] + jnp.dot(mixed, w_ref[...], preferred_element_type=F32)


def _even_out_proj(x, ya, outs, lses, glu_w, glu_b, w_out):
    n, d = x.shape
    batch = outs[0].shape[0]
    width = outs[0].shape[3]
    tm = TOKEN_TILE
    per_seq = n // batch // tm
    row = lambda i: (i, 0)
    grouped = lambda t: pl.BlockSpec((1, t.shape[1], tm // t.shape[1], t.shape[3]),
                                     lambda i: (i // per_seq, 0, i % per_seq, 0))
    return pl.pallas_call(
        _even_out_kernel,
        grid=(n // tm,),
        in_specs=[pl.BlockSpec((tm, d), row),
                  pl.BlockSpec((tm, ya.shape[1]), row),
                  *[grouped(t) for t in outs], *[grouped(t) for t in lses],
                  _resident(glu_w.shape), _resident(glu_b.shape), _resident(w_out.shape)],
        out_specs=pl.BlockSpec((tm, d), row),
        out_shape=jax.ShapeDtypeStruct((n, d), F32),
        scratch_shapes=[pltpu.VMEM((len(outs) * width // LANE, tm, LANE), F32),
                        pltpu.VMEM((len(outs), tm, LANE), F32)],
        compiler_params=_params("parallel"),
        name="even_out_proj",
    )(x, ya, *outs, *lses, glu_w, glu_b, w_out)


def _diff_bias(table, tile):
    first_const = int(np.argmax(_bucket_np(np.arange(4 * MAX_DISTANCE)) == NUM_BUCKETS - 1))
    n_near = -(-(first_const + tile - 1) // tile)
    kr = np.arange(tile)[:, None]
    qc = np.arange(tile)[None, :]
    dist = np.arange(n_near)[:, None, None] * tile + qc[None] - kr[None]
    assert n_near * tile - (tile - 1) >= first_const
    rel = (table - table[NUM_BUCKETS - 1][None, :]) * LOG2E
    return jnp.where(jnp.asarray(dist >= 0)[None], _lookup(rel, _bucket_np(dist)), NEG_INF)


def _diff_attn_kernel(lam_ref, qt_ref, k_ref, vt_ref, bias_ref, g_ref, o_ref, acc_scr, s_scr, p_scr,
                      *, tile, n_near_max, out_scale):
    qi = pl.program_id(2)
    heads = qt_ref.shape[1]
    row = lax.broadcasted_iota(jnp.int32, (LANE, tile), 0)
    q_maps = []
    for h in range(heads):
        qt = qt_ref[0, h, 0]
        zero = jnp.zeros_like(qt)
        q_maps.append((jnp.where(row < HEAD_DIM, qt, zero), jnp.where(row >= HEAD_DIM, qt, zero)))

    def keys(h, j):
        return k_ref[0, pl.ds(pl.multiple_of(j * tile, tile), tile), h * LANE:(h + 1) * LANE]

    def softmax(sa, m_prev):
        m_new = jnp.maximum(m_prev, jnp.max(sa, axis=0, keepdims=True))
        return m_new, jnp.exp2(m_prev - m_new), jnp.exp2(sa - m_new).astype(BF16)

    def accumulate(h, a, j, p, alpha, l_prev):
        r = jnp.dot(vt_ref[0, h, j], p, preferred_element_type=F32)
        acc_scr[h, a] = alpha * acc_scr[h, a] + r[0:LANE]
        return alpha * l_prev + r[LANE:LANE + 1]

    def step(j, carry, with_bias):
        out = []
        for h in range(heads):
            m1, l1, m2, l2, alpha2 = carry[5 * h:5 * h + 5]
            l2 = accumulate(h, 1, jnp.maximum(j - 1, 0), p_scr[h], alpha2, l2)
            s2 = jnp.dot(keys(h, j), q_maps[h][1], preferred_element_type=F32)
            s1 = s_scr[h]
            if with_bias:
                bias = bias_ref[h, qi - j]
                s1 = s1 + bias
                s2 = s2 + bias
            m1, alpha1, p1 = softmax(s1, m1)
            l1 = accumulate(h, 0, j, p1, alpha1, l1)
            m2, alpha2, p2 = softmax(s2, m2)
            p_scr[h] = p2
            s_scr[h] = jnp.dot(keys(h, jnp.minimum(j + 1, qi)), q_maps[h][0], preferred_element_type=F32)
            out += [m1, l1, m2, l2, alpha2]
        return tuple(out)

    acc_scr[...] = jnp.zeros_like(acc_scr)
    p_scr[...] = jnp.zeros_like(p_scr)
    for h in range(heads):
        s_scr[h] = jnp.dot(keys(h, 0), q_maps[h][0], preferred_element_type=F32)
    m0 = jnp.full((1, tile), 2.0 * NEG_INF, F32)
    l0 = jnp.zeros((1, tile), F32)
    carry = (m0, l0, m0, l0, jnp.ones((1, tile), F32)) * heads
    n_near = jnp.minimum(qi + 1, n_near_max)
    carry = lax.fori_loop(0, qi + 1 - n_near, functools.partial(step, with_bias=False), carry)
    carry = lax.fori_loop(qi + 1 - n_near, qi + 1, functools.partial(step, with_bias=True), carry)
    for h in range(heads):
        _, l1, _, l2, alpha2 = carry[5 * h:5 * h + 5]
        l2 = accumulate(h, 1, qi, p_scr[h], alpha2, l2)
        att = acc_scr[h, 0] * (1.0 / l1) - lam_ref[...] * (acc_scr[h, 1] * (1.0 / l2))
        inv = lax.rsqrt(jnp.mean(att * att, axis=0, keepdims=True) + EPS)
        out_t = ((att * inv) * g_ref[...]) * out_scale
        o_ref[0, :, h * LANE:(h + 1) * LANE] = out_t.T.astype(o_ref.dtype)


def _diff_attention(qt, k, vt, bias, lam, subln, out_scale):
    batch, heads, per_seq, _, tile = qt.shape
    n, w = k.shape
    seq = n // batch
    n_near = bias.shape[1]
    hp = ATTN_HEADS_PER_STEP
    const = lambda h, b, i: (0, 0)
    once = pl.Buffered(1)
    out = pl.pallas_call(
        functools.partial(_diff_attn_kernel, tile=tile, n_near_max=n_near, out_scale=out_scale),
        grid=(heads // hp, batch, per_seq),
        in_specs=[pl.BlockSpec((1, tile), const),
                  pl.BlockSpec((1, hp, 1, LANE, tile), lambda h, b, i: (b, h, i, 0, 0)),
                  pl.BlockSpec((1, seq, hp * LANE), lambda h, b, i: (b, 0, h), pipeline_mode=once),
                  pl.BlockSpec((1, hp, per_seq, vt.shape[3], tile), lambda h, b, i: (b, h, 0, 0, 0),
                               pipeline_mode=once),
                  pl.BlockSpec((hp, n_near, tile, tile), lambda h, b, i: (h, 0, 0, 0), pipeline_mode=once),
                  pl.BlockSpec((LANE, tile), const)],
        out_specs=pl.BlockSpec((1, tile, hp * LANE), lambda h, b, i: (b, i, h)),
        out_shape=jax.ShapeDtypeStruct((batch, seq, w), BF16),
        scratch_shapes=[pltpu.VMEM((hp, 2, LANE, tile), F32), pltpu.VMEM((hp, tile, tile), F32),
                        pltpu.VMEM((hp, tile, tile), BF16)],
        compiler_params=_params("parallel", "parallel", "arbitrary"),
        name="diff_attention",
    )(jnp.full((1, tile), lam, F32), qt, k.reshape(batch, seq, w), vt, bias,
      jnp.broadcast_to(subln[:, None], (LANE, tile)))
    return out.reshape(n, w)


def _odd_out_kernel(x_ref, a_ref, w_ref, out_ref):
    out_ref[...] = x_ref[...] + jnp.dot(a_ref[...], w_ref[...], preferred_element_type=F32)


def _odd_out_proj(x, att, w_out):
    n, d = x.shape
    tm = TOKEN_TILE
    row = lambda i: (i, 0)
    return pl.pallas_call(
        _odd_out_kernel,
        grid=(n // tm,),
        in_specs=[pl.BlockSpec((tm, d), row), pl.BlockSpec((tm, att.shape[1]), row),
                  _resident(w_out.shape)],
        out_specs=pl.BlockSpec((tm, d), row),
        out_shape=jax.ShapeDtypeStruct((n, d), F32),
        compiler_params=_params("parallel"),
        name="odd_out_proj",
    )(x, att, w_out)


def _ffn_kernel(x_ref, g_ref, wg_ref, wu_ref, wd_ref, gf_ref, out_ref, act_scr, *, chunk, final_norm):
    x = x_ref[...]
    h = _rms_scale(x, g_ref[...]).astype(BF16)
    for c in range(0, wg_ref.shape[1], chunk):
        gate = jnp.dot(h, wg_ref[:, c:c + chunk], preferred_element_type=F32)
        up = jnp.dot(h, wu_ref[:, c:c + chunk], preferred_element_type=F32)
        act_scr[:, c:c + chunk] = ((gate * _sigmoid(gate)) * up).astype(BF16)
    y = x + jnp.dot(act_scr[...], wd_ref[...], preferred_element_type=F32)
    if final_norm:
        y = _rms_scale(y, gf_ref[...])
    out_ref[...] = y


def _ffn(x, g, w_gate, w_up, w_down, g_final, final_norm):
    n, d = x.shape
    d_ff = w_gate.shape[1]
    tm = TOKEN_TILE
    row = lambda i: (i, 0)
    return pl.pallas_call(
        functools.partial(_ffn_kernel, chunk=FFN_CHUNK, final_norm=final_norm),
        grid=(n // tm,),
        in_specs=[pl.BlockSpec((tm, d), row), _resident((1, d)), _resident(w_gate.shape),
                  _resident(w_up.shape), _resident(w_down.shape), _resident((1, d))],
        out_specs=pl.BlockSpec((tm, d), row),
        out_shape=jax.ShapeDtypeStruct((n, d), F32),
        scratch_shapes=[pltpu.VMEM((tm, d_ff), BF16)],
        compiler_params=_params("parallel"),
        name="ffn_final" if final_norm else "ffn",
    )(x, g, w_gate, w_up, w_down, g_final)


def kernel(x, rel_bias, norm_mix, norm_ffn, norm_final, ffn_w_gate, ffn_w_up, ffn_w_down, even_w_in, even_w_out, s5_lambda_re, s5_lambda_im, s5_log_dt, s5_b_re, s5_b_im, s5_c_re, s5_c_im, s5_d, s5_glu_w, s5_glu_b, odd_w_in, odd_w_out, diff_lambda_q1, diff_lambda_k1, diff_lambda_q2, diff_lambda_k2, diff_subln):
    batch, seq, d_model = x.shape
    depth = norm_mix.shape[0]
    dil_heads = even_w_in.shape[2] // 4 // HEAD_DIM
    table_dil = rel_bias[:, :dil_heads]
    table_diff = rel_bias[:, dil_heads:]
    dil_biases = [_dilated_bias(table_dil, w, r) for w, r in DIL_PATTERNS]
    diff_bias = _diff_bias(table_diff, ATTN_TILE)

    xs = x.reshape(batch * seq, d_model)
    for layer in range(depth):
        g_mix = norm_mix[layer][None, :]
        if layer % 2 == 0:
            e = layer // 2
            u, *qkv = _even_in_proj(xs, g_mix, even_w_in[e].astype(BF16), batch,
                                    tuple(r for _, r in DIL_PATTERNS))
            mats = _s5_matrices(s5_lambda_re[e], s5_lambda_im[e], s5_log_dt[e], s5_b_re[e], s5_b_im[e],
                                s5_c_re[e], s5_c_im[e], s5_d[e])
            ya = _s5_mixer(u, mats, batch)
            outs, lses = [], []
            for p, bias in enumerate(dil_biases):
                o, lse = _dilated_pattern(*qkv[3 * p:3 * p + 3], bias)
                outs.append(o)
                lses.append(lse)
            xs = _even_out_proj(xs, ya, outs, lses, s5_glu_w[e].astype(BF16), s5_glu_b[e][None, :],
                                even_w_out[e].astype(BF16))
        else:
            o = layer // 2
            lam_init = 0.8 - 0.6 * math.exp(-0.3 * layer)
            lam = (jnp.exp(jnp.sum(diff_lambda_q1[o] * diff_lambda_k1[o]))
                   - jnp.exp(jnp.sum(diff_lambda_q2[o] * diff_lambda_k2[o])) + lam_init)
            qt, k, vt = _odd_in_proj(xs, g_mix, odd_w_in[o].astype(BF16), batch)
            att = _diff_attention(qt, k, vt, diff_bias, lam, diff_subln[o], 1.0 - lam_init)
            xs = _odd_out_proj(xs, att, odd_w_out[o].astype(BF16))
        xs = _ffn(xs, norm_ffn[layer][None, :], ffn_w_gate[layer].astype(BF16),
                  ffn_w_up[layer].astype(BF16), ffn_w_down[layer].astype(BF16),
                  norm_final[None, :], layer == depth - 1)
    return xs.reshape(batch, seq, d_model)
```

```python
import functools
import math

import jax
import jax.numpy as jnp
import numpy as np
from jax import lax
from jax.experimental import pallas as pl
from jax.experimental.pallas import tpu as pltpu

F32 = jnp.float32
BF16 = jnp.bfloat16

EPS = 1e-6
NEG_INF = -1e30
LANE = 128
VMEM_LIMIT = 56 * 1024 * 1024

HEAD_DIM = 64
S5_GROUP = 16
S5_STATE = 64
S5_CHUNK = 16
S5_LANE_GROUPS = LANE // S5_GROUP
DIL_PATTERNS = ((128, 1), (512, 4), (2048, 16))
BAND_BLOCK = 128
DIL_BLOCKS_PER_STEP = 4
NUM_BUCKETS = 32
MAX_DISTANCE = 2048

TOKEN_TILE = 512
ATTN_TILE = 512
ATTN_HEADS_PER_STEP = 4
ONES_ROWS = 16
LOG2E = math.log2(math.e)
S5_ROW_TILE = 256
FFN_CHUNK = 256


def _params(*sem):
    return pltpu.CompilerParams(dimension_semantics=sem, vmem_limit_bytes=VMEM_LIMIT)


def _resident(shape):
    return pl.BlockSpec(shape, lambda *_: (0,) * len(shape), pipeline_mode=pl.Buffered(1))


def _rms_scale(x, g):
    inv = lax.rsqrt(jnp.mean(x * x, axis=-1, keepdims=True) + EPS)
    return (x * inv) * g


def _sigmoid(x):
    return 1.0 / (1.0 + jnp.exp(-x))


def _gelu_tanh(x):
    return 0.5 * x * (1.0 + jnp.tanh(math.sqrt(2.0 / math.pi) * (x + 0.044715 * (x * x * x))))


def _bucket_np(dist):
    max_exact = NUM_BUCKETS // 2
    d = np.maximum(dist, 0)
    scaled = (np.log(np.maximum(d, 1).astype(np.float64) / max_exact)
              / math.log(MAX_DISTANCE / max_exact) * (NUM_BUCKETS - max_exact))
    large = np.minimum(max_exact + scaled.astype(np.int64), NUM_BUCKETS - 1)
    return np.where(d < max_exact, d, large)


def _lookup(table, bucket):
    idx = jnp.asarray(bucket.astype(np.int8))[None]
    expand = (slice(None),) + (None,) * bucket.ndim
    out = jnp.zeros((table.shape[1],) + bucket.shape, F32)
    for b in np.unique(bucket):
        out = jnp.where(idx == b, table[int(b)].astype(F32)[expand], out)
    return out


def _even_in_kernel(x_ref, g_ref, w_ref, u_ref, *rest, width, scale, dilations):
    qkv_refs, z_scr = rest[:-1], rest[-1]
    h = _rms_scale(x_ref[...], g_ref[...]).astype(BF16)
    u_ref[...] = jnp.dot(h, w_ref[:, 0:width], preferred_element_type=F32)
    zq = jnp.dot(h, w_ref[:, width:2 * width], preferred_element_type=F32) * scale
    zkv = jnp.dot(h, w_ref[:, 2 * width:4 * width], preferred_element_type=F32)
    nb = width // LANE
    for c in range(nb):
        z_scr[c] = zq[:, c * LANE:(c + 1) * LANE]
    for c in range(2 * nb):
        z_scr[nb + c] = zkv[:, c * LANE:(c + 1) * LANE]
    tm = x_ref.shape[0]
    for p, r in enumerate(dilations):
        for res in range(r):
            rows = pl.ds(res, tm // r, stride=r) if r > 1 else slice(None)
            for c in range(3 * nb):
                qkv_refs[3 * p + c // nb][0, res, :, (c % nb) * LANE:(c % nb + 1) * LANE] = (
                    z_scr[c, rows, :].astype(BF16))


def _even_in_proj(x, g, w, batch, dilations):
    n, d = x.shape
    width = w.shape[1] // 4
    tm = TOKEN_TILE
    seq = n // batch
    per_seq = seq // tm
    row = lambda i: (i, 0)
    out_specs = [pl.BlockSpec((tm, width), row)]
    out_shape = [jax.ShapeDtypeStruct((n, width), F32)]
    for r in dilations:
        out_specs += [pl.BlockSpec((1, r, tm // r, width), lambda i: (i // per_seq, 0, i % per_seq, 0))] * 3
        out_shape += [jax.ShapeDtypeStruct((batch, r, seq // r, width), BF16)] * 3
    return pl.pallas_call(
        functools.partial(_even_in_kernel, width=width, scale=HEAD_DIM ** -0.5 * LOG2E, dilations=dilations),
        grid=(n // tm,),
        in_specs=[pl.BlockSpec((tm, d), row), _resident((1, d)), _resident(w.shape)],
        out_specs=out_specs,
        out_shape=out_shape,
        scratch_shapes=[pltpu.VMEM((3 * width // LANE, tm, LANE), F32)],
        compiler_params=_params("parallel"),
        name="even_in_proj",
    )(x, g, w)


_NT = (((1,), (1,)), ((), ()))


def _odd_in_kernel(x_ref, g_ref, wq_ref, wk_ref, wv_ref, qt_ref, k_ref, vt_ref, *, scale):
    h = _rms_scale(x_ref[...], g_ref[...]).astype(BF16)
    heads = qt_ref.shape[1]
    tm = h.shape[0]
    qt = lax.dot_general(wq_ref[...], h, _NT, preferred_element_type=F32) * scale
    qt_ref[0, :, 0] = qt.astype(BF16).reshape(heads, LANE, tm)
    k_ref[...] = jnp.dot(h, wk_ref[...], preferred_element_type=F32).astype(BF16)
    vt = lax.dot_general(wv_ref[...], h, _NT, preferred_element_type=F32)
    vt_ref[0, :, 0, 0:LANE, :] = vt.astype(BF16).reshape(heads, LANE, tm)
    vt_ref[0, :, 0, LANE:, :] = jnp.ones((heads, vt_ref.shape[3] - LANE, tm), BF16)


def _odd_in_proj(x, g, w, batch):
    n, d = x.shape
    width = w.shape[1] // 3
    heads = width // LANE
    tm = ATTN_TILE
    per_seq = n // batch // tm
    wq_t = w[:, 0:width].T
    wk = w[:, width:2 * width]
    wv_t = w[:, 2 * width:3 * width].T
    row = lambda i: (i, 0)
    tmap = lambda i: (i // per_seq, 0, i % per_seq, 0, 0)
    tshape = lambda rows: jax.ShapeDtypeStruct((batch, heads, per_seq, rows, tm), BF16)
    vrows = LANE + ONES_ROWS
    return pl.pallas_call(
        functools.partial(_odd_in_kernel, scale=HEAD_DIM ** -0.5 * LOG2E),
        grid=(n // tm,),
        in_specs=[pl.BlockSpec((tm, d), row), _resident((1, d)), _resident(wq_t.shape),
                  _resident(wk.shape), _resident(wv_t.shape)],
        out_specs=[pl.BlockSpec((1, heads, 1, LANE, tm), tmap), pl.BlockSpec((tm, width), row),
                   pl.BlockSpec((1, heads, 1, vrows, tm), tmap)],
        out_shape=[tshape(LANE), jax.ShapeDtypeStruct((n, width), BF16), tshape(vrows)],
        compiler_params=_params("parallel"),
        name="odd_in_proj",
    )(x, g, wq_t, wk, wv_t)


def _s5_matrices(lam_re, lam_im, log_dt, b_re, b_im, c_re, c_im, d_skip):
    hi = lax.Precision.HIGHEST
    L = S5_CHUNK
    G, P = lam_re.shape
    nj = G // S5_LANE_GROUPS
    dt = jnp.exp(log_dt)[:, None]
    steps = jnp.arange(L + 1, dtype=F32)[:, None, None]
    mag = jnp.exp(lam_re * dt * steps)
    ang = lam_im * dt * steps
    pr, pi = mag * jnp.cos(ang), mag * jnp.sin(ang)
    nr, ni = pr[1] - 1.0, pi[1]
    den = lam_re * lam_re + lam_im * lam_im
    cr = ((nr * lam_re + ni * lam_im) / den)[..., None]
    ci = ((ni * lam_re - nr * lam_im) / den)[..., None]
    bbr = cr * b_re - ci * b_im
    bbi = cr * b_im + ci * b_re
    car = c_re[None] * pr[:, :, None, :] - c_im[None] * pi[:, :, None, :]
    cai = c_re[None] * pi[:, :, None, :] + c_im[None] * pr[:, :, None, :]
    lg = S5_LANE_GROUPS
    ncol, nst = L * LANE, 2 * lg * P
    row_group = np.arange(lg)[:, None]
    same_col = jnp.asarray(row_group == ((np.arange(ncol) // S5_GROUP) % lg)[None, :])
    same_st = jnp.asarray(row_group == ((np.arange(nst) // P) % lg)[None, :])

    kg = (jnp.einsum('dgcp,gpe->dgce', car[:L], bbr, precision=hi)
          - jnp.einsum('dgcp,gpe->dgce', cai[:L], bbi, precision=hi))
    lag = np.arange(L)[None, :] - np.arange(L)[:, None]
    kt = jnp.where((lag >= 0)[:, :, None, None, None], kg[np.maximum(lag, 0)], 0.0)
    kt = kt.reshape(L, L, nj, lg, S5_GROUP, S5_GROUP).transpose(2, 0, 5, 1, 3, 4).reshape(nj, L, S5_GROUP, ncol)
    t_mat = jnp.where(same_col[None, None, :, None, :], kt[:, :, None, :, :], 0.0).astype(BF16)
    t_mat = t_mat.reshape(nj, ncol, ncol)

    rev = np.arange(L - 1, -1, -1)
    abr = pr[rev][..., None] * bbr[None] - pi[rev][..., None] * bbi[None]
    abi = pr[rev][..., None] * bbi[None] + pi[rev][..., None] * bbr[None]
    ab = jnp.stack([abr, abi]).reshape(2, L, nj, lg, P, S5_GROUP).transpose(2, 1, 5, 0, 3, 4)
    ab = ab.reshape(nj, L, S5_GROUP, nst)
    wst = jnp.where(same_st[None, None, :, None, :], ab[:, :, None, :, :], 0.0).astype(BF16)
    wst = wst.reshape(nj, ncol, nst)

    wo = jnp.stack([car[1:], -cai[1:]]).reshape(2, L, nj, lg, S5_GROUP, P).transpose(2, 0, 5, 1, 3, 4)
    wo = wo.reshape(nj, 2, P, ncol)
    wo = jnp.where(same_col[None, None, :, None, :], wo[:, :, None, :, :], 0.0).astype(BF16)
    wo = wo.reshape(nj, nst, ncol)

    a_chunk = jnp.stack([pr[L], pi[L]]).reshape(2, nj, S5_LANE_GROUPS * P)
    a_chunk = a_chunk.transpose(1, 0, 2).reshape(nj, 1, 2 * S5_LANE_GROUPS * P)
    skip = jnp.tile(d_skip.reshape(nj, 1, LANE), (1, 1, L))
    return t_mat, wst, wo, a_chunk, skip


def _s5_kernel(u_ref, t_ref, wst_ref, wo_ref, a_ref, skip_ref, y_ref, s_scr, h_scr, carry_scr, *, rows, half):
    @pl.when(pl.program_id(2) == 0)
    def _():
        carry_scr[...] = jnp.zeros_like(carry_scr)

    xf = jnp.concatenate([u_ref[pl.ds(t, rows, stride=S5_CHUNK), :] for t in range(S5_CHUNK)], axis=1)
    xb = xf.astype(BF16)
    s_scr[...] = jnp.dot(xb, wst_ref[0], preferred_element_type=F32)
    ar = a_ref[0, :, 0:half]
    ai = a_ref[0, :, half:2 * half]

    def step(i, carry):
        hr, hi = carry
        h_scr[pl.ds(i, 1), 0:half] = hr
        h_scr[pl.ds(i, 1), half:2 * half] = hi
        sr = s_scr[pl.ds(i, 1), 0:half]
        si = s_scr[pl.ds(i, 1), half:2 * half]
        return ar * hr - ai * hi + sr, ar * hi + ai * hr + si

    hr, hi = lax.fori_loop(0, rows, step, (carry_scr[:, 0:half], carry_scr[:, half:2 * half]), unroll=8)
    carry_scr[:, 0:half] = hr
    carry_scr[:, half:2 * half] = hi

    y = _gelu_tanh(jnp.dot(xb, t_ref[0], preferred_element_type=F32)
                   + jnp.dot(h_scr[...].astype(BF16), wo_ref[0], preferred_element_type=F32)
                   + skip_ref[0] * xf)
    for t in range(S5_CHUNK):
        y_ref[pl.ds(t, rows, stride=S5_CHUNK), :] = y[:, t * LANE:(t + 1) * LANE]


def _s5_mixer(u, mats, batch):
    t_mat, wst, wo, a_chunk, skip = mats
    n, width = u.shape
    nj = width // LANE
    cols = S5_CHUNK * LANE
    rows = S5_ROW_TILE
    tokens = rows * S5_CHUNK
    per_seq = n // batch // tokens
    nstate = wst.shape[2]
    tile = lambda j, b, k: (b * per_seq + k, j)
    per_j = lambda j, b, k: (j, 0, 0)
    return pl.pallas_call(
        functools.partial(_s5_kernel, rows=rows, half=nstate // 2),
        grid=(nj, batch, per_seq),
        in_specs=[pl.BlockSpec((tokens, LANE), tile),
                  pl.BlockSpec((1, cols, cols), per_j, pipeline_mode=pl.Buffered(1)),
                  pl.BlockSpec((1, cols, nstate), per_j, pipeline_mode=pl.Buffered(1)),
                  pl.BlockSpec((1, nstate, cols), per_j, pipeline_mode=pl.Buffered(1)),
                  pl.BlockSpec((1, 1, nstate), per_j),
                  pl.BlockSpec((1, 1, cols), per_j)],
        out_specs=pl.BlockSpec((tokens, LANE), tile),
        out_shape=jax.ShapeDtypeStruct((n, width), F32),
        scratch_shapes=[pltpu.VMEM((rows, nstate), F32), pltpu.VMEM((rows, nstate), F32),
                        pltpu.VMEM((1, nstate), F32)],
        compiler_params=_params("arbitrary", "arbitrary", "arbitrary"),
        name="s5_mixer",
    )(u, t_mat, wst, wo, a_chunk, skip)


def _dilated_bias(table, window, dilation):
    span = window // dilation
    kc = np.arange(2 * BAND_BLOCK)[:, None]
    qi = np.arange(BAND_BLOCK)[None, :]
    steps = BAND_BLOCK + qi - kc
    in_band = (steps >= 0) & (steps <= span)
    bias = _lookup(table * LOG2E, _bucket_np(np.clip(steps, 0, span) * dilation))
    rest = jnp.where(jnp.asarray(in_band)[None], bias, NEG_INF)
    first = jnp.where(jnp.asarray(in_band & (kc >= BAND_BLOCK))[None], bias, NEG_INF)
    tiles = jnp.stack([first, rest])
    return jnp.concatenate([tiles[:, 0::2], tiles[:, 1::2]], axis=3)


def _dilated_kernel(q_ref, kp_ref, kc_ref, vp_ref, vc_ref, bias_ref, o_ref, lse_ref, vt_scr, ot_scr, lt_scr,
                    s_scr, *, group, blocks_per_residue):
    first = (pl.program_id(1) * group) % blocks_per_residue == 0
    first_variant = jnp.where(first, 0, 1)
    kfull = jnp.concatenate([kp_ref[0], kc_ref[0]], axis=0)
    vfull = jnp.concatenate([vp_ref[0], vc_ref[0]], axis=0)
    vt_scr[...] = vfull.astype(F32).T.astype(BF16)
    lt_scr[...] = jnp.zeros_like(lt_scr)
    low = lax.broadcasted_iota(jnp.int32, (BAND_BLOCK, LANE), 1) < HEAD_DIM
    pairs = q_ref.shape[2] // LANE
    units = [(g, hp) for g in range(group) for hp in range(pairs)]

    def logits(g, hp):
        rows = slice(g * BAND_BLOCK, (g + 1) * BAND_BLOCK)
        cols = slice(hp * LANE, (hp + 1) * LANE)
        qq = q_ref[0, rows, cols]
        zero = jnp.zeros_like(qq)
        wt = jnp.concatenate([jnp.where(low, qq, zero), jnp.where(low, zero, qq)], axis=0)
        return lax.dot_general(kfull[g * BAND_BLOCK:(g + 2) * BAND_BLOCK, cols], wt, _NT,
                               preferred_element_type=F32)

    s_scr[0] = logits(*units[0])
    for i, (g, hp) in enumerate(units):
        if i + 1 < len(units):
            s_scr[(i + 1) % 2] = logits(*units[i + 1])
        rows = slice(g * BAND_BLOCK, (g + 1) * BAND_BLOCK)
        s = s_scr[i % 2] + bias_ref[first_variant if g == 0 else 1, hp]
        m = jnp.max(s, axis=0, keepdims=True)
        p = jnp.exp2(s - m)
        den = jnp.sum(p, axis=0, keepdims=True)
        ot = jnp.dot(vt_scr[hp * LANE:(hp + 1) * LANE, g * BAND_BLOCK:(g + 2) * BAND_BLOCK], p.astype(BF16),
                     preferred_element_type=F32)
        inv = 1.0 / den
        ot_scr[hp * LANE:hp * LANE + HEAD_DIM, rows] = ot[0:HEAD_DIM, 0:BAND_BLOCK] * inv[:, 0:BAND_BLOCK]
        ot_scr[hp * LANE + HEAD_DIM:(hp + 1) * LANE, rows] = ot[HEAD_DIM:, BAND_BLOCK:] * inv[:, BAND_BLOCK:]
        lse2 = m + jnp.log2(den)
        lt_scr[2 * hp:2 * hp + 1, rows] = lse2[:, 0:BAND_BLOCK]
        lt_scr[2 * hp + 1:2 * hp + 2, rows] = lse2[:, BAND_BLOCK:]
    o_ref[0] = ot_scr[...].T.astype(o_ref.dtype)
    lse_ref[0] = lt_scr[...].T


def _dilated_pattern(q, k, v, bias):
    batch, dilation, length, w = q.shape
    group = DIL_BLOCKS_PER_STEP
    rows = group * BAND_BLOCK
    flat = lambda t: t.reshape(batch, dilation * length, w)
    cur = lambda b, i: (b, i, 0)
    prev = lambda b, i: (b, jnp.maximum(i * group - 1, 0), 0)
    o, lse = pl.pallas_call(
        functools.partial(_dilated_kernel, group=group, blocks_per_residue=length // BAND_BLOCK),
        grid=(batch, dilation * length // rows),
        in_specs=[pl.BlockSpec((1, rows, w), cur),
                  pl.BlockSpec((1, BAND_BLOCK, w), prev), pl.BlockSpec((1, rows, w), cur),
                  pl.BlockSpec((1, BAND_BLOCK, w), prev), pl.BlockSpec((1, rows, w), cur),
                  _resident(bias.shape)],
        out_specs=[pl.BlockSpec((1, rows, w), cur), pl.BlockSpec((1, rows, LANE), cur)],
        out_shape=[jax.ShapeDtypeStruct((batch, dilation * length, w), BF16),
                   jax.ShapeDtypeStruct((batch, dilation * length, LANE), F32)],
        scratch_shapes=[pltpu.VMEM((w, rows + BAND_BLOCK), BF16), pltpu.VMEM((w, rows), F32),
                        pltpu.VMEM((LANE, rows), F32),
                        pltpu.VMEM((2, 2 * BAND_BLOCK, 2 * BAND_BLOCK), F32)],
        compiler_params=_params("parallel", "parallel"),
        name=f"dilated_attn_d{dilation}",
    )(flat(q), flat(k), flat(k), flat(v), flat(v), bias)
    return o.reshape(batch, dilation, length, w), lse.reshape(batch, dilation, length, LANE)


def _even_mix(x_ref, ya_ref, o0_ref, o1_ref, o2_ref, l0_ref, l1_ref, l2_ref, gw_ref, gb_ref, w_ref,
              o_scr, l_scr):
    ya = ya_ref[...]
    gate = jnp.dot(ya.astype(BF16), gw_ref[...], preferred_element_type=F32) + gb_ref[...]
    ya = ya * _sigmoid(gate)

    tm = x_ref.shape[0]
    nb = o0_ref.shape[3] // LANE
    for p, (o_ref, l_ref) in enumerate(((o0_ref, l0_ref), (o1_ref, l1_ref), (o2_ref, l2_ref))):
        r = o_ref.shape[1]
        for res in range(r):
            dst = pl.ds(res, tm // r, stride=r) if r > 1 else slice(None)
            for c in range(nb):
                o_scr[p * nb + c, dst, :] = o_ref[0, res, :, c * LANE:(c + 1) * LANE].astype(F32)
            l_scr[p, dst, :] = l_ref[0, res]

    l0, l1, l2 = l_scr[0], l_scr[1], l_scr[2]
    m = jnp.maximum(jnp.maximum(l0, l1), l2)
    e0, e1, e2 = jnp.exp2(l0 - m), jnp.exp2(l1 - m), jnp.exp2(l2 - m)
    inv = 1.0 / (e0 + e1 + e2)
    alphas = (e0 * inv, e1 * inv, e2 * inv)
    low = lax.broadcasted_iota(jnp.int32, (tm, LANE), 1) < HEAD_DIM
    cols = []
    for c in range(nb):
        acc = jnp.zeros((tm, LANE), F32)
        for p, a in enumerate(alphas):
            weight = jnp.where(low, a[:, 2 * c:2 * c + 1], a[:, 2 * c + 1:2 * c + 2])
            acc = acc + weight * o_scr[p * nb + c]
        cols.append(acc)

    mixed = jnp.concatenate([ya] + cols, axis=1).astype(BF16)
    return x_ref[...] + jnp.dot(mixed, w_ref[...], preferred_element_type=F32)


def _diff_bias(table, tile):
    first_const = int(np.argmax(_bucket_np(np.arange(4 * MAX_DISTANCE)) == NUM_BUCKETS - 1))
    n_near = -(-(first_const + tile - 1) // tile)
    kr = np.arange(tile)[:, None]
    qc = np.arange(tile)[None, :]
    dist = np.arange(n_near)[:, None, None] * tile + qc[None] - kr[None]
    assert n_near * tile - (tile - 1) >= first_const
    rel = (table - table[NUM_BUCKETS - 1][None, :]) * LOG2E
    return jnp.where(jnp.asarray(dist >= 0)[None], _lookup(rel, _bucket_np(dist)), NEG_INF)


def _diff_attn_kernel(lam_ref, qt_ref, k_ref, vt_ref, bias_ref, g_ref, o_ref, acc_scr, s_scr, p_scr,
                      *, tile, n_near_max, out_scale):
    qi = pl.program_id(2)
    heads = qt_ref.shape[1]
    row = lax.broadcasted_iota(jnp.int32, (LANE, tile), 0)
    q_maps = []
    for h in range(heads):
        qt = qt_ref[0, h, 0]
        zero = jnp.zeros_like(qt)
        q_maps.append((jnp.where(row < HEAD_DIM, qt, zero), jnp.where(row >= HEAD_DIM, qt, zero)))

    def keys(h, j):
        return k_ref[0, pl.ds(pl.multiple_of(j * tile, tile), tile), h * LANE:(h + 1) * LANE]

    def softmax(sa, m_prev):
        m_new = jnp.maximum(m_prev, jnp.max(sa, axis=0, keepdims=True))
        return m_new, jnp.exp2(m_prev - m_new), jnp.exp2(sa - m_new).astype(BF16)

    def accumulate(h, a, j, p, alpha, l_prev):
        r = jnp.dot(vt_ref[0, h, j], p, preferred_element_type=F32)
        acc_scr[h, a] = alpha * acc_scr[h, a] + r[0:LANE]
        return alpha * l_prev + r[LANE:LANE + 1]

    def step(j, carry, with_bias):
        out = []
        for h in range(heads):
            m1, l1, m2, l2, alpha2 = carry[5 * h:5 * h + 5]
            l2 = accumulate(h, 1, jnp.maximum(j - 1, 0), p_scr[h], alpha2, l2)
            s2 = jnp.dot(keys(h, j), q_maps[h][1], preferred_element_type=F32)
            s1 = s_scr[h]
            if with_bias:
                bias = bias_ref[h, qi - j]
                s1 = s1 + bias
                s2 = s2 + bias
            m1, alpha1, p1 = softmax(s1, m1)
            l1 = accumulate(h, 0, j, p1, alpha1, l1)
            m2, alpha2, p2 = softmax(s2, m2)
            p_scr[h] = p2
            s_scr[h] = jnp.dot(keys(h, jnp.minimum(j + 1, qi)), q_maps[h][0], preferred_element_type=F32)
            out += [m1, l1, m2, l2, alpha2]
        return tuple(out)

    acc_scr[...] = jnp.zeros_like(acc_scr)
    p_scr[...] = jnp.zeros_like(p_scr)
    for h in range(heads):
        s_scr[h] = jnp.dot(keys(h, 0), q_maps[h][0], preferred_element_type=F32)
    m0 = jnp.full((1, tile), 2.0 * NEG_INF, F32)
    l0 = jnp.zeros((1, tile), F32)
    carry = (m0, l0, m0, l0, jnp.ones((1, tile), F32)) * heads
    n_near = jnp.minimum(qi + 1, n_near_max)
    carry = lax.fori_loop(0, qi + 1 - n_near, functools.partial(step, with_bias=False), carry)
    carry = lax.fori_loop(qi + 1 - n_near, qi + 1, functools.partial(step, with_bias=True), carry)
    for h in range(heads):
        _, l1, _, l2, alpha2 = carry[5 * h:5 * h + 5]
        l2 = accumulate(h, 1, qi, p_scr[h], alpha2, l2)
        att = acc_scr[h, 0] * (1.0 / l1) - lam_ref[...] * (acc_scr[h, 1] * (1.0 / l2))
        inv = lax.rsqrt(jnp.mean(att * att, axis=0, keepdims=True) + EPS)
        out_t = ((att * inv) * g_ref[...]) * out_scale
        o_ref[0, :, h * LANE:(h + 1) * LANE] = out_t.T.astype(o_ref.dtype)


def _diff_attention(qt, k, vt, bias, lam, subln, out_scale):
    batch, heads, per_seq, _, tile = qt.shape
    n, w = k.shape
    seq = n // batch
    n_near = bias.shape[1]
    hp = ATTN_HEADS_PER_STEP
    const = lambda h, b, i: (0, 0)
    once = pl.Buffered(1)
    out = pl.pallas_call(
        functools.partial(_diff_attn_kernel, tile=tile, n_near_max=n_near, out_scale=out_scale),
        grid=(heads // hp, batch, per_seq),
        in_specs=[pl.BlockSpec((1, tile), const),
                  pl.BlockSpec((1, hp, 1, LANE, tile), lambda h, b, i: (b, h, i, 0, 0)),
                  pl.BlockSpec((1, seq, hp * LANE), lambda h, b, i: (b, 0, h), pipeline_mode=once),
                  pl.BlockSpec((1, hp, per_seq, vt.shape[3], tile), lambda h, b, i: (b, h, 0, 0, 0),
                               pipeline_mode=once),
                  pl.BlockSpec((hp, n_near, tile, tile), lambda h, b, i: (h, 0, 0, 0), pipeline_mode=once),
                  pl.BlockSpec((LANE, tile), const)],
        out_specs=pl.BlockSpec((1, tile, hp * LANE), lambda h, b, i: (b, i, h)),
        out_shape=jax.ShapeDtypeStruct((batch, seq, w), BF16),
        scratch_shapes=[pltpu.VMEM((hp, 2, LANE, tile), F32), pltpu.VMEM((hp, tile, tile), F32),
                        pltpu.VMEM((hp, tile, tile), BF16)],
        compiler_params=_params("parallel", "parallel", "arbitrary"),
        name="diff_attention",
    )(jnp.full((1, tile), lam, F32), qt, k.reshape(batch, seq, w), vt, bias,
      jnp.broadcast_to(subln[:, None], (LANE, tile)))
    return out.reshape(n, w)


def _ffn_block(x, g_ref, wg_ref, wu_ref, wd_ref, gf_ref, act_scr, final_norm):
    h = _rms_scale(x, g_ref[...]).astype(BF16)
    for c in range(0, wg_ref.shape[1], FFN_CHUNK):
        gate = jnp.dot(h, wg_ref[:, c:c + FFN_CHUNK], preferred_element_type=F32)
        up = jnp.dot(h, wu_ref[:, c:c + FFN_CHUNK], preferred_element_type=F32)
        act_scr[:, c:c + FFN_CHUNK] = ((gate * _sigmoid(gate)) * up).astype(BF16)
    y = x + jnp.dot(act_scr[...], wd_ref[...], preferred_element_type=F32)
    return _rms_scale(y, gf_ref[...]) if final_norm else y


def _even_tail_kernel(x_ref, ya_ref, o0_ref, o1_ref, o2_ref, l0_ref, l1_ref, l2_ref, gw_ref, gb_ref, w_ref,
                      g_ref, wg_ref, wu_ref, wd_ref, gf_ref, out_ref, o_scr, l_scr, act_scr, *, final_norm):
    x = _even_mix(x_ref, ya_ref, o0_ref, o1_ref, o2_ref, l0_ref, l1_ref, l2_ref, gw_ref, gb_ref, w_ref,
                  o_scr, l_scr)
    out_ref[...] = _ffn_block(x, g_ref, wg_ref, wu_ref, wd_ref, gf_ref, act_scr, final_norm)


def _odd_tail_kernel(x_ref, a_ref, w_ref, g_ref, wg_ref, wu_ref, wd_ref, gf_ref, out_ref, act_scr,
                     *, final_norm):
    x = x_ref[...] + jnp.dot(a_ref[...], w_ref[...], preferred_element_type=F32)
    out_ref[...] = _ffn_block(x, g_ref, wg_ref, wu_ref, wd_ref, gf_ref, act_scr, final_norm)


def _ffn_specs(d, ffn):
    g, w_gate, w_up, w_down, g_final = ffn
    return [_resident((1, d)), _resident(w_gate.shape), _resident(w_up.shape), _resident(w_down.shape),
            _resident((1, d))]


def _even_tail(x, ya, outs, lses, glu_w, glu_b, w_out, ffn, final_norm):
    n, d = x.shape
    batch = outs[0].shape[0]
    width = outs[0].shape[3]
    tm = TOKEN_TILE
    per_seq = n // batch // tm
    row = lambda i: (i, 0)
    grouped = lambda t: pl.BlockSpec((1, t.shape[1], tm // t.shape[1], t.shape[3]),
                                     lambda i: (i // per_seq, 0, i % per_seq, 0))
    return pl.pallas_call(
        functools.partial(_even_tail_kernel, final_norm=final_norm),
        grid=(n // tm,),
        in_specs=[pl.BlockSpec((tm, d), row),
                  pl.BlockSpec((tm, ya.shape[1]), row),
                  *[grouped(t) for t in outs], *[grouped(t) for t in lses],
                  _resident(glu_w.shape), _resident(glu_b.shape), _resident(w_out.shape),
                  *_ffn_specs(d, ffn)],
        out_specs=pl.BlockSpec((tm, d), row),
        out_shape=jax.ShapeDtypeStruct((n, d), F32),
        scratch_shapes=[pltpu.VMEM((len(outs) * width // LANE, tm, LANE), F32),
                        pltpu.VMEM((len(outs), tm, LANE), F32),
                        pltpu.VMEM((tm, ffn[1].shape[1]), BF16)],
        compiler_params=_params("parallel"),
        name="even_tail",
    )(x, ya, *outs, *lses, glu_w, glu_b, w_out, *ffn)


def _odd_tail(x, att, w_out, ffn, final_norm):
    n, d = x.shape
    tm = TOKEN_TILE
    row = lambda i: (i, 0)
    return pl.pallas_call(
        functools.partial(_odd_tail_kernel, final_norm=final_norm),
        grid=(n // tm,),
        in_specs=[pl.BlockSpec((tm, d), row), pl.BlockSpec((tm, att.shape[1]), row),
                  _resident(w_out.shape), *_ffn_specs(d, ffn)],
        out_specs=pl.BlockSpec((tm, d), row),
        out_shape=jax.ShapeDtypeStruct((n, d), F32),
        scratch_shapes=[pltpu.VMEM((tm, ffn[1].shape[1]), BF16)],
        compiler_params=_params("parallel"),
        name="odd_tail",
    )(x, att, w_out, *ffn)


def kernel(x, rel_bias, norm_mix, norm_ffn, norm_final, ffn_w_gate, ffn_w_up, ffn_w_down, even_w_in, even_w_out, s5_lambda_re, s5_lambda_im, s5_log_dt, s5_b_re, s5_b_im, s5_c_re, s5_c_im, s5_d, s5_glu_w, s5_glu_b, odd_w_in, odd_w_out, diff_lambda_q1, diff_lambda_k1, diff_lambda_q2, diff_lambda_k2, diff_subln):
    batch, seq, d_model = x.shape
    depth = norm_mix.shape[0]
    dil_heads = even_w_in.shape[2] // 4 // HEAD_DIM
    table_dil = rel_bias[:, :dil_heads]
    table_diff = rel_bias[:, dil_heads:]
    dil_biases = [_dilated_bias(table_dil, w, r) for w, r in DIL_PATTERNS]
    diff_bias = _diff_bias(table_diff, ATTN_TILE)

    xs = x.reshape(batch * seq, d_model)
    for layer in range(depth):
        g_mix = norm_mix[layer][None, :]
        final = layer == depth - 1
        ffn = (norm_ffn[layer][None, :], ffn_w_gate[layer].astype(BF16), ffn_w_up[layer].astype(BF16),
               ffn_w_down[layer].astype(BF16), norm_final[None, :])
        if layer % 2 == 0:
            e = layer // 2
            u, *qkv = _even_in_proj(xs, g_mix, even_w_in[e].astype(BF16), batch,
                                    tuple(r for _, r in DIL_PATTERNS))
            mats = _s5_matrices(s5_lambda_re[e], s5_lambda_im[e], s5_log_dt[e], s5_b_re[e], s5_b_im[e],
                                s5_c_re[e], s5_c_im[e], s5_d[e])
            ya = _s5_mixer(u, mats, batch)
            outs, lses = [], []
            for p, bias in enumerate(dil_biases):
                o, lse = _dilated_pattern(*qkv[3 * p:3 * p + 3], bias)
                outs.append(o)
                lses.append(lse)
            xs = _even_tail(xs, ya, outs, lses, s5_glu_w[e].astype(BF16), s5_glu_b[e][None, :],
                            even_w_out[e].astype(BF16), ffn, final)
        else:
            o = layer // 2
            lam_init = 0.8 - 0.6 * math.exp(-0.3 * layer)
            lam = (jnp.exp(jnp.sum(diff_lambda_q1[o] * diff_lambda_k1[o]))
                   - jnp.exp(jnp.sum(diff_lambda_q2[o] * diff_lambda_k2[o])) + lam_init)
            qt, k, vt = _odd_in_proj(xs, g_mix, odd_w_in[o].astype(BF16), batch)
            att = _diff_attention(qt, k, vt, diff_bias, lam, diff_subln[o], 1.0 - lam_init)
            xs = _odd_tail(xs, att, odd_w_out[o].astype(BF16), ffn, final)
    return xs.reshape(batch, seq, d_model)
```

```python
import functools
import math

import jax
import jax.numpy as jnp
import numpy as np
from jax import lax
from jax.experimental import pallas as pl
from jax.experimental.pallas import tpu as pltpu

F32 = jnp.float32
BF16 = jnp.bfloat16

EPS = 1e-6
NEG_INF = -1e30
LANE = 128
VMEM_LIMIT = 56 * 1024 * 1024

HEAD_DIM = 64
S5_GROUP = 16
S5_STATE = 64
S5_CHUNK = 16
S5_LANE_GROUPS = LANE // S5_GROUP
DIL_PATTERNS = ((128, 1), (512, 4), (2048, 16))
BAND_BLOCK = 128
DIL_BLOCKS_PER_STEP = 4
NUM_BUCKETS = 32
MAX_DISTANCE = 2048

TOKEN_TILE = 512
ATTN_TILE = 512
ATTN_HEADS_PER_STEP = 4
LOG2E = math.log2(math.e)
S5_ROW_TILE = 256
FFN_CHUNK = 256


def _params(*sem):
    return pltpu.CompilerParams(dimension_semantics=sem, vmem_limit_bytes=VMEM_LIMIT)


def _resident(shape):
    return pl.BlockSpec(shape, lambda *_: (0,) * len(shape), pipeline_mode=pl.Buffered(1))


def _rms_scale(x, g):
    inv = lax.rsqrt(jnp.mean(x * x, axis=-1, keepdims=True) + EPS)
    return (x * inv) * g


def _sigmoid(x):
    return 1.0 / (1.0 + jnp.exp(-x))


def _gelu_tanh(x):
    return 0.5 * x * (1.0 + jnp.tanh(math.sqrt(2.0 / math.pi) * (x + 0.044715 * (x * x * x))))


def _bucket_np(dist):
    max_exact = NUM_BUCKETS // 2
    d = np.maximum(dist, 0)
    scaled = (np.log(np.maximum(d, 1).astype(np.float64) / max_exact)
              / math.log(MAX_DISTANCE / max_exact) * (NUM_BUCKETS - max_exact))
    large = np.minimum(max_exact + scaled.astype(np.int64), NUM_BUCKETS - 1)
    return np.where(d < max_exact, d, large)


def _lookup(table, bucket):
    idx = jnp.asarray(bucket.astype(np.int8))[None]
    expand = (slice(None),) + (None,) * bucket.ndim
    out = jnp.zeros((table.shape[1],) + bucket.shape, F32)
    for b in np.unique(bucket):
        out = jnp.where(idx == b, table[int(b)].astype(F32)[expand], out)
    return out


def _even_in_kernel(x_ref, g_ref, w_ref, u_ref, *rest, width, scale, dilations):
    qkv_refs, z_scr = rest[:-1], rest[-1]
    h = _rms_scale(x_ref[...], g_ref[...]).astype(BF16)
    u_ref[...] = jnp.dot(h, w_ref[:, 0:width], preferred_element_type=F32)
    zq = jnp.dot(h, w_ref[:, width:2 * width], preferred_element_type=F32) * scale
    zkv = jnp.dot(h, w_ref[:, 2 * width:4 * width], preferred_element_type=F32)
    nb = width // LANE
    for c in range(nb):
        z_scr[c] = zq[:, c * LANE:(c + 1) * LANE]
    for c in range(2 * nb):
        z_scr[nb + c] = zkv[:, c * LANE:(c + 1) * LANE]
    tm = x_ref.shape[0]
    for p, r in enumerate(dilations):
        for res in range(r):
            rows = pl.ds(res, tm // r, stride=r) if r > 1 else slice(None)
            for c in range(3 * nb):
                qkv_refs[3 * p + c // nb][0, res, :, (c % nb) * LANE:(c % nb + 1) * LANE] = (
                    z_scr[c, rows, :].astype(BF16))


def _even_in_proj(x, g, w, batch, dilations):
    n, d = x.shape
    width = w.shape[1] // 4
    tm = TOKEN_TILE
    seq = n // batch
    per_seq = seq // tm
    row = lambda i: (i, 0)
    out_specs = [pl.BlockSpec((tm, width), row)]
    out_shape = [jax.ShapeDtypeStruct((n, width), F32)]
    for r in dilations:
        out_specs += [pl.BlockSpec((1, r, tm // r, width), lambda i: (i // per_seq, 0, i % per_seq, 0))] * 3
        out_shape += [jax.ShapeDtypeStruct((batch, r, seq // r, width), BF16)] * 3
    return pl.pallas_call(
        functools.partial(_even_in_kernel, width=width, scale=HEAD_DIM ** -0.5 * LOG2E, dilations=dilations),
        grid=(n // tm,),
        in_specs=[pl.BlockSpec((tm, d), row), _resident((1, d)), _resident(w.shape)],
        out_specs=out_specs,
        out_shape=out_shape,
        scratch_shapes=[pltpu.VMEM((3 * width // LANE, tm, LANE), F32)],
        compiler_params=_params("parallel"),
        name="even_in_proj",
    )(x, g, w)


_NT = (((1,), (1,)), ((), ()))


def _odd_in_kernel(x_ref, g_ref, wq_ref, wk_ref, wv_ref, qt_ref, k_ref, vt_ref, *, scale):
    h = _rms_scale(x_ref[...], g_ref[...]).astype(BF16)
    heads = qt_ref.shape[1]
    tm = h.shape[0]
    qt = lax.dot_general(wq_ref[...], h, _NT, preferred_element_type=F32) * scale
    qt_ref[0, :, 0] = qt.astype(BF16).reshape(heads, LANE, tm)
    k_ref[...] = jnp.dot(h, wk_ref[...], preferred_element_type=F32).astype(BF16)
    vt = lax.dot_general(wv_ref[...], h, _NT, preferred_element_type=F32)
    vt_ref[0, :, 0] = vt.astype(BF16).reshape(heads, LANE, tm)


def _odd_in_proj(x, g, w, batch):
    n, d = x.shape
    width = w.shape[1] // 3
    heads = width // LANE
    tm = ATTN_TILE
    per_seq = n // batch // tm
    wq_t = w[:, 0:width].T
    wk = w[:, width:2 * width]
    wv_t = w[:, 2 * width:3 * width].T
    row = lambda i: (i, 0)
    tmap = lambda i: (i // per_seq, 0, i % per_seq, 0, 0)
    tshape = jax.ShapeDtypeStruct((batch, heads, per_seq, LANE, tm), BF16)
    tblock = pl.BlockSpec((1, heads, 1, LANE, tm), tmap)
    return pl.pallas_call(
        functools.partial(_odd_in_kernel, scale=HEAD_DIM ** -0.5 * LOG2E),
        grid=(n // tm,),
        in_specs=[pl.BlockSpec((tm, d), row), _resident((1, d)), _resident(wq_t.shape),
                  _resident(wk.shape), _resident(wv_t.shape)],
        out_specs=[tblock, pl.BlockSpec((tm, width), row), tblock],
        out_shape=[tshape, jax.ShapeDtypeStruct((n, width), BF16), tshape],
        compiler_params=_params("parallel"),
        name="odd_in_proj",
    )(x, g, wq_t, wk, wv_t)


def _s5_matrices(lam_re, lam_im, log_dt, b_re, b_im, c_re, c_im, d_skip):
    hi = lax.Precision.HIGHEST
    L = S5_CHUNK
    G, P = lam_re.shape
    nj = G // S5_LANE_GROUPS
    dt = jnp.exp(log_dt)[:, None]
    steps = jnp.arange(L + 1, dtype=F32)[:, None, None]
    mag = jnp.exp(lam_re * dt * steps)
    ang = lam_im * dt * steps
    pr, pi = mag * jnp.cos(ang), mag * jnp.sin(ang)
    nr, ni = pr[1] - 1.0, pi[1]
    den = lam_re * lam_re + lam_im * lam_im
    cr = ((nr * lam_re + ni * lam_im) / den)[..., None]
    ci = ((ni * lam_re - nr * lam_im) / den)[..., None]
    bbr = cr * b_re - ci * b_im
    bbi = cr * b_im + ci * b_re
    car = c_re[None] * pr[:, :, None, :] - c_im[None] * pi[:, :, None, :]
    cai = c_re[None] * pi[:, :, None, :] + c_im[None] * pr[:, :, None, :]
    lg = S5_LANE_GROUPS
    ncol, nst = L * LANE, 2 * lg * P
    row_group = np.arange(lg)[:, None]
    same_col = jnp.asarray(row_group == ((np.arange(ncol) // S5_GROUP) % lg)[None, :])
    same_st = jnp.asarray(row_group == ((np.arange(nst) // P) % lg)[None, :])

    kg = (jnp.einsum('dgcp,gpe->dgce', car[:L], bbr, precision=hi)
          - jnp.einsum('dgcp,gpe->dgce', cai[:L], bbi, precision=hi))
    lag = np.arange(L)[None, :] - np.arange(L)[:, None]
    kt = jnp.where((lag >= 0)[:, :, None, None, None], kg[np.maximum(lag, 0)], 0.0)
    kt = kt.reshape(L, L, nj, lg, S5_GROUP, S5_GROUP).transpose(2, 0, 5, 1, 3, 4).reshape(nj, L, S5_GROUP, ncol)
    t_mat = jnp.where(same_col[None, None, :, None, :], kt[:, :, None, :, :], 0.0).astype(BF16)
    t_mat = t_mat.reshape(nj, ncol, ncol)

    rev = np.arange(L - 1, -1, -1)
    abr = pr[rev][..., None] * bbr[None] - pi[rev][..., None] * bbi[None]
    abi = pr[rev][..., None] * bbi[None] + pi[rev][..., None] * bbr[None]
    ab = jnp.stack([abr, abi]).reshape(2, L, nj, lg, P, S5_GROUP).transpose(2, 1, 5, 0, 3, 4)
    ab = ab.reshape(nj, L, S5_GROUP, nst)
    wst = jnp.where(same_st[None, None, :, None, :], ab[:, :, None, :, :], 0.0).astype(BF16)
    wst = wst.reshape(nj, ncol, nst)

    wo = jnp.stack([car[1:], -cai[1:]]).reshape(2, L, nj, lg, S5_GROUP, P).transpose(2, 0, 5, 1, 3, 4)
    wo = wo.reshape(nj, 2, P, ncol)
    wo = jnp.where(same_col[None, None, :, None, :], wo[:, :, None, :, :], 0.0).astype(BF16)
    wo = wo.reshape(nj, nst, ncol)

    a_chunk = jnp.stack([pr[L], pi[L]]).reshape(2, nj, S5_LANE_GROUPS * P)
    a_chunk = a_chunk.transpose(1, 0, 2).reshape(nj, 1, 2 * S5_LANE_GROUPS * P)
    skip = jnp.tile(d_skip.reshape(nj, 1, LANE), (1, 1, L))
    return t_mat, wst, wo, a_chunk, skip


def _s5_kernel(u_ref, t_ref, wst_ref, wo_ref, a_ref, skip_ref, y_ref, s_scr, h_scr, carry_scr, *, rows, half):
    @pl.when(pl.program_id(2) == 0)
    def _():
        carry_scr[...] = jnp.zeros_like(carry_scr)

    xf = jnp.concatenate([u_ref[pl.ds(t, rows, stride=S5_CHUNK), :] for t in range(S5_CHUNK)], axis=1)
    xb = xf.astype(BF16)
    s_scr[...] = jnp.dot(xb, wst_ref[0], preferred_element_type=F32)
    ar = a_ref[0, :, 0:half]
    ai = a_ref[0, :, half:2 * half]

    def step(i, carry):
        hr, hi = carry
        h_scr[pl.ds(i, 1), 0:half] = hr
        h_scr[pl.ds(i, 1), half:2 * half] = hi
        sr = s_scr[pl.ds(i, 1), 0:half]
        si = s_scr[pl.ds(i, 1), half:2 * half]
        return ar * hr - ai * hi + sr, ar * hi + ai * hr + si

    hr, hi = lax.fori_loop(0, rows, step, (carry_scr[:, 0:half], carry_scr[:, half:2 * half]), unroll=8)
    carry_scr[:, 0:half] = hr
    carry_scr[:, half:2 * half] = hi

    y = _gelu_tanh(jnp.dot(xb, t_ref[0], preferred_element_type=F32)
                   + jnp.dot(h_scr[...].astype(BF16), wo_ref[0], preferred_element_type=F32)
                   + skip_ref[0] * xf)
    for t in range(S5_CHUNK):
        y_ref[pl.ds(t, rows, stride=S5_CHUNK), :] = y[:, t * LANE:(t + 1) * LANE]


def _s5_mixer(u, mats, batch):
    t_mat, wst, wo, a_chunk, skip = mats
    n, width = u.shape
    nj = width // LANE
    cols = S5_CHUNK * LANE
    rows = S5_ROW_TILE
    tokens = rows * S5_CHUNK
    per_seq = n // batch // tokens
    nstate = wst.shape[2]
    tile = lambda j, b, k: (b * per_seq + k, j)
    per_j = lambda j, b, k: (j, 0, 0)
    return pl.pallas_call(
        functools.partial(_s5_kernel, rows=rows, half=nstate // 2),
        grid=(nj, batch, per_seq),
        in_specs=[pl.BlockSpec((tokens, LANE), tile),
                  pl.BlockSpec((1, cols, cols), per_j, pipeline_mode=pl.Buffered(1)),
                  pl.BlockSpec((1, cols, nstate), per_j, pipeline_mode=pl.Buffered(1)),
                  pl.BlockSpec((1, nstate, cols), per_j, pipeline_mode=pl.Buffered(1)),
                  pl.BlockSpec((1, 1, nstate), per_j),
                  pl.BlockSpec((1, 1, cols), per_j)],
        out_specs=pl.BlockSpec((tokens, LANE), tile),
        out_shape=jax.ShapeDtypeStruct((n, width), F32),
        scratch_shapes=[pltpu.VMEM((rows, nstate), F32), pltpu.VMEM((rows, nstate), F32),
                        pltpu.VMEM((1, nstate), F32)],
        compiler_params=_params("arbitrary", "arbitrary", "arbitrary"),
        name="s5_mixer",
    )(u, t_mat, wst, wo, a_chunk, skip)


def _dilated_bias(table, window, dilation):
    span = window // dilation
    kc = np.arange(2 * BAND_BLOCK)[:, None]
    qi = np.arange(BAND_BLOCK)[None, :]
    steps = BAND_BLOCK + qi - kc
    in_band = (steps >= 0) & (steps <= span)
    bias = _lookup(table * LOG2E, _bucket_np(np.clip(steps, 0, span) * dilation))
    rest = jnp.where(jnp.asarray(in_band)[None], bias, NEG_INF)
    first = jnp.where(jnp.asarray(in_band & (kc >= BAND_BLOCK))[None], bias, NEG_INF)
    tiles = jnp.stack([first, rest])
    return jnp.concatenate([tiles[:, 0::2], tiles[:, 1::2]], axis=3)


def _dilated_kernel(q_ref, kp_ref, kc_ref, vp_ref, vc_ref, bias_ref, o_ref, lse_ref, vt_scr, ot_scr, lt_scr,
                    s_scr, *, group, blocks_per_residue):
    first = (pl.program_id(1) * group) % blocks_per_residue == 0
    first_variant = jnp.where(first, 0, 1)
    kfull = jnp.concatenate([kp_ref[0], kc_ref[0]], axis=0)
    vfull = jnp.concatenate([vp_ref[0], vc_ref[0]], axis=0)
    vt_scr[...] = vfull.astype(F32).T.astype(BF16)
    lt_scr[...] = jnp.zeros_like(lt_scr)
    low = lax.broadcasted_iota(jnp.int32, (BAND_BLOCK, LANE), 1) < HEAD_DIM
    pairs = q_ref.shape[2] // LANE
    units = [(g, hp) for g in range(group) for hp in range(pairs)]

    def logits(g, hp):
        rows = slice(g * BAND_BLOCK, (g + 1) * BAND_BLOCK)
        cols = slice(hp * LANE, (hp + 1) * LANE)
        qq = q_ref[0, rows, cols]
        zero = jnp.zeros_like(qq)
        wt = jnp.concatenate([jnp.where(low, qq, zero), jnp.where(low, zero, qq)], axis=0)
        return lax.dot_general(kfull[g * BAND_BLOCK:(g + 2) * BAND_BLOCK, cols], wt, _NT,
                               preferred_element_type=F32)

    s_scr[0] = logits(*units[0])
    for i, (g, hp) in enumerate(units):
        if i + 1 < len(units):
            s_scr[(i + 1) % 2] = logits(*units[i + 1])
        rows = slice(g * BAND_BLOCK, (g + 1) * BAND_BLOCK)
        s = s_scr[i % 2] + bias_ref[first_variant if g == 0 else 1, hp]
        m = jnp.max(s, axis=0, keepdims=True)
        p = jnp.exp2(s - m)
        den = jnp.sum(p, axis=0, keepdims=True)
        ot = jnp.dot(vt_scr[hp * LANE:(hp + 1) * LANE, g * BAND_BLOCK:(g + 2) * BAND_BLOCK], p.astype(BF16),
                     preferred_element_type=F32)
        inv = 1.0 / den
        ot_scr[hp * LANE:hp * LANE + HEAD_DIM, rows] = ot[0:HEAD_DIM, 0:BAND_BLOCK] * inv[:, 0:BAND_BLOCK]
        ot_scr[hp * LANE + HEAD_DIM:(hp + 1) * LANE, rows] = ot[HEAD_DIM:, BAND_BLOCK:] * inv[:, BAND_BLOCK:]
        lse2 = m + jnp.log2(den)
        lt_scr[2 * hp:2 * hp + 1, rows] = lse2[:, 0:BAND_BLOCK]
        lt_scr[2 * hp + 1:2 * hp + 2, rows] = lse2[:, BAND_BLOCK:]
    o_ref[0] = ot_scr[...].T.astype(o_ref.dtype)
    lse_ref[0] = lt_scr[...].T


def _dilated_pattern(q, k, v, bias):
    batch, dilation, length, w = q.shape
    group = DIL_BLOCKS_PER_STEP
    rows = group * BAND_BLOCK
    flat = lambda t: t.reshape(batch, dilation * length, w)
    cur = lambda b, i: (b, i, 0)
    prev = lambda b, i: (b, jnp.maximum(i * group - 1, 0), 0)
    o, lse = pl.pallas_call(
        functools.partial(_dilated_kernel, group=group, blocks_per_residue=length // BAND_BLOCK),
        grid=(batch, dilation * length // rows),
        in_specs=[pl.BlockSpec((1, rows, w), cur),
                  pl.BlockSpec((1, BAND_BLOCK, w), prev), pl.BlockSpec((1, rows, w), cur),
                  pl.BlockSpec((1, BAND_BLOCK, w), prev), pl.BlockSpec((1, rows, w), cur),
                  _resident(bias.shape)],
        out_specs=[pl.BlockSpec((1, rows, w), cur), pl.BlockSpec((1, rows, LANE), cur)],
        out_shape=[jax.ShapeDtypeStruct((batch, dilation * length, w), BF16),
                   jax.ShapeDtypeStruct((batch, dilation * length, LANE), F32)],
        scratch_shapes=[pltpu.VMEM((w, rows + BAND_BLOCK), BF16), pltpu.VMEM((w, rows), F32),
                        pltpu.VMEM((LANE, rows), F32),
                        pltpu.VMEM((2, 2 * BAND_BLOCK, 2 * BAND_BLOCK), F32)],
        compiler_params=_params("parallel", "parallel"),
        name=f"dilated_attn_d{dilation}",
    )(flat(q), flat(k), flat(k), flat(v), flat(v), bias)
    return o.reshape(batch, dilation, length, w), lse.reshape(batch, dilation, length, LANE)


def _even_mix(x_ref, ya_ref, o0_ref, o1_ref, o2_ref, l0_ref, l1_ref, l2_ref, gw_ref, gb_ref, w_ref,
              o_scr, l_scr):
    ya = ya_ref[...]
    gate = jnp.dot(ya.astype(BF16), gw_ref[...], preferred_element_type=F32) + gb_ref[...]
    ya = ya * _sigmoid(gate)

    tm = x_ref.shape[0]
    nb = o0_ref.shape[3] // LANE
    for p, (o_ref, l_ref) in enumerate(((o0_ref, l0_ref), (o1_ref, l1_ref), (o2_ref, l2_ref))):
        r = o_ref.shape[1]
        for res in range(r):
            dst = pl.ds(res, tm // r, stride=r) if r > 1 else slice(None)
            for c in range(nb):
                o_scr[p * nb + c, dst, :] = o_ref[0, res, :, c * LANE:(c + 1) * LANE].astype(F32)
            l_scr[p, dst, :] = l_ref[0, res]

    l0, l1, l2 = l_scr[0], l_scr[1], l_scr[2]
    m = jnp.maximum(jnp.maximum(l0, l1), l2)
    e0, e1, e2 = jnp.exp2(l0 - m), jnp.exp2(l1 - m), jnp.exp2(l2 - m)
    inv = 1.0 / (e0 + e1 + e2)
    alphas = (e0 * inv, e1 * inv, e2 * inv)
    low = lax.broadcasted_iota(jnp.int32, (tm, LANE), 1) < HEAD_DIM
    cols = []
    for c in range(nb):
        acc = jnp.zeros((tm, LANE), F32)
        for p, a in enumerate(alphas):
            weight = jnp.where(low, a[:, 2 * c:2 * c + 1], a[:, 2 * c + 1:2 * c + 2])
            acc = acc + weight * o_scr[p * nb + c]
        cols.append(acc)

    mixed = jnp.concatenate([ya] + cols, axis=1).astype(BF16)
    return x_ref[...] + jnp.dot(mixed, w_ref[...], preferred_element_type=F32)


def _diff_bias(table, tile):
    first_const = int(np.argmax(_bucket_np(np.arange(4 * MAX_DISTANCE)) == NUM_BUCKETS - 1))
    n_near = -(-(first_const + tile - 1) // tile)
    kr = np.arange(tile)[:, None]
    qc = np.arange(tile)[None, :]
    dist = np.arange(n_near)[:, None, None] * tile + qc[None] - kr[None]
    assert n_near * tile - (tile - 1) >= first_const
    rel = (table - table[NUM_BUCKETS - 1][None, :]) * LOG2E
    return jnp.where(jnp.asarray(dist >= 0)[None], _lookup(rel, _bucket_np(dist)), NEG_INF)


def _diff_attn_kernel(lam_ref, qt_ref, k_ref, vt_ref, bias_ref, g_ref, o_ref, acc_scr, s_scr, p_scr,
                      *, tile, n_near_max, out_scale):
    qi = pl.program_id(2)
    heads = qt_ref.shape[1]
    row = lax.broadcasted_iota(jnp.int32, (LANE, tile), 0)
    q_maps = []
    for h in range(heads):
        qt = qt_ref[0, h, 0]
        zero = jnp.zeros_like(qt)
        q_maps.append((jnp.where(row < HEAD_DIM, qt, zero), jnp.where(row >= HEAD_DIM, qt, zero)))

    def keys(h, j):
        return k_ref[0, pl.ds(pl.multiple_of(j * tile, tile), tile), h * LANE:(h + 1) * LANE]

    def softmax(sa, m_prev, l_prev):
        m_new = jnp.maximum(m_prev, jnp.max(sa, axis=0, keepdims=True))
        alpha = jnp.exp2(m_prev - m_new)
        p = jnp.exp2(sa - m_new)
        return m_new, alpha * l_prev + jnp.sum(p, axis=0, keepdims=True), alpha, p.astype(BF16)

    def accumulate(h, a, j, p, alpha):
        acc_scr[h, a] = alpha * acc_scr[h, a] + jnp.dot(vt_ref[0, h, j], p, preferred_element_type=F32)

    def step(j, carry, with_bias):
        out = []
        for h in range(heads):
            m1, l1, m2, l2, alpha2 = carry[5 * h:5 * h + 5]
            accumulate(h, 1, jnp.maximum(j - 1, 0), p_scr[h], alpha2)
            s2 = jnp.dot(keys(h, j), q_maps[h][1], preferred_element_type=F32)
            s1 = s_scr[h]
            if with_bias:
                bias = bias_ref[h, qi - j]
                s1 = s1 + bias
                s2 = s2 + bias
            m1, l1, alpha1, p1 = softmax(s1, m1, l1)
            accumulate(h, 0, j, p1, alpha1)
            m2, l2, alpha2, p2 = softmax(s2, m2, l2)
            p_scr[h] = p2
            s_scr[h] = jnp.dot(keys(h, jnp.minimum(j + 1, qi)), q_maps[h][0], preferred_element_type=F32)
            out += [m1, l1, m2, l2, alpha2]
        return tuple(out)

    acc_scr[...] = jnp.zeros_like(acc_scr)
    p_scr[...] = jnp.zeros_like(p_scr)
    for h in range(heads):
        s_scr[h] = jnp.dot(keys(h, 0), q_maps[h][0], preferred_element_type=F32)
    m0 = jnp.full((1, tile), 2.0 * NEG_INF, F32)
    l0 = jnp.zeros((1, tile), F32)
    carry = (m0, l0, m0, l0, jnp.ones((1, tile), F32)) * heads
    n_near = jnp.minimum(qi + 1, n_near_max)
    carry = lax.fori_loop(0, qi + 1 - n_near, functools.partial(step, with_bias=False), carry)
    carry = lax.fori_loop(qi + 1 - n_near, qi + 1, functools.partial(step, with_bias=True), carry)
    for h in range(heads):
        _, l1, _, l2, alpha2 = carry[5 * h:5 * h + 5]
        accumulate(h, 1, qi, p_scr[h], alpha2)
        att = acc_scr[h, 0] * (1.0 / l1) - lam_ref[...] * (acc_scr[h, 1] * (1.0 / l2))
        inv = lax.rsqrt(jnp.mean(att * att, axis=0, keepdims=True) + EPS)
        out_t = ((att * inv) * g_ref[...]) * out_scale
        o_ref[0, :, h * LANE:(h + 1) * LANE] = out_t.T.astype(o_ref.dtype)


def _diff_attention(qt, k, vt, bias, lam, subln, out_scale):
    batch, heads, per_seq, _, tile = qt.shape
    n, w = k.shape
    seq = n // batch
    n_near = bias.shape[1]
    hp = ATTN_HEADS_PER_STEP
    const = lambda h, b, i: (0, 0)
    once = pl.Buffered(1)
    out = pl.pallas_call(
        functools.partial(_diff_attn_kernel, tile=tile, n_near_max=n_near, out_scale=out_scale),
        grid=(heads // hp, batch, per_seq),
        in_specs=[pl.BlockSpec((1, tile), const),
                  pl.BlockSpec((1, hp, 1, LANE, tile), lambda h, b, i: (b, h, i, 0, 0)),
                  pl.BlockSpec((1, seq, hp * LANE), lambda h, b, i: (b, 0, h), pipeline_mode=once),
                  pl.BlockSpec((1, hp, per_seq, vt.shape[3], tile), lambda h, b, i: (b, h, 0, 0, 0),
                               pipeline_mode=once),
                  pl.BlockSpec((hp, n_near, tile, tile), lambda h, b, i: (h, 0, 0, 0), pipeline_mode=once),
                  pl.BlockSpec((LANE, tile), const)],
        out_specs=pl.BlockSpec((1, tile, hp * LANE), lambda h, b, i: (b, i, h)),
        out_shape=jax.ShapeDtypeStruct((batch, seq, w), BF16),
        scratch_shapes=[pltpu.VMEM((hp, 2, LANE, tile), F32), pltpu.VMEM((hp, tile, tile), F32),
                        pltpu.VMEM((hp, tile, tile), BF16)],
        compiler_params=_params("parallel", "parallel", "arbitrary"),
        name="diff_attention",
    )(jnp.full((1, tile), lam, F32), qt, k.reshape(batch, seq, w), vt, bias,
      jnp.broadcast_to(subln[:, None], (LANE, tile)))
    return out.reshape(n, w)


def _ffn_block(x, g_ref, wg_ref, wu_ref, wd_ref, gf_ref, act_scr, final_norm):
    h = _rms_scale(x, g_ref[...]).astype(BF16)
    for c in range(0, wg_ref.shape[1], FFN_CHUNK):
        gate = jnp.dot(h, wg_ref[:, c:c + FFN_CHUNK], preferred_element_type=F32)
        up = jnp.dot(h, wu_ref[:, c:c + FFN_CHUNK], preferred_element_type=F32)
        act_scr[:, c:c + FFN_CHUNK] = ((gate * _sigmoid(gate)) * up).astype(BF16)
    y = x + jnp.dot(act_scr[...], wd_ref[...], preferred_element_type=F32)
    return _rms_scale(y, gf_ref[...]) if final_norm else y


def _even_tail_kernel(x_ref, ya_ref, o0_ref, o1_ref, o2_ref, l0_ref, l1_ref, l2_ref, gw_ref, gb_ref, w_ref,
                      g_ref, wg_ref, wu_ref, wd_ref, gf_ref, out_ref, o_scr, l_scr, act_scr, *, final_norm):
    x = _even_mix(x_ref, ya_ref, o0_ref, o1_ref, o2_ref, l0_ref, l1_ref, l2_ref, gw_ref, gb_ref, w_ref,
                  o_scr, l_scr)
    out_ref[...] = _ffn_block(x, g_ref, wg_ref, wu_ref, wd_ref, gf_ref, act_scr, final_norm)


def _odd_tail_kernel(x_ref, a_ref, w_ref, g_ref, wg_ref, wu_ref, wd_ref, gf_ref, out_ref, act_scr,
                     *, final_norm):
    x = x_ref[...] + jnp.dot(a_ref[...], w_ref[...], preferred_element_type=F32)
    out_ref[...] = _ffn_block(x, g_ref, wg_ref, wu_ref, wd_ref, gf_ref, act_scr, final_norm)


def _ffn_specs(d, ffn):
    g, w_gate, w_up, w_down, g_final = ffn
    return [_resident((1, d)), _resident(w_gate.shape), _resident(w_up.shape), _resident(w_down.shape),
            _resident((1, d))]


def _even_tail(x, ya, outs, lses, glu_w, glu_b, w_out, ffn, final_norm):
    n, d = x.shape
    batch = outs[0].shape[0]
    width = outs[0].shape[3]
    tm = TOKEN_TILE
    per_seq = n // batch // tm
    row = lambda i: (i, 0)
    grouped = lambda t: pl.BlockSpec((1, t.shape[1], tm // t.shape[1], t.shape[3]),
                                     lambda i: (i // per_seq, 0, i % per_seq, 0))
    return pl.pallas_call(
        functools.partial(_even_tail_kernel, final_norm=final_norm),
        grid=(n // tm,),
        in_specs=[pl.BlockSpec((tm, d), row),
                  pl.BlockSpec((tm, ya.shape[1]), row),
                  *[grouped(t) for t in outs], *[grouped(t) for t in lses],
                  _resident(glu_w.shape), _resident(glu_b.shape), _resident(w_out.shape),
                  *_ffn_specs(d, ffn)],
        out_specs=pl.BlockSpec((tm, d), row),
        out_shape=jax.ShapeDtypeStruct((n, d), F32),
        scratch_shapes=[pltpu.VMEM((len(outs) * width // LANE, tm, LANE), F32),
                        pltpu.VMEM((len(outs), tm, LANE), F32),
                        pltpu.VMEM((tm, ffn[1].shape[1]), BF16)],
        compiler_params=_params("parallel"),
        name="even_tail",
    )(x, ya, *outs, *lses, glu_w, glu_b, w_out, *ffn)


def _odd_tail(x, att, w_out, ffn, final_norm):
    n, d = x.shape
    tm = TOKEN_TILE
    row = lambda i: (i, 0)
    return pl.pallas_call(
        functools.partial(_odd_tail_kernel, final_norm=final_norm),
        grid=(n // tm,),
        in_specs=[pl.BlockSpec((tm, d), row), pl.BlockSpec((tm, att.shape[1]), row),
                  _resident(w_out.shape), *_ffn_specs(d, ffn)],
        out_specs=pl.BlockSpec((tm, d), row),
        out_shape=jax.ShapeDtypeStruct((n, d), F32),
        scratch_shapes=[pltpu.VMEM((tm, ffn[1].shape[1]), BF16)],
        compiler_params=_params("parallel"),
        name="odd_tail",
    )(x, att, w_out, *ffn)


def kernel(x, rel_bias, norm_mix, norm_ffn, norm_final, ffn_w_gate, ffn_w_up, ffn_w_down, even_w_in, even_w_out, s5_lambda_re, s5_lambda_im, s5_log_dt, s5_b_re, s5_b_im, s5_c_re, s5_c_im, s5_d, s5_glu_w, s5_glu_b, odd_w_in, odd_w_out, diff_lambda_q1, diff_lambda_k1, diff_lambda_q2, diff_lambda_k2, diff_subln):
    batch, seq, d_model = x.shape
    depth = norm_mix.shape[0]
    dil_heads = even_w_in.shape[2] // 4 // HEAD_DIM
    table_dil = rel_bias[:, :dil_heads]
    table_diff = rel_bias[:, dil_heads:]
    dil_biases = [_dilated_bias(table_dil, w, r) for w, r in DIL_PATTERNS]
    diff_bias = _diff_bias(table_diff, ATTN_TILE)

    xs = x.reshape(batch * seq, d_model)
    for layer in range(depth):
        g_mix = norm_mix[layer][None, :]
        final = layer == depth - 1
        ffn = (norm_ffn[layer][None, :], ffn_w_gate[layer].astype(BF16), ffn_w_up[layer].astype(BF16),
               ffn_w_down[layer].astype(BF16), norm_final[None, :])
        if layer % 2 == 0:
            e = layer // 2
            u, *qkv = _even_in_proj(xs, g_mix, even_w_in[e].astype(BF16), batch,
                                    tuple(r for _, r in DIL_PATTERNS))
            mats = _s5_matrices(s5_lambda_re[e], s5_lambda_im[e], s5_log_dt[e], s5_b_re[e], s5_b_im[e],
                                s5_c_re[e], s5_c_im[e], s5_d[e])
            ya = _s5_mixer(u, mats, batch)
            outs, lses = [], []
            for p, bias in enumerate(dil_biases):
                o, lse = _dilated_pattern(*qkv[3 * p:3 * p + 3], bias)
                outs.append(o)
                lses.append(lse)
            xs = _even_tail(xs, ya, outs, lses, s5_glu_w[e].astype(BF16), s5_glu_b[e][None, :],
                            even_w_out[e].astype(BF16), ffn, final)
        else:
            o = layer // 2
            lam_init = 0.8 - 0.6 * math.exp(-0.3 * layer)
            lam = (jnp.exp(jnp.sum(diff_lambda_q1[o] * diff_lambda_k1[o]))
                   - jnp.exp(jnp.sum(diff_lambda_q2[o] * diff_lambda_k2[o])) + lam_init)
            qt, k, vt = _odd_in_proj(xs, g_mix, odd_w_in[o].astype(BF16), batch)
            att = _diff_attention(qt, k, vt, diff_bias, lam, diff_subln[o], 1.0 - lam_init)
            xs = _odd_tail(xs, att, odd_w_out[o].astype(BF16), ffn, final)
    return xs.reshape(batch, seq, d_model)
```

```python
import functools
import math

import jax
import jax.numpy as jnp
import numpy as np
from jax import lax
from jax.experimental import pallas as pl
from jax.experimental.pallas import tpu as pltpu

F32 = jnp.float32
BF16 = jnp.bfloat16

EPS = 1e-6
NEG_INF = -1e30
LANE = 128
VMEM_LIMIT = 56 * 1024 * 1024

HEAD_DIM = 64
S5_GROUP = 16
S5_STATE = 64
S5_CHUNK = 16
S5_LANE_GROUPS = LANE // S5_GROUP
DIL_PATTERNS = ((128, 1), (512, 4), (2048, 16))
BAND_BLOCK = 128
DIL_BLOCKS_PER_STEP = 4
NUM_BUCKETS = 32
MAX_DISTANCE = 2048

TOKEN_TILE = 512
ATTN_TILE = 512
ATTN_HEADS_PER_STEP = 4
ONES_ROWS = 16
LOG2E = math.log2(math.e)
S5_ROW_TILE = 256
FFN_CHUNK = 256


def _params(*sem):
    return pltpu.CompilerParams(dimension_semantics=sem, vmem_limit_bytes=VMEM_LIMIT)


def _resident(shape):
    return pl.BlockSpec(shape, lambda *_: (0,) * len(shape), pipeline_mode=pl.Buffered(1))


def _rms_scale(x, g):
    inv = lax.rsqrt(jnp.mean(x * x, axis=-1, keepdims=True) + EPS)
    return (x * inv) * g


def _sigmoid(x):
    return 1.0 / (1.0 + jnp.exp(-x))


def _gelu_tanh(x):
    return 0.5 * x * (1.0 + jnp.tanh(math.sqrt(2.0 / math.pi) * (x + 0.044715 * (x * x * x))))


def _bucket_np(dist):
    max_exact = NUM_BUCKETS // 2
    d = np.maximum(dist, 0)
    scaled = (np.log(np.maximum(d, 1).astype(np.float64) / max_exact)
              / math.log(MAX_DISTANCE / max_exact) * (NUM_BUCKETS - max_exact))
    large = np.minimum(max_exact + scaled.astype(np.int64), NUM_BUCKETS - 1)
    return np.where(d < max_exact, d, large)


def _lookup(table, bucket):
    idx = jnp.asarray(bucket.astype(np.int8))[None]
    expand = (slice(None),) + (None,) * bucket.ndim
    out = jnp.zeros((table.shape[1],) + bucket.shape, F32)
    for b in np.unique(bucket):
        out = jnp.where(idx == b, table[int(b)].astype(F32)[expand], out)
    return out


def _even_in_kernel(x_ref, g_ref, w_ref, u_ref, *rest, width, scale, dilations):
    qkv_refs, z_scr = rest[:-1], rest[-1]
    h = _rms_scale(x_ref[...], g_ref[...]).astype(BF16)
    u_ref[...] = jnp.dot(h, w_ref[:, 0:width], preferred_element_type=F32)
    zq = jnp.dot(h, w_ref[:, width:2 * width], preferred_element_type=F32) * scale
    zkv = jnp.dot(h, w_ref[:, 2 * width:4 * width], preferred_element_type=F32)
    nb = width // LANE
    for c in range(nb):
        z_scr[c] = zq[:, c * LANE:(c + 1) * LANE]
    for c in range(2 * nb):
        z_scr[nb + c] = zkv[:, c * LANE:(c + 1) * LANE]
    tm = x_ref.shape[0]
    for p, r in enumerate(dilations):
        for res in range(r):
            rows = pl.ds(res, tm // r, stride=r) if r > 1 else slice(None)
            for c in range(3 * nb):
                qkv_refs[3 * p + c // nb][0, res, :, (c % nb) * LANE:(c % nb + 1) * LANE] = (
                    z_scr[c, rows, :].astype(BF16))


def _even_in_proj(x, g, w, batch, dilations):
    n, d = x.shape
    width = w.shape[1] // 4
    tm = TOKEN_TILE
    seq = n // batch
    per_seq = seq // tm
    row = lambda i: (i, 0)
    out_specs = [pl.BlockSpec((tm, width), row)]
    out_shape = [jax.ShapeDtypeStruct((n, width), F32)]
    for r in dilations:
        out_specs += [pl.BlockSpec((1, r, tm // r, width), lambda i: (i // per_seq, 0, i % per_seq, 0))] * 3
        out_shape += [jax.ShapeDtypeStruct((batch, r, seq // r, width), BF16)] * 3
    return pl.pallas_call(
        functools.partial(_even_in_kernel, width=width, scale=HEAD_DIM ** -0.5 * LOG2E, dilations=dilations),
        grid=(n // tm,),
        in_specs=[pl.BlockSpec((tm, d), row), _resident((1, d)), _resident(w.shape)],
        out_specs=out_specs,
        out_shape=out_shape,
        scratch_shapes=[pltpu.VMEM((3 * width // LANE, tm, LANE), F32)],
        compiler_params=_params("parallel"),
        name="even_in_proj",
    )(x, g, w)


_NT = (((1,), (1,)), ((), ()))


def _odd_in_kernel(x_ref, g_ref, wq_ref, wk_ref, wv_ref, qt_ref, k_ref, vt_ref, *, scale):
    h = _rms_scale(x_ref[...], g_ref[...]).astype(BF16)
    heads = qt_ref.shape[1]
    tm = h.shape[0]
    qt = lax.dot_general(wq_ref[...], h, _NT, preferred_element_type=F32) * scale
    qt_ref[0, :, 0] = qt.astype(BF16).reshape(heads, LANE, tm)
    k_ref[...] = jnp.dot(h, wk_ref[...], preferred_element_type=F32).astype(BF16)
    vt = lax.dot_general(wv_ref[...], h, _NT, preferred_element_type=F32)
    vt_ref[0, :, 0, 0:LANE, :] = vt.astype(BF16).reshape(heads, LANE, tm)
    vt_ref[0, :, 0, LANE:, :] = jnp.ones((heads, vt_ref.shape[3] - LANE, tm), BF16)


def _odd_in_proj(x, g, w, batch):
    n, d = x.shape
    width = w.shape[1] // 3
    heads = width // LANE
    tm = ATTN_TILE
    per_seq = n // batch // tm
    wq_t = w[:, 0:width].T
    wk = w[:, width:2 * width]
    wv_t = w[:, 2 * width:3 * width].T
    row = lambda i: (i, 0)
    tmap = lambda i: (i // per_seq, 0, i % per_seq, 0, 0)
    tshape = lambda rows: jax.ShapeDtypeStruct((batch, heads, per_seq, rows, tm), BF16)
    vrows = LANE + ONES_ROWS
    return pl.pallas_call(
        functools.partial(_odd_in_kernel, scale=HEAD_DIM ** -0.5 * LOG2E),
        grid=(n // tm,),
        in_specs=[pl.BlockSpec((tm, d), row), _resident((1, d)), _resident(wq_t.shape),
                  _resident(wk.shape), _resident(wv_t.shape)],
        out_specs=[pl.BlockSpec((1, heads, 1, LANE, tm), tmap), pl.BlockSpec((tm, width), row),
                   pl.BlockSpec((1, heads, 1, vrows, tm), tmap)],
        out_shape=[tshape(LANE), jax.ShapeDtypeStruct((n, width), BF16), tshape(vrows)],
        compiler_params=_params("parallel"),
        name="odd_in_proj",
    )(x, g, wq_t, wk, wv_t)


def _s5_matrices(lam_re, lam_im, log_dt, b_re, b_im, c_re, c_im, d_skip):
    hi = lax.Precision.HIGHEST
    L = S5_CHUNK
    G, P = lam_re.shape
    nj = G // S5_LANE_GROUPS
    dt = jnp.exp(log_dt)[:, None]
    steps = jnp.arange(L + 1, dtype=F32)[:, None, None]
    mag = jnp.exp(lam_re * dt * steps)
    ang = lam_im * dt * steps
    pr, pi = mag * jnp.cos(ang), mag * jnp.sin(ang)
    nr, ni = pr[1] - 1.0, pi[1]
    den = lam_re * lam_re + lam_im * lam_im
    cr = ((nr * lam_re + ni * lam_im) / den)[..., None]
    ci = ((ni * lam_re - nr * lam_im) / den)[..., None]
    bbr = cr * b_re - ci * b_im
    bbi = cr * b_im + ci * b_re
    car = c_re[None] * pr[:, :, None, :] - c_im[None] * pi[:, :, None, :]
    cai = c_re[None] * pi[:, :, None, :] + c_im[None] * pr[:, :, None, :]
    lg = S5_LANE_GROUPS
    ncol, nst = L * LANE, 2 * lg * P
    row_group = np.arange(lg)[:, None]
    same_col = jnp.asarray(row_group == ((np.arange(ncol) // S5_GROUP) % lg)[None, :])
    same_st = jnp.asarray(row_group == ((np.arange(nst) // P) % lg)[None, :])

    kg = (jnp.einsum('dgcp,gpe->dgce', car[:L], bbr, precision=hi)
          - jnp.einsum('dgcp,gpe->dgce', cai[:L], bbi, precision=hi))
    lag = np.arange(L)[None, :] - np.arange(L)[:, None]
    kt = jnp.where((lag >= 0)[:, :, None, None, None], kg[np.maximum(lag, 0)], 0.0)
    kt = kt.reshape(L, L, nj, lg, S5_GROUP, S5_GROUP).transpose(2, 0, 5, 1, 3, 4).reshape(nj, L, S5_GROUP, ncol)
    t_mat = jnp.where(same_col[None, None, :, None, :], kt[:, :, None, :, :], 0.0).astype(BF16)
    t_mat = t_mat.reshape(nj, ncol, ncol)

    rev = np.arange(L - 1, -1, -1)
    abr = pr[rev][..., None] * bbr[None] - pi[rev][..., None] * bbi[None]
    abi = pr[rev][..., None] * bbi[None] + pi[rev][..., None] * bbr[None]
    ab = jnp.stack([abr, abi]).reshape(2, L, nj, lg, P, S5_GROUP).transpose(2, 1, 5, 0, 3, 4)
    ab = ab.reshape(nj, L, S5_GROUP, nst)
    wst = jnp.where(same_st[None, None, :, None, :], ab[:, :, None, :, :], 0.0).astype(BF16)
    wst = wst.reshape(nj, ncol, nst)

    wo = jnp.stack([car[1:], -cai[1:]]).reshape(2, L, nj, lg, S5_GROUP, P).transpose(2, 0, 5, 1, 3, 4)
    wo = wo.reshape(nj, 2, P, ncol)
    wo = jnp.where(same_col[None, None, :, None, :], wo[:, :, None, :, :], 0.0).astype(BF16)
    wo = wo.reshape(nj, nst, ncol)

    a_chunk = jnp.stack([pr[L], pi[L]]).reshape(2, nj, S5_LANE_GROUPS * P)
    a_chunk = a_chunk.transpose(1, 0, 2).reshape(nj, 1, 2 * S5_LANE_GROUPS * P)
    skip = jnp.tile(d_skip.reshape(nj, 1, LANE), (1, 1, L))
    return t_mat, wst, wo, a_chunk, skip


def _s5_kernel(u_ref, t_ref, wst_ref, wo_ref, a_ref, skip_ref, y_ref, s_scr, h_scr, carry_scr, *, rows, half):
    @pl.when(pl.program_id(2) == 0)
    def _():
        carry_scr[...] = jnp.zeros_like(carry_scr)

    xf = jnp.concatenate([u_ref[pl.ds(t, rows, stride=S5_CHUNK), :] for t in range(S5_CHUNK)], axis=1)
    xb = xf.astype(BF16)
    s_scr[...] = jnp.dot(xb, wst_ref[0], preferred_element_type=F32)
    ar = a_ref[0, :, 0:half]
    ai = a_ref[0, :, half:2 * half]

    def step(i, carry):
        hr, hi = carry
        h_scr[pl.ds(i, 1), 0:half] = hr
        h_scr[pl.ds(i, 1), half:2 * half] = hi
        sr = s_scr[pl.ds(i, 1), 0:half]
        si = s_scr[pl.ds(i, 1), half:2 * half]
        return ar * hr - ai * hi + sr, ar * hi + ai * hr + si

    hr, hi = lax.fori_loop(0, rows, step, (carry_scr[:, 0:half], carry_scr[:, half:2 * half]), unroll=8)
    carry_scr[:, 0:half] = hr
    carry_scr[:, half:2 * half] = hi

    y = _gelu_tanh(jnp.dot(xb, t_ref[0], preferred_element_type=F32)
                   + jnp.dot(h_scr[...].astype(BF16), wo_ref[0], preferred_element_type=F32)
                   + skip_ref[0] * xf)
    for t in range(S5_CHUNK):
        y_ref[pl.ds(t, rows, stride=S5_CHUNK), :] = y[:, t * LANE:(t + 1) * LANE]


def _s5_mixer(u, mats, batch):
    t_mat, wst, wo, a_chunk, skip = mats
    n, width = u.shape
    nj = width // LANE
    cols = S5_CHUNK * LANE
    rows = S5_ROW_TILE
    tokens = rows * S5_CHUNK
    per_seq = n // batch // tokens
    nstate = wst.shape[2]
    tile = lambda j, b, k: (b * per_seq + k, j)
    per_j = lambda j, b, k: (j, 0, 0)
    return pl.pallas_call(
        functools.partial(_s5_kernel, rows=rows, half=nstate // 2),
        grid=(nj, batch, per_seq),
        in_specs=[pl.BlockSpec((tokens, LANE), tile),
                  pl.BlockSpec((1, cols, cols), per_j, pipeline_mode=pl.Buffered(1)),
                  pl.BlockSpec((1, cols, nstate), per_j, pipeline_mode=pl.Buffered(1)),
                  pl.BlockSpec((1, nstate, cols), per_j, pipeline_mode=pl.Buffered(1)),
                  pl.BlockSpec((1, 1, nstate), per_j),
                  pl.BlockSpec((1, 1, cols), per_j)],
        out_specs=pl.BlockSpec((tokens, LANE), tile),
        out_shape=jax.ShapeDtypeStruct((n, width), F32),
        scratch_shapes=[pltpu.VMEM((rows, nstate), F32), pltpu.VMEM((rows, nstate), F32),
                        pltpu.VMEM((1, nstate), F32)],
        compiler_params=_params("arbitrary", "arbitrary", "arbitrary"),
        name="s5_mixer",
    )(u, t_mat, wst, wo, a_chunk, skip)


def _dilated_bias(table, window, dilation):
    span = window // dilation
    kc = np.arange(2 * BAND_BLOCK)[:, None]
    qi = np.arange(BAND_BLOCK)[None, :]
    steps = BAND_BLOCK + qi - kc
    in_band = (steps >= 0) & (steps <= span)
    bias = _lookup(table * LOG2E, _bucket_np(np.clip(steps, 0, span) * dilation))
    rest = jnp.where(jnp.asarray(in_band)[None], bias, NEG_INF)
    first = jnp.where(jnp.asarray(in_band & (kc >= BAND_BLOCK))[None], bias, NEG_INF)
    tiles = jnp.stack([first, rest])
    return jnp.concatenate([tiles[:, 0::2], tiles[:, 1::2]], axis=3)


def _dilated_kernel(q_ref, kp_ref, kc_ref, vp_ref, vc_ref, bias_ref, o_ref, lse_ref, vt_scr, ot_scr, lt_scr,
                    s_scr, *, group, blocks_per_residue):
    first = (pl.program_id(1) * group) % blocks_per_residue == 0
    first_variant = jnp.where(first, 0, 1)
    kfull = jnp.concatenate([kp_ref[0], kc_ref[0]], axis=0)
    vfull = jnp.concatenate([vp_ref[0], vc_ref[0]], axis=0)
    vt_scr[...] = vfull.astype(F32).T.astype(BF16)
    lt_scr[...] = jnp.zeros_like(lt_scr)
    low = lax.broadcasted_iota(jnp.int32, (BAND_BLOCK, LANE), 1) < HEAD_DIM
    pairs = q_ref.shape[2] // LANE
    units = [(g, hp) for g in range(group) for hp in range(pairs)]

    def logits(g, hp):
        rows = slice(g * BAND_BLOCK, (g + 1) * BAND_BLOCK)
        cols = slice(hp * LANE, (hp + 1) * LANE)
        qq = q_ref[0, rows, cols]
        zero = jnp.zeros_like(qq)
        wt = jnp.concatenate([jnp.where(low, qq, zero), jnp.where(low, zero, qq)], axis=0)
        return lax.dot_general(kfull[g * BAND_BLOCK:(g + 2) * BAND_BLOCK, cols], wt, _NT,
                               preferred_element_type=F32)

    s_scr[0] = logits(*units[0])
    for i, (g, hp) in enumerate(units):
        if i + 1 < len(units):
            s_scr[(i + 1) % 2] = logits(*units[i + 1])
        rows = slice(g * BAND_BLOCK, (g + 1) * BAND_BLOCK)
        s = s_scr[i % 2] + bias_ref[first_variant if g == 0 else 1, hp]
        m = jnp.max(s, axis=0, keepdims=True)
        p = jnp.exp2(s - m)
        den = jnp.sum(p, axis=0, keepdims=True)
        ot = jnp.dot(vt_scr[hp * LANE:(hp + 1) * LANE, g * BAND_BLOCK:(g + 2) * BAND_BLOCK], p.astype(BF16),
                     preferred_element_type=F32)
        inv = 1.0 / den
        ot_scr[hp * LANE:hp * LANE + HEAD_DIM, rows] = ot[0:HEAD_DIM, 0:BAND_BLOCK] * inv[:, 0:BAND_BLOCK]
        ot_scr[hp * LANE + HEAD_DIM:(hp + 1) * LANE, rows] = ot[HEAD_DIM:, BAND_BLOCK:] * inv[:, BAND_BLOCK:]
        lse2 = m + jnp.log2(den)
        lt_scr[2 * hp:2 * hp + 1, rows] = lse2[:, 0:BAND_BLOCK]
        lt_scr[2 * hp + 1:2 * hp + 2, rows] = lse2[:, BAND_BLOCK:]
    o_ref[0] = ot_scr[...].T.astype(o_ref.dtype)
    lse_ref[0] = lt_scr[...].T


def _dilated_pattern(q, k, v, bias):
    batch, dilation, length, w = q.shape
    group = DIL_BLOCKS_PER_STEP
    rows = group * BAND_BLOCK
    flat = lambda t: t.reshape(batch, dilation * length, w)
    cur = lambda b, i: (b, i, 0)
    prev = lambda b, i: (b, jnp.maximum(i * group - 1, 0), 0)
    o, lse = pl.pallas_call(
        functools.partial(_dilated_kernel, group=group, blocks_per_residue=length // BAND_BLOCK),
        grid=(batch, dilation * length // rows),
        in_specs=[pl.BlockSpec((1, rows, w), cur),
                  pl.BlockSpec((1, BAND_BLOCK, w), prev), pl.BlockSpec((1, rows, w), cur),
                  pl.BlockSpec((1, BAND_BLOCK, w), prev), pl.BlockSpec((1, rows, w), cur),
                  _resident(bias.shape)],
        out_specs=[pl.BlockSpec((1, rows, w), cur), pl.BlockSpec((1, rows, LANE), cur)],
        out_shape=[jax.ShapeDtypeStruct((batch, dilation * length, w), BF16),
                   jax.ShapeDtypeStruct((batch, dilation * length, LANE), F32)],
        scratch_shapes=[pltpu.VMEM((w, rows + BAND_BLOCK), BF16), pltpu.VMEM((w, rows), F32),
                        pltpu.VMEM((LANE, rows), F32),
                        pltpu.VMEM((2, 2 * BAND_BLOCK, 2 * BAND_BLOCK), F32)],
        compiler_params=_params("parallel", "parallel"),
        name=f"dilated_attn_d{dilation}",
    )(flat(q), flat(k), flat(k), flat(v), flat(v), bias)
    return o.reshape(batch, dilation, length, w), lse.reshape(batch, dilation, length, LANE)


def _even_mix(x_ref, ya_ref, o0_ref, o1_ref, o2_ref, l0_ref, l1_ref, l2_ref, gw_ref, gb_ref, w_ref,
              o_scr, l_scr):
    ya = ya_ref[...]
    gate = jnp.dot(ya.astype(BF16), gw_ref[...], preferred_element_type=F32) + gb_ref[...]
    ya = ya * _sigmoid(gate)

    tm = x_ref.shape[0]
    nb = o0_ref.shape[3] // LANE
    for p, (o_ref, l_ref) in enumerate(((o0_ref, l0_ref), (o1_ref, l1_ref), (o2_ref, l2_ref))):
        r = o_ref.shape[1]
        for res in range(r):
            dst = pl.ds(res, tm // r, stride=r) if r > 1 else slice(None)
            for c in range(nb):
                o_scr[p * nb + c, dst, :] = o_ref[0, res, :, c * LANE:(c + 1) * LANE].astype(F32)
            l_scr[p, dst, :] = l_ref[0, res]

    l0, l1, l2 = l_scr[0], l_scr[1], l_scr[2]
    m = jnp.maximum(jnp.maximum(l0, l1), l2)
    e0, e1, e2 = jnp.exp2(l0 - m), jnp.exp2(l1 - m), jnp.exp2(l2 - m)
    inv = 1.0 / (e0 + e1 + e2)
    alphas = (e0 * inv, e1 * inv, e2 * inv)
    low = lax.broadcasted_iota(jnp.int32, (tm, LANE), 1) < HEAD_DIM
    cols = []
    for c in range(nb):
        acc = jnp.zeros((tm, LANE), F32)
        for p, a in enumerate(alphas):
            weight = jnp.where(low, a[:, 2 * c:2 * c + 1], a[:, 2 * c + 1:2 * c + 2])
            acc = acc + weight * o_scr[p * nb + c]
        cols.append(acc)

    mixed = jnp.concatenate([ya] + cols, axis=1).astype(BF16)
    return x_ref[...] + jnp.dot(mixed, w_ref[...], preferred_element_type=F32)


def _diff_bias(table, tile):
    first_const = int(np.argmax(_bucket_np(np.arange(4 * MAX_DISTANCE)) == NUM_BUCKETS - 1))
    n_near = -(-(first_const + tile - 1) // tile)
    kr = np.arange(tile)[:, None]
    qc = np.arange(tile)[None, :]
    dist = np.arange(n_near)[:, None, None] * tile + qc[None] - kr[None]
    assert n_near * tile - (tile - 1) >= first_const
    rel = (table - table[NUM_BUCKETS - 1][None, :]) * LOG2E
    return jnp.where(jnp.asarray(dist >= 0)[None], _lookup(rel, _bucket_np(dist)), NEG_INF)


def _diff_attn_kernel(lam_ref, qt_ref, k_ref, vt_ref, bias_ref, g_ref, o_ref, acc_scr, s_scr, p_scr,
                      *, tile, n_near_max, out_scale):
    qi = pl.program_id(2)
    heads = qt_ref.shape[1]
    row = lax.broadcasted_iota(jnp.int32, (LANE, tile), 0)
    q_maps = []
    for h in range(heads):
        qt = qt_ref[0, h, 0]
        zero = jnp.zeros_like(qt)
        q_maps.append((jnp.where(row < HEAD_DIM, qt, zero), jnp.where(row >= HEAD_DIM, qt, zero)))

    def keys(h, j):
        return k_ref[0, pl.ds(pl.multiple_of(j * tile, tile), tile), h * LANE:(h + 1) * LANE]

    def softmax(sa, m_prev):
        m_new = jnp.maximum(m_prev, jnp.max(sa, axis=0, keepdims=True))
        return m_new, jnp.exp2(m_prev - m_new), jnp.exp2(sa - m_new).astype(BF16)

    def accumulate(h, a, j, p, alpha, l_prev):
        r = jnp.dot(vt_ref[0, h, j], p, preferred_element_type=F32)
        acc_scr[h, a] = alpha * acc_scr[h, a] + r[0:LANE]
        return alpha * l_prev + r[LANE:LANE + 1]

    def logits(h, a, j):
        s_scr[h, a] = jnp.dot(keys(h, j), q_maps[h][a], preferred_element_type=F32)

    def step(j, carry, with_bias, lookahead):
        out = []
        for h in range(heads):
            stats = list(carry[6 * h:6 * h + 6])
            for a in range(2):
                m, l, alpha = stats[3 * a:3 * a + 3]
                l = accumulate(h, a, jnp.maximum(j - 1, 0), p_scr[h, a], alpha, l)
                s = s_scr[h, a]
                if with_bias:
                    s = s + bias_ref[h, qi - j]
                m, alpha, p = softmax(s, m)
                p_scr[h, a] = p
                stats[3 * a:3 * a + 3] = [m, l, alpha]
            if lookahead:
                for a in range(2):
                    logits(h, a, j + 1)
            out += stats
        return tuple(out)

    acc_scr[...] = jnp.zeros_like(acc_scr)
    p_scr[...] = jnp.zeros_like(p_scr)
    for h in range(heads):
        for a in range(2):
            logits(h, a, 0)
    m0 = jnp.full((1, tile), 2.0 * NEG_INF, F32)
    l0 = jnp.zeros((1, tile), F32)
    carry = (m0, l0, jnp.ones((1, tile), F32)) * (2 * heads)
    n_far = qi + 1 - jnp.minimum(qi + 1, n_near_max)
    carry = lax.fori_loop(0, n_far, functools.partial(step, with_bias=False, lookahead=True), carry)
    carry = lax.fori_loop(n_far, qi, functools.partial(step, with_bias=True, lookahead=True), carry)
    carry = step(qi, carry, with_bias=True, lookahead=False)
    for h in range(heads):
        _, l1, alpha1, _, l2, alpha2 = carry[6 * h:6 * h + 6]
        l1 = accumulate(h, 0, qi, p_scr[h, 0], alpha1, l1)
        l2 = accumulate(h, 1, qi, p_scr[h, 1], alpha2, l2)
        att = acc_scr[h, 0] * (1.0 / l1) - lam_ref[...] * (acc_scr[h, 1] * (1.0 / l2))
        inv = lax.rsqrt(jnp.mean(att * att, axis=0, keepdims=True) + EPS)
        out_t = ((att * inv) * g_ref[...]) * out_scale
        o_ref[0, :, h * LANE:(h + 1) * LANE] = out_t.T.astype(o_ref.dtype)


def _diff_attention(qt, k, vt, bias, lam, subln, out_scale):
    batch, heads, per_seq, _, tile = qt.shape
    n, w = k.shape
    seq = n // batch
    n_near = bias.shape[1]
    hp = ATTN_HEADS_PER_STEP
    const = lambda h, b, i: (0, 0)
    once = pl.Buffered(1)
    out = pl.pallas_call(
        functools.partial(_diff_attn_kernel, tile=tile, n_near_max=n_near, out_scale=out_scale),
        grid=(heads // hp, batch, per_seq),
        in_specs=[pl.BlockSpec((1, tile), const),
                  pl.BlockSpec((1, hp, 1, LANE, tile), lambda h, b, i: (b, h, i, 0, 0)),
                  pl.BlockSpec((1, seq, hp * LANE), lambda h, b, i: (b, 0, h), pipeline_mode=once),
                  pl.BlockSpec((1, hp, per_seq, vt.shape[3], tile), lambda h, b, i: (b, h, 0, 0, 0),
                               pipeline_mode=once),
                  pl.BlockSpec((hp, n_near, tile, tile), lambda h, b, i: (h, 0, 0, 0), pipeline_mode=once),
                  pl.BlockSpec((LANE, tile), const)],
        out_specs=pl.BlockSpec((1, tile, hp * LANE), lambda h, b, i: (b, i, h)),
        out_shape=jax.ShapeDtypeStruct((batch, seq, w), BF16),
        scratch_shapes=[pltpu.VMEM((hp, 2, LANE, tile), F32), pltpu.VMEM((hp, 2, tile, tile), F32),
                        pltpu.VMEM((hp, 2, tile, tile), BF16)],
        compiler_params=_params("parallel", "parallel", "arbitrary"),
        name="diff_attention",
    )(jnp.full((1, tile), lam, F32), qt, k.reshape(batch, seq, w), vt, bias,
      jnp.broadcast_to(subln[:, None], (LANE, tile)))
    return out.reshape(n, w)


def _ffn_block(x, g_ref, wg_ref, wu_ref, wd_ref, gf_ref, act_scr, final_norm):
    h = _rms_scale(x, g_ref[...]).astype(BF16)
    for c in range(0, wg_ref.shape[1], FFN_CHUNK):
        gate = jnp.dot(h, wg_ref[:, c:c + FFN_CHUNK], preferred_element_type=F32)
        up = jnp.dot(h, wu_ref[:, c:c + FFN_CHUNK], preferred_element_type=F32)
        act_scr[:, c:c + FFN_CHUNK] = ((gate * _sigmoid(gate)) * up).astype(BF16)
    y = x + jnp.dot(act_scr[...], wd_ref[...], preferred_element_type=F32)
    return _rms_scale(y, gf_ref[...]) if final_norm else y


def _even_tail_kernel(x_ref, ya_ref, o0_ref, o1_ref, o2_ref, l0_ref, l1_ref, l2_ref, gw_ref, gb_ref, w_ref,
                      g_ref, wg_ref, wu_ref, wd_ref, gf_ref, out_ref, o_scr, l_scr, act_scr, *, final_norm):
    x = _even_mix(x_ref, ya_ref, o0_ref, o1_ref, o2_ref, l0_ref, l1_ref, l2_ref, gw_ref, gb_ref, w_ref,
                  o_scr, l_scr)
    out_ref[...] = _ffn_block(x, g_ref, wg_ref, wu_ref, wd_ref, gf_ref, act_scr, final_norm)


def _odd_tail_kernel(x_ref, a_ref, w_ref, g_ref, wg_ref, wu_ref, wd_ref, gf_ref, out_ref, act_scr,
                     *, final_norm):
    x = x_ref[...] + jnp.dot(a_ref[...], w_ref[...], preferred_element_type=F32)
    out_ref[...] = _ffn_block(x, g_ref, wg_ref, wu_ref, wd_ref, gf_ref, act_scr, final_norm)


def _ffn_specs(d, ffn):
    g, w_gate, w_up, w_down, g_final = ffn
    return [_resident((1, d)), _resident(w_gate.shape), _resident(w_up.shape), _resident(w_down.shape),
            _resident((1, d))]


def _even_tail(x, ya, outs, lses, glu_w, glu_b, w_out, ffn, final_norm):
    n, d = x.shape
    batch = outs[0].shape[0]
    width = outs[0].shape[3]
    tm = TOKEN_TILE
    per_seq = n // batch // tm
    row = lambda i: (i, 0)
    grouped = lambda t: pl.BlockSpec((1, t.shape[1], tm // t.shape[1], t.shape[3]),
                                     lambda i: (i // per_seq, 0, i % per_seq, 0))
    return pl.pallas_call(
        functools.partial(_even_tail_kernel, final_norm=final_norm),
        grid=(n // tm,),
        in_specs=[pl.BlockSpec((tm, d), row),
                  pl.BlockSpec((tm, ya.shape[1]), row),
                  *[grouped(t) for t in outs], *[grouped(t) for t in lses],
                  _resident(glu_w.shape), _resident(glu_b.shape), _resident(w_out.shape),
                  *_ffn_specs(d, ffn)],
        out_specs=pl.BlockSpec((tm, d), row),
        out_shape=jax.ShapeDtypeStruct((n, d), F32),
        scratch_shapes=[pltpu.VMEM((len(outs) * width // LANE, tm, LANE), F32),
                        pltpu.VMEM((len(outs), tm, LANE), F32),
                        pltpu.VMEM((tm, ffn[1].shape[1]), BF16)],
        compiler_params=_params("parallel"),
        name="even_tail",
    )(x, ya, *outs, *lses, glu_w, glu_b, w_out, *ffn)


def _odd_tail(x, att, w_out, ffn, final_norm):
    n, d = x.shape
    tm = TOKEN_TILE
    row = lambda i: (i, 0)
    return pl.pallas_call(
        functools.partial(_odd_tail_kernel, final_norm=final_norm),
        grid=(n // tm,),
        in_specs=[pl.BlockSpec((tm, d), row), pl.BlockSpec((tm, att.shape[1]), row),
                  _resident(w_out.shape), *_ffn_specs(d, ffn)],
        out_specs=pl.BlockSpec((tm, d), row),
        out_shape=jax.ShapeDtypeStruct((n, d), F32),
        scratch_shapes=[pltpu.VMEM((tm, ffn[1].shape[1]), BF16)],
        compiler_params=_params("parallel"),
        name="odd_tail",
    )(x, att, w_out, *ffn)


def kernel(x, rel_bias, norm_mix, norm_ffn, norm_final, ffn_w_gate, ffn_w_up, ffn_w_down, even_w_in, even_w_out, s5_lambda_re, s5_lambda_im, s5_log_dt, s5_b_re, s5_b_im, s5_c_re, s5_c_im, s5_d, s5_glu_w, s5_glu_b, odd_w_in, odd_w_out, diff_lambda_q1, diff_lambda_k1, diff_lambda_q2, diff_lambda_k2, diff_subln):
    batch, seq, d_model = x.shape
    depth = norm_mix.shape[0]
    dil_heads = even_w_in.shape[2] // 4 // HEAD_DIM
    table_dil = rel_bias[:, :dil_heads]
    table_diff = rel_bias[:, dil_heads:]
    dil_biases = [_dilated_bias(table_dil, w, r) for w, r in DIL_PATTERNS]
    diff_bias = _diff_bias(table_diff, ATTN_TILE)

    xs = x.reshape(batch * seq, d_model)
    for layer in range(depth):
        g_mix = norm_mix[layer][None, :]
        final = layer == depth - 1
        ffn = (norm_ffn[layer][None, :], ffn_w_gate[layer].astype(BF16), ffn_w_up[layer].astype(BF16),
               ffn_w_down[layer].astype(BF16), norm_final[None, :])
        if layer % 2 == 0:
            e = layer // 2
            u, *qkv = _even_in_proj(xs, g_mix, even_w_in[e].astype(BF16), batch,
                                    tuple(r for _, r in DIL_PATTERNS))
            mats = _s5_matrices(s5_lambda_re[e], s5_lambda_im[e], s5_log_dt[e], s5_b_re[e], s5_b_im[e],
                                s5_c_re[e], s5_c_im[e], s5_d[e])
            ya = _s5_mixer(u, mats, batch)
            outs, lses = [], []
            for p, bias in enumerate(dil_biases):
                o, lse = _dilated_pattern(*qkv[3 * p:3 * p + 3], bias)
                outs.append(o)
                lses.append(lse)
            xs = _even_tail(xs, ya, outs, lses, s5_glu_w[e].astype(BF16), s5_glu_b[e][None, :],
                            even_w_out[e].astype(BF16), ffn, final)
        else:
            o = layer // 2
            lam_init = 0.8 - 0.6 * math.exp(-0.3 * layer)
            lam = (jnp.exp(jnp.sum(diff_lambda_q1[o] * diff_lambda_k1[o]))
                   - jnp.exp(jnp.sum(diff_lambda_q2[o] * diff_lambda_k2[o])) + lam_init)
            qt, k, vt = _odd_in_proj(xs, g_mix, odd_w_in[o].astype(BF16), batch)
            att = _diff_attention(qt, k, vt, diff_bias, lam, diff_subln[o], 1.0 - lam_init)
            xs = _odd_tail(xs, att, odd_w_out[o].astype(BF16), ffn, final)
    return xs.reshape(batch, seq, d_model)
```

```python
import functools
import math

import jax
import jax.numpy as jnp
import numpy as np
from jax import lax
from jax.experimental import pallas as pl
from jax.experimental.pallas import tpu as pltpu

F32 = jnp.float32
BF16 = jnp.bfloat16

EPS = 1e-6
NEG_INF = -1e30
LANE = 128
VMEM_LIMIT = 56 * 1024 * 1024

HEAD_DIM = 64
S5_GROUP = 16
S5_STATE = 64
S5_CHUNK = 16
S5_LANE_GROUPS = LANE // S5_GROUP
DIL_PATTERNS = ((128, 1), (512, 4), (2048, 16))
BAND_BLOCK = 128
DIL_BLOCKS_PER_STEP = 4
DIL_LOOKAHEAD = 3
NUM_BUCKETS = 32
MAX_DISTANCE = 2048

TOKEN_TILE = 512
ATTN_TILE = 512
ATTN_HEADS_PER_STEP = 4
ONES_ROWS = 16
LOG2E = math.log2(math.e)
S5_ROW_TILE = 256
FFN_CHUNK = 256


def _params(*sem):
    return pltpu.CompilerParams(dimension_semantics=sem, vmem_limit_bytes=VMEM_LIMIT)


def _resident(shape):
    return pl.BlockSpec(shape, lambda *_: (0,) * len(shape), pipeline_mode=pl.Buffered(1))


def _rms_scale(x, g):
    inv = lax.rsqrt(jnp.mean(x * x, axis=-1, keepdims=True) + EPS)
    return (x * inv) * g


def _sigmoid(x):
    return 1.0 / (1.0 + jnp.exp(-x))


def _gelu_tanh(x):
    return 0.5 * x * (1.0 + jnp.tanh(math.sqrt(2.0 / math.pi) * (x + 0.044715 * (x * x * x))))


def _bucket_np(dist):
    max_exact = NUM_BUCKETS // 2
    d = np.maximum(dist, 0)
    scaled = (np.log(np.maximum(d, 1).astype(np.float64) / max_exact)
              / math.log(MAX_DISTANCE / max_exact) * (NUM_BUCKETS - max_exact))
    large = np.minimum(max_exact + scaled.astype(np.int64), NUM_BUCKETS - 1)
    return np.where(d < max_exact, d, large)


def _lookup(table, bucket):
    idx = jnp.asarray(bucket.astype(np.int8))[None]
    expand = (slice(None),) + (None,) * bucket.ndim
    out = jnp.zeros((table.shape[1],) + bucket.shape, F32)
    for b in np.unique(bucket):
        out = jnp.where(idx == b, table[int(b)].astype(F32)[expand], out)
    return out


def _even_in_kernel(x_ref, g_ref, w_ref, u_ref, *rest, width, scale, dilations):
    qkv_refs, z_scr = rest[:-1], rest[-1]
    h = _rms_scale(x_ref[...], g_ref[...]).astype(BF16)
    u_ref[...] = jnp.dot(h, w_ref[:, 0:width], preferred_element_type=F32)
    zq = jnp.dot(h, w_ref[:, width:2 * width], preferred_element_type=F32) * scale
    zkv = jnp.dot(h, w_ref[:, 2 * width:4 * width], preferred_element_type=F32)
    nb = width // LANE
    for c in range(nb):
        z_scr[c] = zq[:, c * LANE:(c + 1) * LANE]
    for c in range(2 * nb):
        z_scr[nb + c] = zkv[:, c * LANE:(c + 1) * LANE]
    tm = x_ref.shape[0]
    for p, r in enumerate(dilations):
        for res in range(r):
            rows = pl.ds(res, tm // r, stride=r) if r > 1 else slice(None)
            for c in range(3 * nb):
                qkv_refs[3 * p + c // nb][0, res, :, (c % nb) * LANE:(c % nb + 1) * LANE] = (
                    z_scr[c, rows, :].astype(BF16))


def _even_in_proj(x, g, w, batch, dilations):
    n, d = x.shape
    width = w.shape[1] // 4
    tm = TOKEN_TILE
    seq = n // batch
    per_seq = seq // tm
    row = lambda i: (i, 0)
    out_specs = [pl.BlockSpec((tm, width), row)]
    out_shape = [jax.ShapeDtypeStruct((n, width), F32)]
    for r in dilations:
        out_specs += [pl.BlockSpec((1, r, tm // r, width), lambda i: (i // per_seq, 0, i % per_seq, 0))] * 3
        out_shape += [jax.ShapeDtypeStruct((batch, r, seq // r, width), BF16)] * 3
    return pl.pallas_call(
        functools.partial(_even_in_kernel, width=width, scale=HEAD_DIM ** -0.5 * LOG2E, dilations=dilations),
        grid=(n // tm,),
        in_specs=[pl.BlockSpec((tm, d), row), _resident((1, d)), _resident(w.shape)],
        out_specs=out_specs,
        out_shape=out_shape,
        scratch_shapes=[pltpu.VMEM((3 * width // LANE, tm, LANE), F32)],
        compiler_params=_params("parallel"),
        name="even_in_proj",
    )(x, g, w)


_NT = (((1,), (1,)), ((), ()))


def _odd_in_kernel(x_ref, g_ref, wq_ref, wk_ref, wv_ref, qt_ref, k_ref, vt_ref, *, scale):
    h = _rms_scale(x_ref[...], g_ref[...]).astype(BF16)
    heads = qt_ref.shape[1]
    tm = h.shape[0]
    qt = lax.dot_general(wq_ref[...], h, _NT, preferred_element_type=F32) * scale
    qt_ref[0, :, 0] = qt.astype(BF16).reshape(heads, LANE, tm)
    k_ref[...] = jnp.dot(h, wk_ref[...], preferred_element_type=F32).astype(BF16)
    vt = lax.dot_general(wv_ref[...], h, _NT, preferred_element_type=F32)
    vt_ref[0, :, 0, 0:LANE, :] = vt.astype(BF16).reshape(heads, LANE, tm)
    vt_ref[0, :, 0, LANE:, :] = jnp.ones((heads, vt_ref.shape[3] - LANE, tm), BF16)


def _odd_in_proj(x, g, w, batch):
    n, d = x.shape
    width = w.shape[1] // 3
    heads = width // LANE
    tm = ATTN_TILE
    per_seq = n // batch // tm
    wq_t = w[:, 0:width].T
    wk = w[:, width:2 * width]
    wv_t = w[:, 2 * width:3 * width].T
    row = lambda i: (i, 0)
    tmap = lambda i: (i // per_seq, 0, i % per_seq, 0, 0)
    tshape = lambda rows: jax.ShapeDtypeStruct((batch, heads, per_seq, rows, tm), BF16)
    vrows = LANE + ONES_ROWS
    return pl.pallas_call(
        functools.partial(_odd_in_kernel, scale=HEAD_DIM ** -0.5 * LOG2E),
        grid=(n // tm,),
        in_specs=[pl.BlockSpec((tm, d), row), _resident((1, d)), _resident(wq_t.shape),
                  _resident(wk.shape), _resident(wv_t.shape)],
        out_specs=[pl.BlockSpec((1, heads, 1, LANE, tm), tmap), pl.BlockSpec((tm, width), row),
                   pl.BlockSpec((1, heads, 1, vrows, tm), tmap)],
        out_shape=[tshape(LANE), jax.ShapeDtypeStruct((n, width), BF16), tshape(vrows)],
        compiler_params=_params("parallel"),
        name="odd_in_proj",
    )(x, g, wq_t, wk, wv_t)


def _s5_matrices(lam_re, lam_im, log_dt, b_re, b_im, c_re, c_im, d_skip):
    hi = lax.Precision.HIGHEST
    L = S5_CHUNK
    G, P = lam_re.shape
    nj = G // S5_LANE_GROUPS
    dt = jnp.exp(log_dt)[:, None]
    steps = jnp.arange(L + 1, dtype=F32)[:, None, None]
    mag = jnp.exp(lam_re * dt * steps)
    ang = lam_im * dt * steps
    pr, pi = mag * jnp.cos(ang), mag * jnp.sin(ang)
    nr, ni = pr[1] - 1.0, pi[1]
    den = lam_re * lam_re + lam_im * lam_im
    cr = ((nr * lam_re + ni * lam_im) / den)[..., None]
    ci = ((ni * lam_re - nr * lam_im) / den)[..., None]
    bbr = cr * b_re - ci * b_im
    bbi = cr * b_im + ci * b_re
    car = c_re[None] * pr[:, :, None, :] - c_im[None] * pi[:, :, None, :]
    cai = c_re[None] * pi[:, :, None, :] + c_im[None] * pr[:, :, None, :]
    lg = S5_LANE_GROUPS
    ncol, nst = L * LANE, 2 * lg * P
    row_group = np.arange(lg)[:, None]
    same_col = jnp.asarray(row_group == ((np.arange(ncol) // S5_GROUP) % lg)[None, :])
    same_st = jnp.asarray(row_group == ((np.arange(nst) // P) % lg)[None, :])

    kg = (jnp.einsum('dgcp,gpe->dgce', car[:L], bbr, precision=hi)
          - jnp.einsum('dgcp,gpe->dgce', cai[:L], bbi, precision=hi))
    lag = np.arange(L)[None, :] - np.arange(L)[:, None]
    kt = jnp.where((lag >= 0)[:, :, None, None, None], kg[np.maximum(lag, 0)], 0.0)
    kt = kt.reshape(L, L, nj, lg, S5_GROUP, S5_GROUP).transpose(2, 0, 5, 1, 3, 4).reshape(nj, L, S5_GROUP, ncol)
    t_mat = jnp.where(same_col[None, None, :, None, :], kt[:, :, None, :, :], 0.0).astype(BF16)
    t_mat = t_mat.reshape(nj, ncol, ncol)

    rev = np.arange(L - 1, -1, -1)
    abr = pr[rev][..., None] * bbr[None] - pi[rev][..., None] * bbi[None]
    abi = pr[rev][..., None] * bbi[None] + pi[rev][..., None] * bbr[None]
    ab = jnp.stack([abr, abi]).reshape(2, L, nj, lg, P, S5_GROUP).transpose(2, 1, 5, 0, 3, 4)
    ab = ab.reshape(nj, L, S5_GROUP, nst)
    wst = jnp.where(same_st[None, None, :, None, :], ab[:, :, None, :, :], 0.0).astype(BF16)
    wst = wst.reshape(nj, ncol, nst)

    wo = jnp.stack([car[1:], -cai[1:]]).reshape(2, L, nj, lg, S5_GROUP, P).transpose(2, 0, 5, 1, 3, 4)
    wo = wo.reshape(nj, 2, P, ncol)
    wo = jnp.where(same_col[None, None, :, None, :], wo[:, :, None, :, :], 0.0).astype(BF16)
    wo = wo.reshape(nj, nst, ncol)

    a_chunk = jnp.stack([pr[L], pi[L]]).reshape(2, nj, S5_LANE_GROUPS * P)
    a_chunk = a_chunk.transpose(1, 0, 2).reshape(nj, 1, 2 * S5_LANE_GROUPS * P)
    skip = jnp.tile(d_skip.reshape(nj, 1, LANE), (1, 1, L))
    return t_mat, wst, wo, a_chunk, skip


def _s5_kernel(u_ref, t_ref, wst_ref, wo_ref, a_ref, skip_ref, y_ref, s_scr, h_scr, carry_scr, *, rows, half):
    @pl.when(pl.program_id(2) == 0)
    def _():
        carry_scr[...] = jnp.zeros_like(carry_scr)

    xf = jnp.concatenate([u_ref[pl.ds(t, rows, stride=S5_CHUNK), :] for t in range(S5_CHUNK)], axis=1)
    xb = xf.astype(BF16)
    s_scr[...] = jnp.dot(xb, wst_ref[0], preferred_element_type=F32)
    ar = a_ref[0, :, 0:half]
    ai = a_ref[0, :, half:2 * half]

    def step(i, carry):
        hr, hi = carry
        h_scr[pl.ds(i, 1), 0:half] = hr
        h_scr[pl.ds(i, 1), half:2 * half] = hi
        sr = s_scr[pl.ds(i, 1), 0:half]
        si = s_scr[pl.ds(i, 1), half:2 * half]
        return ar * hr - ai * hi + sr, ar * hi + ai * hr + si

    hr, hi = lax.fori_loop(0, rows, step, (carry_scr[:, 0:half], carry_scr[:, half:2 * half]), unroll=8)
    carry_scr[:, 0:half] = hr
    carry_scr[:, half:2 * half] = hi

    y = _gelu_tanh(jnp.dot(xb, t_ref[0], preferred_element_type=F32)
                   + jnp.dot(h_scr[...].astype(BF16), wo_ref[0], preferred_element_type=F32)
                   + skip_ref[0] * xf)
    for t in range(S5_CHUNK):
        y_ref[pl.ds(t, rows, stride=S5_CHUNK), :] = y[:, t * LANE:(t + 1) * LANE]


def _s5_mixer(u, mats, batch):
    t_mat, wst, wo, a_chunk, skip = mats
    n, width = u.shape
    nj = width // LANE
    cols = S5_CHUNK * LANE
    rows = S5_ROW_TILE
    tokens = rows * S5_CHUNK
    per_seq = n // batch // tokens
    nstate = wst.shape[2]
    tile = lambda j, b, k: (b * per_seq + k, j)
    per_j = lambda j, b, k: (j, 0, 0)
    return pl.pallas_call(
        functools.partial(_s5_kernel, rows=rows, half=nstate // 2),
        grid=(nj, batch, per_seq),
        in_specs=[pl.BlockSpec((tokens, LANE), tile),
                  pl.BlockSpec((1, cols, cols), per_j, pipeline_mode=pl.Buffered(1)),
                  pl.BlockSpec((1, cols, nstate), per_j, pipeline_mode=pl.Buffered(1)),
                  pl.BlockSpec((1, nstate, cols), per_j, pipeline_mode=pl.Buffered(1)),
                  pl.BlockSpec((1, 1, nstate), per_j),
                  pl.BlockSpec((1, 1, cols), per_j)],
        out_specs=pl.BlockSpec((tokens, LANE), tile),
        out_shape=jax.ShapeDtypeStruct((n, width), F32),
        scratch_shapes=[pltpu.VMEM((rows, nstate), F32), pltpu.VMEM((rows, nstate), F32),
                        pltpu.VMEM((1, nstate), F32)],
        compiler_params=_params("arbitrary", "arbitrary", "arbitrary"),
        name="s5_mixer",
    )(u, t_mat, wst, wo, a_chunk, skip)


def _dilated_bias(table, window, dilation):
    span = window // dilation
    kc = np.arange(2 * BAND_BLOCK)[:, None]
    qi = np.arange(BAND_BLOCK)[None, :]
    steps = BAND_BLOCK + qi - kc
    in_band = (steps >= 0) & (steps <= span)
    bias = _lookup(table * LOG2E, _bucket_np(np.clip(steps, 0, span) * dilation))
    rest = jnp.where(jnp.asarray(in_band)[None], bias, NEG_INF)
    first = jnp.where(jnp.asarray(in_band & (kc >= BAND_BLOCK))[None], bias, NEG_INF)
    tiles = jnp.stack([first, rest])
    return jnp.concatenate([tiles[:, 0::2], tiles[:, 1::2]], axis=3)


def _dilated_kernel(q_ref, kp_ref, kc_ref, vp_ref, vc_ref, bias_ref, o_ref, lse_ref, vt_scr, ot_scr, lt_scr,
                    s_scr, *, group, blocks_per_residue):
    first = (pl.program_id(1) * group) % blocks_per_residue == 0
    first_variant = jnp.where(first, 0, 1)
    kfull = jnp.concatenate([kp_ref[0], kc_ref[0]], axis=0)
    vfull = jnp.concatenate([vp_ref[0], vc_ref[0]], axis=0)
    vt_scr[...] = vfull.astype(F32).T.astype(BF16)
    lt_scr[...] = jnp.zeros_like(lt_scr)
    low = lax.broadcasted_iota(jnp.int32, (BAND_BLOCK, LANE), 1) < HEAD_DIM
    pairs = q_ref.shape[2] // LANE
    units = [(g, hp) for g in range(group) for hp in range(pairs)]

    def logits(g, hp):
        rows = slice(g * BAND_BLOCK, (g + 1) * BAND_BLOCK)
        cols = slice(hp * LANE, (hp + 1) * LANE)
        qq = q_ref[0, rows, cols]
        zero = jnp.zeros_like(qq)
        kk = kfull[g * BAND_BLOCK:(g + 2) * BAND_BLOCK, cols]
        return jnp.concatenate(
            [lax.dot_general(kk, jnp.where(low, qq, zero), _NT, preferred_element_type=F32),
             lax.dot_general(kk, jnp.where(low, zero, qq), _NT, preferred_element_type=F32)],
            axis=1)

    ahead = s_scr.shape[0] - 1
    for i in range(min(ahead, len(units))):
        s_scr[i] = logits(*units[i])
    for i, (g, hp) in enumerate(units):
        if i + ahead < len(units):
            s_scr[(i + ahead) % (ahead + 1)] = logits(*units[i + ahead])
        rows = slice(g * BAND_BLOCK, (g + 1) * BAND_BLOCK)
        s = s_scr[i % (ahead + 1)] + bias_ref[first_variant if g == 0 else 1, hp]
        m = jnp.max(s, axis=0, keepdims=True)
        p = jnp.exp2(s - m)
        den = jnp.sum(p, axis=0, keepdims=True)
        ot = jnp.dot(vt_scr[hp * LANE:(hp + 1) * LANE, g * BAND_BLOCK:(g + 2) * BAND_BLOCK], p.astype(BF16),
                     preferred_element_type=F32)
        inv = 1.0 / den
        ot_scr[hp * LANE:hp * LANE + HEAD_DIM, rows] = ot[0:HEAD_DIM, 0:BAND_BLOCK] * inv[:, 0:BAND_BLOCK]
        ot_scr[hp * LANE + HEAD_DIM:(hp + 1) * LANE, rows] = ot[HEAD_DIM:, BAND_BLOCK:] * inv[:, BAND_BLOCK:]
        lse2 = m + jnp.log2(den)
        lt_scr[2 * hp:2 * hp + 1, rows] = lse2[:, 0:BAND_BLOCK]
        lt_scr[2 * hp + 1:2 * hp + 2, rows] = lse2[:, BAND_BLOCK:]
    o_ref[0] = ot_scr[...].T.astype(o_ref.dtype)
    lse_ref[0] = lt_scr[...].T


def _dilated_pattern(q, k, v, bias):
    batch, dilation, length, w = q.shape
    group = DIL_BLOCKS_PER_STEP
    rows = group * BAND_BLOCK
    flat = lambda t: t.reshape(batch, dilation * length, w)
    cur = lambda b, i: (b, i, 0)
    prev = lambda b, i: (b, jnp.maximum(i * group - 1, 0), 0)
    o, lse = pl.pallas_call(
        functools.partial(_dilated_kernel, group=group, blocks_per_residue=length // BAND_BLOCK),
        grid=(batch, dilation * length // rows),
        in_specs=[pl.BlockSpec((1, rows, w), cur),
                  pl.BlockSpec((1, BAND_BLOCK, w), prev), pl.BlockSpec((1, rows, w), cur),
                  pl.BlockSpec((1, BAND_BLOCK, w), prev), pl.BlockSpec((1, rows, w), cur),
                  _resident(bias.shape)],
        out_specs=[pl.BlockSpec((1, rows, w), cur), pl.BlockSpec((1, rows, LANE), cur)],
        out_shape=[jax.ShapeDtypeStruct((batch, dilation * length, w), BF16),
                   jax.ShapeDtypeStruct((batch, dilation * length, LANE), F32)],
        scratch_shapes=[pltpu.VMEM((w, rows + BAND_BLOCK), BF16), pltpu.VMEM((w, rows), F32),
                        pltpu.VMEM((LANE, rows), F32),
                        pltpu.VMEM((DIL_LOOKAHEAD + 1, 2 * BAND_BLOCK, 2 * BAND_BLOCK), F32)],
        compiler_params=_params("parallel", "parallel"),
        name=f"dilated_attn_d{dilation}",
    )(flat(q), flat(k), flat(k), flat(v), flat(v), bias)
    return o.reshape(batch, dilation, length, w), lse.reshape(batch, dilation, length, LANE)


def _even_mix(x_ref, ya_ref, o0_ref, o1_ref, o2_ref, l0_ref, l1_ref, l2_ref, gw_ref, gb_ref, w_ref,
              o_scr, l_scr):
    ya = ya_ref[...]
    gate = jnp.dot(ya.astype(BF16), gw_ref[...], preferred_element_type=F32) + gb_ref[...]
    ya = ya * _sigmoid(gate)

    tm = x_ref.shape[0]
    nb = o0_ref.shape[3] // LANE
    for p, (o_ref, l_ref) in enumerate(((o0_ref, l0_ref), (o1_ref, l1_ref), (o2_ref, l2_ref))):
        r = o_ref.shape[1]
        for res in range(r):
            dst = pl.ds(res, tm // r, stride=r) if r > 1 else slice(None)
            for c in range(nb):
                o_scr[p * nb + c, dst, :] = o_ref[0, res, :, c * LANE:(c + 1) * LANE].astype(F32)
            l_scr[p, dst, :] = l_ref[0, res]

    l0, l1, l2 = l_scr[0], l_scr[1], l_scr[2]
    m = jnp.maximum(jnp.maximum(l0, l1), l2)
    e0, e1, e2 = jnp.exp2(l0 - m), jnp.exp2(l1 - m), jnp.exp2(l2 - m)
    inv = 1.0 / (e0 + e1 + e2)
    alphas = (e0 * inv, e1 * inv, e2 * inv)
    low = lax.broadcasted_iota(jnp.int32, (tm, LANE), 1) < HEAD_DIM
    cols = []
    for c in range(nb):
        acc = jnp.zeros((tm, LANE), F32)
        for p, a in enumerate(alphas):
            weight = jnp.where(low, a[:, 2 * c:2 * c + 1], a[:, 2 * c + 1:2 * c + 2])
            acc = acc + weight * o_scr[p * nb + c]
        cols.append(acc)

    mixed = jnp.concatenate([ya] + cols, axis=1).astype(BF16)
    return x_ref[...] + jnp.dot(mixed, w_ref[...], preferred_element_type=F32)


def _diff_bias(table, tile):
    first_const = int(np.argmax(_bucket_np(np.arange(4 * MAX_DISTANCE)) == NUM_BUCKETS - 1))
    n_near = -(-(first_const + tile - 1) // tile)
    kr = np.arange(tile)[:, None]
    qc = np.arange(tile)[None, :]
    dist = np.arange(n_near)[:, None, None] * tile + qc[None] - kr[None]
    assert n_near * tile - (tile - 1) >= first_const
    rel = (table - table[NUM_BUCKETS - 1][None, :]) * LOG2E
    return jnp.where(jnp.asarray(dist >= 0)[None], _lookup(rel, _bucket_np(dist)), NEG_INF)


def _diff_attn_kernel(lam_ref, qt_ref, k_ref, vt_ref, bias_ref, g_ref, o_ref, acc_scr, s_scr, p_scr,
                      *, tile, n_near_max, out_scale):
    qi = pl.program_id(2)
    heads = qt_ref.shape[1]
    row = lax.broadcasted_iota(jnp.int32, (LANE, tile), 0)
    q_maps = []
    for h in range(heads):
        qt = qt_ref[0, h, 0]
        zero = jnp.zeros_like(qt)
        q_maps.append((jnp.where(row < HEAD_DIM, qt, zero), jnp.where(row >= HEAD_DIM, qt, zero)))

    def keys(h, j):
        return k_ref[0, pl.ds(pl.multiple_of(j * tile, tile), tile), h * LANE:(h + 1) * LANE]

    def softmax(sa, m_prev):
        m_new = jnp.maximum(m_prev, jnp.max(sa, axis=0, keepdims=True))
        return m_new, jnp.exp2(m_prev - m_new), jnp.exp2(sa - m_new).astype(BF16)

    def accumulate(h, a, j, p, alpha, l_prev):
        r = jnp.dot(vt_ref[0, h, j], p, preferred_element_type=F32)
        acc_scr[h, a] = alpha * acc_scr[h, a] + r[0:LANE]
        return alpha * l_prev + r[LANE:LANE + 1]

    def logits(h, a, j):
        s_scr[h, a] = jnp.dot(keys(h, j), q_maps[h][a], preferred_element_type=F32)

    def step(j, carry, with_bias, lookahead):
        out = []
        for h in range(heads):
            stats = list(carry[6 * h:6 * h + 6])
            for a in range(2):
                m, l, alpha = stats[3 * a:3 * a + 3]
                l = accumulate(h, a, jnp.maximum(j - 1, 0), p_scr[h, a], alpha, l)
                s = s_scr[h, a]
                if with_bias:
                    s = s + bias_ref[h, qi - j]
                m, alpha, p = softmax(s, m)
                p_scr[h, a] = p
                stats[3 * a:3 * a + 3] = [m, l, alpha]
            if lookahead:
                for a in range(2):
                    logits(h, a, j + 1)
            out += stats
        return tuple(out)

    acc_scr[...] = jnp.zeros_like(acc_scr)
    p_scr[...] = jnp.zeros_like(p_scr)
    for h in range(heads):
        for a in range(2):
            logits(h, a, 0)
    m0 = jnp.full((1, tile), 2.0 * NEG_INF, F32)
    l0 = jnp.zeros((1, tile), F32)
    carry = (m0, l0, jnp.ones((1, tile), F32)) * (2 * heads)
    n_far = qi + 1 - jnp.minimum(qi + 1, n_near_max)
    carry = lax.fori_loop(0, n_far, functools.partial(step, with_bias=False, lookahead=True), carry)
    carry = lax.fori_loop(n_far, qi, functools.partial(step, with_bias=True, lookahead=True), carry)
    carry = step(qi, carry, with_bias=True, lookahead=False)
    for h in range(heads):
        _, l1, alpha1, _, l2, alpha2 = carry[6 * h:6 * h + 6]
        l1 = accumulate(h, 0, qi, p_scr[h, 0], alpha1, l1)
        l2 = accumulate(h, 1, qi, p_scr[h, 1], alpha2, l2)
        att = acc_scr[h, 0] * (1.0 / l1) - lam_ref[...] * (acc_scr[h, 1] * (1.0 / l2))
        inv = lax.rsqrt(jnp.mean(att * att, axis=0, keepdims=True) + EPS)
        out_t = ((att * inv) * g_ref[...]) * out_scale
        o_ref[0, :, h * LANE:(h + 1) * LANE] = out_t.T.astype(o_ref.dtype)


def _diff_attention(qt, k, vt, bias, lam, subln, out_scale):
    batch, heads, per_seq, _, tile = qt.shape
    n, w = k.shape
    seq = n // batch
    n_near = bias.shape[1]
    hp = ATTN_HEADS_PER_STEP
    const = lambda h, b, i: (0, 0)
    once = pl.Buffered(1)
    out = pl.pallas_call(
        functools.partial(_diff_attn_kernel, tile=tile, n_near_max=n_near, out_scale=out_scale),
        grid=(heads // hp, batch, per_seq),
        in_specs=[pl.BlockSpec((1, tile), const),
                  pl.BlockSpec((1, hp, 1, LANE, tile), lambda h, b, i: (b, h, i, 0, 0)),
                  pl.BlockSpec((1, seq, hp * LANE), lambda h, b, i: (b, 0, h), pipeline_mode=once),
                  pl.BlockSpec((1, hp, per_seq, vt.shape[3], tile), lambda h, b, i: (b, h, 0, 0, 0),
                               pipeline_mode=once),
                  pl.BlockSpec((hp, n_near, tile, tile), lambda h, b, i: (h, 0, 0, 0), pipeline_mode=once),
                  pl.BlockSpec((LANE, tile), const)],
        out_specs=pl.BlockSpec((1, tile, hp * LANE), lambda h, b, i: (b, i, h)),
        out_shape=jax.ShapeDtypeStruct((batch, seq, w), BF16),
        scratch_shapes=[pltpu.VMEM((hp, 2, LANE, tile), F32), pltpu.VMEM((hp, 2, tile, tile), F32),
                        pltpu.VMEM((hp, 2, tile, tile), BF16)],
        compiler_params=_params("parallel", "parallel", "arbitrary"),
        name="diff_attention",
    )(jnp.full((1, tile), lam, F32), qt, k.reshape(batch, seq, w), vt, bias,
      jnp.broadcast_to(subln[:, None], (LANE, tile)))
    return out.reshape(n, w)


def _ffn_block(x, g_ref, wg_ref, wu_ref, wd_ref, gf_ref, act_scr, final_norm):
    h = _rms_scale(x, g_ref[...]).astype(BF16)
    for c in range(0, wg_ref.shape[1], FFN_CHUNK):
        gate = jnp.dot(h, wg_ref[:, c:c + FFN_CHUNK], preferred_element_type=F32)
        up = jnp.dot(h, wu_ref[:, c:c + FFN_CHUNK], preferred_element_type=F32)
        act_scr[:, c:c + FFN_CHUNK] = ((gate * _sigmoid(gate)) * up).astype(BF16)
    y = x + jnp.dot(act_scr[...], wd_ref[...], preferred_element_type=F32)
    return _rms_scale(y, gf_ref[...]) if final_norm else y


def _even_tail_kernel(x_ref, ya_ref, o0_ref, o1_ref, o2_ref, l0_ref, l1_ref, l2_ref, gw_ref, gb_ref, w_ref,
                      g_ref, wg_ref, wu_ref, wd_ref, gf_ref, out_ref, o_scr, l_scr, act_scr, *, final_norm):
    x = _even_mix(x_ref, ya_ref, o0_ref, o1_ref, o2_ref, l0_ref, l1_ref, l2_ref, gw_ref, gb_ref, w_ref,
                  o_scr, l_scr)
    out_ref[...] = _ffn_block(x, g_ref, wg_ref, wu_ref, wd_ref, gf_ref, act_scr, final_norm)


def _odd_tail_kernel(x_ref, a_ref, w_ref, g_ref, wg_ref, wu_ref, wd_ref, gf_ref, out_ref, act_scr,
                     *, final_norm):
    x = x_ref[...] + jnp.dot(a_ref[...], w_ref[...], preferred_element_type=F32)
    out_ref[...] = _ffn_block(x, g_ref, wg_ref, wu_ref, wd_ref, gf_ref, act_scr, final_norm)


def _ffn_specs(d, ffn):
    g, w_gate, w_up, w_down, g_final = ffn
    return [_resident((1, d)), _resident(w_gate.shape), _resident(w_up.shape), _resident(w_down.shape),
            _resident((1, d))]


def _even_tail(x, ya, outs, lses, glu_w, glu_b, w_out, ffn, final_norm):
    n, d = x.shape
    batch = outs[0].shape[0]
    width = outs[0].shape[3]
    tm = TOKEN_TILE
    per_seq = n // batch // tm
    row = lambda i: (i, 0)
    grouped = lambda t: pl.BlockSpec((1, t.shape[1], tm // t.shape[1], t.shape[3]),
                                     lambda i: (i // per_seq, 0, i % per_seq, 0))
    return pl.pallas_call(
        functools.partial(_even_tail_kernel, final_norm=final_norm),
        grid=(n // tm,),
        in_specs=[pl.BlockSpec((tm, d), row),
                  pl.BlockSpec((tm, ya.shape[1]), row),
                  *[grouped(t) for t in outs], *[grouped(t) for t in lses],
                  _resident(glu_w.shape), _resident(glu_b.shape), _resident(w_out.shape),
                  *_ffn_specs(d, ffn)],
        out_specs=pl.BlockSpec((tm, d), row),
        out_shape=jax.ShapeDtypeStruct((n, d), F32),
        scratch_shapes=[pltpu.VMEM((len(outs) * width // LANE, tm, LANE), F32),
                        pltpu.VMEM((len(outs), tm, LANE), F32),
                        pltpu.VMEM((tm, ffn[1].shape[1]), BF16)],
        compiler_params=_params("parallel"),
        name="even_tail",
    )(x, ya, *outs, *lses, glu_w, glu_b, w_out, *ffn)


def _odd_tail(x, att, w_out, ffn, final_norm):
    n, d = x.shape
    tm = TOKEN_TILE
    row = lambda i: (i, 0)
    return pl.pallas_call(
        functools.partial(_odd_tail_kernel, final_norm=final_norm),
        grid=(n // tm,),
        in_specs=[pl.BlockSpec((tm, d), row), pl.BlockSpec((tm, att.shape[1]), row),
                  _resident(w_out.shape), *_ffn_specs(d, ffn)],
        out_specs=pl.BlockSpec((tm, d), row),
        out_shape=jax.ShapeDtypeStruct((n, d), F32),
        scratch_shapes=[pltpu.VMEM((tm, ffn[1].shape[1]), BF16)],
        compiler_params=_params("parallel"),
        name="odd_tail",
    )(x, att, w_out, *ffn)


def kernel(x, rel_bias, norm_mix, norm_ffn, norm_final, ffn_w_gate, ffn_w_up, ffn_w_down, even_w_in, even_w_out, s5_lambda_re, s5_lambda_im, s5_log_dt, s5_b_re, s5_b_im, s5_c_re, s5_c_im, s5_d, s5_glu_w, s5_glu_b, odd_w_in, odd_w_out, diff_lambda_q1, diff_lambda_k1, diff_lambda_q2, diff_lambda_k2, diff_subln):
    batch, seq, d_model = x.shape
    depth = norm_mix.shape[0]
    dil_heads = even_w_in.shape[2] // 4 // HEAD_DIM
    table_dil = rel_bias[:, :dil_heads]
    table_diff = rel_bias[:, dil_heads:]
    dil_biases = [_dilated_bias(table_dil, w, r) for w, r in DIL_PATTERNS]
    diff_bias = _diff_bias(table_diff, ATTN_TILE)

    xs = x.reshape(batch * seq, d_model)
    for layer in range(depth):
        g_mix = norm_mix[layer][None, :]
        final = layer == depth - 1
        ffn = (norm_ffn[layer][None, :], ffn_w_gate[layer].astype(BF16), ffn_w_up[layer].astype(BF16),
               ffn_w_down[layer].astype(BF16), norm_final[None, :])
        if layer % 2 == 0:
            e = layer // 2
            u, *qkv = _even_in_proj(xs, g_mix, even_w_in[e].astype(BF16), batch,
                                    tuple(r for _, r in DIL_PATTERNS))
            mats = _s5_matrices(s5_lambda_re[e], s5_lambda_im[e], s5_log_dt[e], s5_b_re[e], s5_b_im[e],
                                s5_c_re[e], s5_c_im[e], s5_d[e])
            ya = _s5_mixer(u, mats, batch)
            outs, lses = [], []
            for p, bias in enumerate(dil_biases):
                o, lse = _dilated_pattern(*qkv[3 * p:3 * p + 3], bias)
                outs.append(o)
                lses.append(lse)
            xs = _even_tail(xs, ya, outs, lses, s5_glu_w[e].astype(BF16), s5_glu_b[e][None, :],
                            even_w_out[e].astype(BF16), ffn, final)
        else:
            o = layer // 2
            lam_init = 0.8 - 0.6 * math.exp(-0.3 * layer)
            lam = (jnp.exp(jnp.sum(diff_lambda_q1[o] * diff_lambda_k1[o]))
                   - jnp.exp(jnp.sum(diff_lambda_q2[o] * diff_lambda_k2[o])) + lam_init)
            qt, k, vt = _odd_in_proj(xs, g_mix, odd_w_in[o].astype(BF16), batch)
            att = _diff_attention(qt, k, vt, diff_bias, lam, diff_subln[o], 1.0 - lam_init)
            xs = _odd_tail(xs, att, odd_w_out[o].astype(BF16), ffn, final)
    return xs.reshape(batch, seq, d_model)
```

```python
import functools
import math

import jax
import jax.numpy as jnp
import numpy as np
from jax import lax
from jax.experimental import pallas as pl
from jax.experimental.pallas import tpu as pltpu

F32 = jnp.float32
BF16 = jnp.bfloat16

EPS = 1e-6
NEG_INF = -1e30
LANE = 128
VMEM_LIMIT = 56 * 1024 * 1024

HEAD_DIM = 64
S5_GROUP = 16
S5_STATE = 64
S5_CHUNK = 16
S5_LANE_GROUPS = LANE // S5_GROUP
DIL_PATTERNS = ((128, 1), (512, 4), (2048, 16))
BAND_BLOCK = 128
DIL_BLOCKS_PER_STEP = 16
DIL_LOOKAHEAD = 3
NUM_BUCKETS = 32
MAX_DISTANCE = 2048

TOKEN_TILE = 512
ATTN_TILE = 512
ATTN_HEADS_PER_STEP = 4
ONES_ROWS = 16
LOG2E = math.log2(math.e)
S5_ROW_TILE = 256
FFN_CHUNK = 256


def _params(*sem):
    return pltpu.CompilerParams(dimension_semantics=sem, vmem_limit_bytes=VMEM_LIMIT)


def _resident(shape):
    return pl.BlockSpec(shape, lambda *_: (0,) * len(shape), pipeline_mode=pl.Buffered(1))


def _rms_scale(x, g):
    inv = lax.rsqrt(jnp.mean(x * x, axis=-1, keepdims=True) + EPS)
    return (x * inv) * g


def _sigmoid(x):
    return 1.0 / (1.0 + jnp.exp(-x))


def _gelu_tanh(x):
    return 0.5 * x * (1.0 + jnp.tanh(math.sqrt(2.0 / math.pi) * (x + 0.044715 * (x * x * x))))


def _bucket_np(dist):
    max_exact = NUM_BUCKETS // 2
    d = np.maximum(dist, 0)
    scaled = (np.log(np.maximum(d, 1).astype(np.float64) / max_exact)
              / math.log(MAX_DISTANCE / max_exact) * (NUM_BUCKETS - max_exact))
    large = np.minimum(max_exact + scaled.astype(np.int64), NUM_BUCKETS - 1)
    return np.where(d < max_exact, d, large)


def _lookup(table, bucket):
    idx = jnp.asarray(bucket.astype(np.int8))[None]
    expand = (slice(None),) + (None,) * bucket.ndim
    out = jnp.zeros((table.shape[1],) + bucket.shape, F32)
    for b in np.unique(bucket):
        out = jnp.where(idx == b, table[int(b)].astype(F32)[expand], out)
    return out


def _even_in_kernel(x_ref, g_ref, w_ref, u_ref, *rest, width, scale, dilations):
    qkv_refs, z_scr = rest[:-1], rest[-1]
    h = _rms_scale(x_ref[...], g_ref[...]).astype(BF16)
    u_ref[...] = jnp.dot(h, w_ref[:, 0:width], preferred_element_type=F32)
    zq = jnp.dot(h, w_ref[:, width:2 * width], preferred_element_type=F32) * scale
    zkv = jnp.dot(h, w_ref[:, 2 * width:4 * width], preferred_element_type=F32)
    nb = width // LANE
    for c in range(nb):
        z_scr[c] = zq[:, c * LANE:(c + 1) * LANE]
    for c in range(2 * nb):
        z_scr[nb + c] = zkv[:, c * LANE:(c + 1) * LANE]
    tm = x_ref.shape[0]
    for p, r in enumerate(dilations):
        for res in range(r):
            rows = pl.ds(res, tm // r, stride=r) if r > 1 else slice(None)
            for c in range(3 * nb):
                qkv_refs[3 * p + c // nb][0, res, :, (c % nb) * LANE:(c % nb + 1) * LANE] = (
                    z_scr[c, rows, :].astype(BF16))


def _even_in_proj(x, g, w, batch, dilations):
    n, d = x.shape
    width = w.shape[1] // 4
    tm = TOKEN_TILE
    seq = n // batch
    per_seq = seq // tm
    row = lambda i: (i, 0)
    out_specs = [pl.BlockSpec((tm, width), row)]
    out_shape = [jax.ShapeDtypeStruct((n, width), F32)]
    for r in dilations:
        out_specs += [pl.BlockSpec((1, r, tm // r, width), lambda i: (i // per_seq, 0, i % per_seq, 0))] * 3
        out_shape += [jax.ShapeDtypeStruct((batch, r, seq // r, width), BF16)] * 3
    return pl.pallas_call(
        functools.partial(_even_in_kernel, width=width, scale=HEAD_DIM ** -0.5 * LOG2E, dilations=dilations),
        grid=(n // tm,),
        in_specs=[pl.BlockSpec((tm, d), row), _resident((1, d)), _resident(w.shape)],
        out_specs=out_specs,
        out_shape=out_shape,
        scratch_shapes=[pltpu.VMEM((3 * width // LANE, tm, LANE), F32)],
        compiler_params=_params("parallel"),
        name="even_in_proj",
    )(x, g, w)


_NT = (((1,), (1,)), ((), ()))


def _odd_in_kernel(x_ref, g_ref, wq_ref, wk_ref, wv_ref, qt_ref, k_ref, vt_ref, *, scale):
    h = _rms_scale(x_ref[...], g_ref[...]).astype(BF16)
    heads = qt_ref.shape[1]
    tm = h.shape[0]
    qt = lax.dot_general(wq_ref[...], h, _NT, preferred_element_type=F32) * scale
    qt_ref[0, :, 0] = qt.astype(BF16).reshape(heads, LANE, tm)
    k_ref[...] = jnp.dot(h, wk_ref[...], preferred_element_type=F32).astype(BF16)
    vt = lax.dot_general(wv_ref[...], h, _NT, preferred_element_type=F32)
    vt_ref[0, :, 0, 0:LANE, :] = vt.astype(BF16).reshape(heads, LANE, tm)
    vt_ref[0, :, 0, LANE:, :] = jnp.ones((heads, vt_ref.shape[3] - LANE, tm), BF16)


def _odd_in_proj(x, g, w, batch):
    n, d = x.shape
    width = w.shape[1] // 3
    heads = width // LANE
    tm = ATTN_TILE
    per_seq = n // batch // tm
    wq_t = w[:, 0:width].T
    wk = w[:, width:2 * width]
    wv_t = w[:, 2 * width:3 * width].T
    row = lambda i: (i, 0)
    tmap = lambda i: (i // per_seq, 0, i % per_seq, 0, 0)
    tshape = lambda rows: jax.ShapeDtypeStruct((batch, heads, per_seq, rows, tm), BF16)
    vrows = LANE + ONES_ROWS
    return pl.pallas_call(
        functools.partial(_odd_in_kernel, scale=HEAD_DIM ** -0.5 * LOG2E),
        grid=(n // tm,),
        in_specs=[pl.BlockSpec((tm, d), row), _resident((1, d)), _resident(wq_t.shape),
                  _resident(wk.shape), _resident(wv_t.shape)],
        out_specs=[pl.BlockSpec((1, heads, 1, LANE, tm), tmap), pl.BlockSpec((tm, width), row),
                   pl.BlockSpec((1, heads, 1, vrows, tm), tmap)],
        out_shape=[tshape(LANE), jax.ShapeDtypeStruct((n, width), BF16), tshape(vrows)],
        compiler_params=_params("parallel"),
        name="odd_in_proj",
    )(x, g, wq_t, wk, wv_t)


def _s5_matrices(lam_re, lam_im, log_dt, b_re, b_im, c_re, c_im, d_skip):
    hi = lax.Precision.HIGHEST
    L = S5_CHUNK
    G, P = lam_re.shape
    nj = G // S5_LANE_GROUPS
    dt = jnp.exp(log_dt)[:, None]
    steps = jnp.arange(L + 1, dtype=F32)[:, None, None]
    mag = jnp.exp(lam_re * dt * steps)
    ang = lam_im * dt * steps
    pr, pi = mag * jnp.cos(ang), mag * jnp.sin(ang)
    nr, ni = pr[1] - 1.0, pi[1]
    den = lam_re * lam_re + lam_im * lam_im
    cr = ((nr * lam_re + ni * lam_im) / den)[..., None]
    ci = ((ni * lam_re - nr * lam_im) / den)[..., None]
    bbr = cr * b_re - ci * b_im
    bbi = cr * b_im + ci * b_re
    car = c_re[None] * pr[:, :, None, :] - c_im[None] * pi[:, :, None, :]
    cai = c_re[None] * pi[:, :, None, :] + c_im[None] * pr[:, :, None, :]
    lg = S5_LANE_GROUPS
    ncol, nst = L * LANE, 2 * lg * P
    row_group = np.arange(lg)[:, None]
    same_col = jnp.asarray(row_group == ((np.arange(ncol) // S5_GROUP) % lg)[None, :])
    same_st = jnp.asarray(row_group == ((np.arange(nst) // P) % lg)[None, :])

    kg = (jnp.einsum('dgcp,gpe->dgce', car[:L], bbr, precision=hi)
          - jnp.einsum('dgcp,gpe->dgce', cai[:L], bbi, precision=hi))
    lag = np.arange(L)[None, :] - np.arange(L)[:, None]
    kt = jnp.where((lag >= 0)[:, :, None, None, None], kg[np.maximum(lag, 0)], 0.0)
    kt = kt.reshape(L, L, nj, lg, S5_GROUP, S5_GROUP).transpose(2, 0, 5, 1, 3, 4).reshape(nj, L, S5_GROUP, ncol)
    t_mat = jnp.where(same_col[None, None, :, None, :], kt[:, :, None, :, :], 0.0).astype(BF16)
    t_mat = t_mat.reshape(nj, ncol, ncol)

    rev = np.arange(L - 1, -1, -1)
    abr = pr[rev][..., None] * bbr[None] - pi[rev][..., None] * bbi[None]
    abi = pr[rev][..., None] * bbi[None] + pi[rev][..., None] * bbr[None]
    ab = jnp.stack([abr, abi]).reshape(2, L, nj, lg, P, S5_GROUP).transpose(2, 1, 5, 0, 3, 4)
    ab = ab.reshape(nj, L, S5_GROUP, nst)
    wst = jnp.where(same_st[None, None, :, None, :], ab[:, :, None, :, :], 0.0).astype(BF16)
    wst = wst.reshape(nj, ncol, nst)

    wo = jnp.stack([car[1:], -cai[1:]]).reshape(2, L, nj, lg, S5_GROUP, P).transpose(2, 0, 5, 1, 3, 4)
    wo = wo.reshape(nj, 2, P, ncol)
    wo = jnp.where(same_col[None, None, :, None, :], wo[:, :, None, :, :], 0.0).astype(BF16)
    wo = wo.reshape(nj, nst, ncol)

    a_chunk = jnp.stack([pr[L], pi[L]]).reshape(2, nj, S5_LANE_GROUPS * P)
    a_chunk = a_chunk.transpose(1, 0, 2).reshape(nj, 1, 2 * S5_LANE_GROUPS * P)
    skip = jnp.tile(d_skip.reshape(nj, 1, LANE), (1, 1, L))
    return t_mat, wst, wo, a_chunk, skip


def _s5_kernel(u_ref, t_ref, wst_ref, wo_ref, a_ref, skip_ref, y_ref, s_scr, h_scr, carry_scr, *, rows, half):
    @pl.when(pl.program_id(2) == 0)
    def _():
        carry_scr[...] = jnp.zeros_like(carry_scr)

    xf = jnp.concatenate([u_ref[pl.ds(t, rows, stride=S5_CHUNK), :] for t in range(S5_CHUNK)], axis=1)
    xb = xf.astype(BF16)
    s_scr[...] = jnp.dot(xb, wst_ref[0], preferred_element_type=F32)
    ar = a_ref[0, :, 0:half]
    ai = a_ref[0, :, half:2 * half]

    def step(i, carry):
        hr, hi = carry
        h_scr[pl.ds(i, 1), 0:half] = hr
        h_scr[pl.ds(i, 1), half:2 * half] = hi
        sr = s_scr[pl.ds(i, 1), 0:half]
        si = s_scr[pl.ds(i, 1), half:2 * half]
        return ar * hr - ai * hi + sr, ar * hi + ai * hr + si

    hr, hi = lax.fori_loop(0, rows, step, (carry_scr[:, 0:half], carry_scr[:, half:2 * half]), unroll=8)
    carry_scr[:, 0:half] = hr
    carry_scr[:, half:2 * half] = hi

    y = _gelu_tanh(jnp.dot(xb, t_ref[0], preferred_element_type=F32)
                   + jnp.dot(h_scr[...].astype(BF16), wo_ref[0], preferred_element_type=F32)
                   + skip_ref[0] * xf)
    for t in range(S5_CHUNK):
        y_ref[pl.ds(t, rows, stride=S5_CHUNK), :] = y[:, t * LANE:(t + 1) * LANE]


def _s5_mixer(u, mats, batch):
    t_mat, wst, wo, a_chunk, skip = mats
    n, width = u.shape
    nj = width // LANE
    cols = S5_CHUNK * LANE
    rows = S5_ROW_TILE
    tokens = rows * S5_CHUNK
    per_seq = n // batch // tokens
    nstate = wst.shape[2]
    tile = lambda j, b, k: (b * per_seq + k, j)
    per_j = lambda j, b, k: (j, 0, 0)
    return pl.pallas_call(
        functools.partial(_s5_kernel, rows=rows, half=nstate // 2),
        grid=(nj, batch, per_seq),
        in_specs=[pl.BlockSpec((tokens, LANE), tile),
                  pl.BlockSpec((1, cols, cols), per_j, pipeline_mode=pl.Buffered(1)),
                  pl.BlockSpec((1, cols, nstate), per_j, pipeline_mode=pl.Buffered(1)),
                  pl.BlockSpec((1, nstate, cols), per_j, pipeline_mode=pl.Buffered(1)),
                  pl.BlockSpec((1, 1, nstate), per_j),
                  pl.BlockSpec((1, 1, cols), per_j)],
        out_specs=pl.BlockSpec((tokens, LANE), tile),
        out_shape=jax.ShapeDtypeStruct((n, width), F32),
        scratch_shapes=[pltpu.VMEM((rows, nstate), F32), pltpu.VMEM((rows, nstate), F32),
                        pltpu.VMEM((1, nstate), F32)],
        compiler_params=_params("arbitrary", "arbitrary", "arbitrary"),
        name="s5_mixer",
    )(u, t_mat, wst, wo, a_chunk, skip)


def _dilated_bias(table, window, dilation):
    span = window // dilation
    kc = np.arange(2 * BAND_BLOCK)[:, None]
    qi = np.arange(BAND_BLOCK)[None, :]
    steps = BAND_BLOCK + qi - kc
    in_band = (steps >= 0) & (steps <= span)
    bias = _lookup(table * LOG2E, _bucket_np(np.clip(steps, 0, span) * dilation))
    rest = jnp.where(jnp.asarray(in_band)[None], bias, NEG_INF)
    first = jnp.where(jnp.asarray(in_band & (kc >= BAND_BLOCK))[None], bias, NEG_INF)
    tiles = jnp.stack([first, rest])
    return jnp.concatenate([tiles[:, 0::2], tiles[:, 1::2]], axis=3)


def _dilated_kernel(q_ref, kp_ref, kc_ref, vp_ref, vc_ref, bias_ref, o_ref, lse_ref, vt_scr, ot_scr, lt_scr,
                    s_scr, *, group, blocks_per_residue):
    first = (pl.program_id(1) * group) % blocks_per_residue == 0
    first_variant = jnp.where(first, 0, 1)
    kfull = jnp.concatenate([kp_ref[0], kc_ref[0]], axis=0)
    vfull = jnp.concatenate([vp_ref[0], vc_ref[0]], axis=0)
    vt_scr[...] = vfull.astype(F32).T.astype(BF16)
    lt_scr[...] = jnp.zeros_like(lt_scr)
    low = lax.broadcasted_iota(jnp.int32, (BAND_BLOCK, LANE), 1) < HEAD_DIM
    pairs = q_ref.shape[2] // LANE
    units = [(g, hp) for g in range(group) for hp in range(pairs)]

    def logits(g, hp):
        rows = slice(g * BAND_BLOCK, (g + 1) * BAND_BLOCK)
        cols = slice(hp * LANE, (hp + 1) * LANE)
        qq = q_ref[0, rows, cols]
        zero = jnp.zeros_like(qq)
        kk = kfull[g * BAND_BLOCK:(g + 2) * BAND_BLOCK, cols]
        return jnp.concatenate(
            [lax.dot_general(kk, jnp.where(low, qq, zero), _NT, preferred_element_type=F32),
             lax.dot_general(kk, jnp.where(low, zero, qq), _NT, preferred_element_type=F32)],
            axis=1)

    ahead = s_scr.shape[0] - 1
    for i in range(min(ahead, len(units))):
        s_scr[i] = logits(*units[i])
    for i, (g, hp) in enumerate(units):
        if i + ahead < len(units):
            s_scr[(i + ahead) % (ahead + 1)] = logits(*units[i + ahead])
        rows = slice(g * BAND_BLOCK, (g + 1) * BAND_BLOCK)
        s = s_scr[i % (ahead + 1)] + bias_ref[first_variant if g == 0 else 1, hp]
        m = jnp.max(s, axis=0, keepdims=True)
        p = jnp.exp2(s - m)
        den = jnp.sum(p, axis=0, keepdims=True)
        ot = jnp.dot(vt_scr[hp * LANE:(hp + 1) * LANE, g * BAND_BLOCK:(g + 2) * BAND_BLOCK], p.astype(BF16),
                     preferred_element_type=F32)
        inv = 1.0 / den
        ot_scr[hp * LANE:hp * LANE + HEAD_DIM, rows] = ot[0:HEAD_DIM, 0:BAND_BLOCK] * inv[:, 0:BAND_BLOCK]
        ot_scr[hp * LANE + HEAD_DIM:(hp + 1) * LANE, rows] = ot[HEAD_DIM:, BAND_BLOCK:] * inv[:, BAND_BLOCK:]
        lse2 = m + jnp.log2(den)
        lt_scr[2 * hp:2 * hp + 1, rows] = lse2[:, 0:BAND_BLOCK]
        lt_scr[2 * hp + 1:2 * hp + 2, rows] = lse2[:, BAND_BLOCK:]
    o_ref[0] = ot_scr[...].T.astype(o_ref.dtype)
    lse_ref[0] = lt_scr[...].T


def _dilated_pattern(q, k, v, bias):
    batch, dilation, length, w = q.shape
    group = min(DIL_BLOCKS_PER_STEP, length // BAND_BLOCK)
    rows = group * BAND_BLOCK
    flat = lambda t: t.reshape(batch, dilation * length, w)
    cur = lambda b, i: (b, i, 0)
    prev = lambda b, i: (b, jnp.maximum(i * group - 1, 0), 0)
    o, lse = pl.pallas_call(
        functools.partial(_dilated_kernel, group=group, blocks_per_residue=length // BAND_BLOCK),
        grid=(batch, dilation * length // rows),
        in_specs=[pl.BlockSpec((1, rows, w), cur),
                  pl.BlockSpec((1, BAND_BLOCK, w), prev), pl.BlockSpec((1, rows, w), cur),
                  pl.BlockSpec((1, BAND_BLOCK, w), prev), pl.BlockSpec((1, rows, w), cur),
                  _resident(bias.shape)],
        out_specs=[pl.BlockSpec((1, rows, w), cur), pl.BlockSpec((1, rows, LANE), cur)],
        out_shape=[jax.ShapeDtypeStruct((batch, dilation * length, w), BF16),
                   jax.ShapeDtypeStruct((batch, dilation * length, LANE), F32)],
        scratch_shapes=[pltpu.VMEM((w, rows + BAND_BLOCK), BF16), pltpu.VMEM((w, rows), F32),
                        pltpu.VMEM((LANE, rows), F32),
                        pltpu.VMEM((DIL_LOOKAHEAD + 1, 2 * BAND_BLOCK, 2 * BAND_BLOCK), F32)],
        compiler_params=_params("parallel", "parallel"),
        name=f"dilated_attn_d{dilation}",
    )(flat(q), flat(k), flat(k), flat(v), flat(v), bias)
    return o.reshape(batch, dilation, length, w), lse.reshape(batch, dilation, length, LANE)


def _even_mix(x_ref, ya_ref, o0_ref, o1_ref, o2_ref, l0_ref, l1_ref, l2_ref, gw_ref, gb_ref, w_ref,
              o_scr, l_scr):
    ya = ya_ref[...]
    gate = jnp.dot(ya.astype(BF16), gw_ref[...], preferred_element_type=F32) + gb_ref[...]
    ya = ya * _sigmoid(gate)

    tm = x_ref.shape[0]
    nb = o0_ref.shape[3] // LANE
    for p, (o_ref, l_ref) in enumerate(((o0_ref, l0_ref), (o1_ref, l1_ref), (o2_ref, l2_ref))):
        r = o_ref.shape[1]
        for res in range(r):
            dst = pl.ds(res, tm // r, stride=r) if r > 1 else slice(None)
            for c in range(nb):
                o_scr[p * nb + c, dst, :] = o_ref[0, res, :, c * LANE:(c + 1) * LANE].astype(F32)
            l_scr[p, dst, :] = l_ref[0, res]

    l0, l1, l2 = l_scr[0], l_scr[1], l_scr[2]
    m = jnp.maximum(jnp.maximum(l0, l1), l2)
    e0, e1, e2 = jnp.exp2(l0 - m), jnp.exp2(l1 - m), jnp.exp2(l2 - m)
    inv = 1.0 / (e0 + e1 + e2)
    alphas = (e0 * inv, e1 * inv, e2 * inv)
    low = lax.broadcasted_iota(jnp.int32, (tm, LANE), 1) < HEAD_DIM
    cols = []
    for c in range(nb):
        acc = jnp.zeros((tm, LANE), F32)
        for p, a in enumerate(alphas):
            weight = jnp.where(low, a[:, 2 * c:2 * c + 1], a[:, 2 * c + 1:2 * c + 2])
            acc = acc + weight * o_scr[p * nb + c]
        cols.append(acc)

    mixed = jnp.concatenate([ya] + cols, axis=1).astype(BF16)
    return x_ref[...] + jnp.dot(mixed, w_ref[...], preferred_element_type=F32)


def _diff_bias(table, tile):
    first_const = int(np.argmax(_bucket_np(np.arange(4 * MAX_DISTANCE)) == NUM_BUCKETS - 1))
    n_near = -(-(first_const + tile - 1) // tile)
    kr = np.arange(tile)[:, None]
    qc = np.arange(tile)[None, :]
    dist = np.arange(n_near)[:, None, None] * tile + qc[None] - kr[None]
    assert n_near * tile - (tile - 1) >= first_const
    rel = (table - table[NUM_BUCKETS - 1][None, :]) * LOG2E
    return jnp.where(jnp.asarray(dist >= 0)[None], _lookup(rel, _bucket_np(dist)), NEG_INF)


def _diff_attn_kernel(lam_ref, qt_ref, k_ref, vt_ref, bias_ref, g_ref, o_ref, acc_scr, s_scr, p_scr,
                      *, tile, n_near_max, out_scale):
    qi = pl.program_id(2)
    heads = qt_ref.shape[1]
    row = lax.broadcasted_iota(jnp.int32, (LANE, tile), 0)
    q_maps = []
    for h in range(heads):
        qt = qt_ref[0, h, 0]
        zero = jnp.zeros_like(qt)
        q_maps.append((jnp.where(row < HEAD_DIM, qt, zero), jnp.where(row >= HEAD_DIM, qt, zero)))

    def keys(h, j):
        return k_ref[0, pl.ds(pl.multiple_of(j * tile, tile), tile), h * LANE:(h + 1) * LANE]

    def softmax(sa, m_prev):
        m_new = jnp.maximum(m_prev, jnp.max(sa, axis=0, keepdims=True))
        return m_new, jnp.exp2(m_prev - m_new), jnp.exp2(sa - m_new).astype(BF16)

    def accumulate(h, a, j, p, alpha, l_prev):
        r = jnp.dot(vt_ref[0, h, j], p, preferred_element_type=F32)
        acc_scr[h, a] = alpha * acc_scr[h, a] + r[0:LANE]
        return alpha * l_prev + r[LANE:LANE + 1]

    def logits(h, a, j):
        s_scr[h, a] = jnp.dot(keys(h, j), q_maps[h][a], preferred_element_type=F32)

    def step(j, carry, with_bias, lookahead):
        out = []
        for h in range(heads):
            stats = list(carry[6 * h:6 * h + 6])
            for a in range(2):
                m, l, alpha = stats[3 * a:3 * a + 3]
                l = accumulate(h, a, jnp.maximum(j - 1, 0), p_scr[h, a], alpha, l)
                s = s_scr[h, a]
                if with_bias:
                    s = s + bias_ref[h, qi - j]
                m, alpha, p = softmax(s, m)
                p_scr[h, a] = p
                stats[3 * a:3 * a + 3] = [m, l, alpha]
            if lookahead:
                for a in range(2):
                    logits(h, a, j + 1)
            out += stats
        return tuple(out)

    acc_scr[...] = jnp.zeros_like(acc_scr)
    p_scr[...] = jnp.zeros_like(p_scr)
    for h in range(heads):
        for a in range(2):
            logits(h, a, 0)
    m0 = jnp.full((1, tile), 2.0 * NEG_INF, F32)
    l0 = jnp.zeros((1, tile), F32)
    carry = (m0, l0, jnp.ones((1, tile), F32)) * (2 * heads)
    n_far = qi + 1 - jnp.minimum(qi + 1, n_near_max)
    carry = lax.fori_loop(0, n_far, functools.partial(step, with_bias=False, lookahead=True), carry)
    carry = lax.fori_loop(n_far, qi, functools.partial(step, with_bias=True, lookahead=True), carry)
    carry = step(qi, carry, with_bias=True, lookahead=False)
    for h in range(heads):
        _, l1, alpha1, _, l2, alpha2 = carry[6 * h:6 * h + 6]
        l1 = accumulate(h, 0, qi, p_scr[h, 0], alpha1, l1)
        l2 = accumulate(h, 1, qi, p_scr[h, 1], alpha2, l2)
        att = acc_scr[h, 0] * (1.0 / l1) - lam_ref[...] * (acc_scr[h, 1] * (1.0 / l2))
        inv = lax.rsqrt(jnp.mean(att * att, axis=0, keepdims=True) + EPS)
        out_t = ((att * inv) * g_ref[...]) * out_scale
        o_ref[0, :, h * LANE:(h + 1) * LANE] = out_t.T.astype(o_ref.dtype)


def _diff_attention(qt, k, vt, bias, lam, subln, out_scale):
    batch, heads, per_seq, _, tile = qt.shape
    n, w = k.shape
    seq = n // batch
    n_near = bias.shape[1]
    hp = ATTN_HEADS_PER_STEP
    const = lambda h, b, i: (0, 0)
    once = pl.Buffered(1)
    out = pl.pallas_call(
        functools.partial(_diff_attn_kernel, tile=tile, n_near_max=n_near, out_scale=out_scale),
        grid=(heads // hp, batch, per_seq),
        in_specs=[pl.BlockSpec((1, tile), const),
                  pl.BlockSpec((1, hp, 1, LANE, tile), lambda h, b, i: (b, h, i, 0, 0)),
                  pl.BlockSpec((1, seq, hp * LANE), lambda h, b, i: (b, 0, h), pipeline_mode=once),
                  pl.BlockSpec((1, hp, per_seq, vt.shape[3], tile), lambda h, b, i: (b, h, 0, 0, 0),
                               pipeline_mode=once),
                  pl.BlockSpec((hp, n_near, tile, tile), lambda h, b, i: (h, 0, 0, 0), pipeline_mode=once),
                  pl.BlockSpec((LANE, tile), const)],
        out_specs=pl.BlockSpec((1, tile, hp * LANE), lambda h, b, i: (b, i, h)),
        out_shape=jax.ShapeDtypeStruct((batch, seq, w), BF16),
        scratch_shapes=[pltpu.VMEM((hp, 2, LANE, tile), F32), pltpu.VMEM((hp, 2, tile, tile), F32),
                        pltpu.VMEM((hp, 2, tile, tile), BF16)],
        compiler_params=_params("parallel", "parallel", "arbitrary"),
        name="diff_attention",
    )(jnp.full((1, tile), lam, F32), qt, k.reshape(batch, seq, w), vt, bias,
      jnp.broadcast_to(subln[:, None], (LANE, tile)))
    return out.reshape(n, w)


def _ffn_block(x, g_ref, wg_ref, wu_ref, wd_ref, gf_ref, act_scr, final_norm):
    h = _rms_scale(x, g_ref[...]).astype(BF16)
    for c in range(0, wg_ref.shape[1], FFN_CHUNK):
        gate = jnp.dot(h, wg_ref[:, c:c + FFN_CHUNK], preferred_element_type=F32)
        up = jnp.dot(h, wu_ref[:, c:c + FFN_CHUNK], preferred_element_type=F32)
        act_scr[:, c:c + FFN_CHUNK] = ((gate * _sigmoid(gate)) * up).astype(BF16)
    y = x + jnp.dot(act_scr[...], wd_ref[...], preferred_element_type=F32)
    return _rms_scale(y, gf_ref[...]) if final_norm else y


def _even_tail_kernel(x_ref, ya_ref, o0_ref, o1_ref, o2_ref, l0_ref, l1_ref, l2_ref, gw_ref, gb_ref, w_ref,
                      g_ref, wg_ref, wu_ref, wd_ref, gf_ref, out_ref, o_scr, l_scr, act_scr, *, final_norm):
    x = _even_mix(x_ref, ya_ref, o0_ref, o1_ref, o2_ref, l0_ref, l1_ref, l2_ref, gw_ref, gb_ref, w_ref,
                  o_scr, l_scr)
    out_ref[...] = _ffn_block(x, g_ref, wg_ref, wu_ref, wd_ref, gf_ref, act_scr, final_norm)


def _odd_tail_kernel(x_ref, a_ref, w_ref, g_ref, wg_ref, wu_ref, wd_ref, gf_ref, out_ref, act_scr,
                     *, final_norm):
    x = x_ref[...] + jnp.dot(a_ref[...], w_ref[...], preferred_element_type=F32)
    out_ref[...] = _ffn_block(x, g_ref, wg_ref, wu_ref, wd_ref, gf_ref, act_scr, final_norm)


def _ffn_specs(d, ffn):
    g, w_gate, w_up, w_down, g_final = ffn
    return [_resident((1, d)), _resident(w_gate.shape), _resident(w_up.shape), _resident(w_down.shape),
            _resident((1, d))]


def _even_tail(x, ya, outs, lses, glu_w, glu_b, w_out, ffn, final_norm):
    n, d = x.shape
    batch = outs[0].shape[0]
    width = outs[0].shape[3]
    tm = TOKEN_TILE
    per_seq = n // batch // tm
    row = lambda i: (i, 0)
    grouped = lambda t: pl.BlockSpec((1, t.shape[1], tm // t.shape[1], t.shape[3]),
                                     lambda i: (i // per_seq, 0, i % per_seq, 0))
    return pl.pallas_call(
        functools.partial(_even_tail_kernel, final_norm=final_norm),
        grid=(n // tm,),
        in_specs=[pl.BlockSpec((tm, d), row),
                  pl.BlockSpec((tm, ya.shape[1]), row),
                  *[grouped(t) for t in outs], *[grouped(t) for t in lses],
                  _resident(glu_w.shape), _resident(glu_b.shape), _resident(w_out.shape),
                  *_ffn_specs(d, ffn)],
        out_specs=pl.BlockSpec((tm, d), row),
        out_shape=jax.ShapeDtypeStruct((n, d), F32),
        scratch_shapes=[pltpu.VMEM((len(outs) * width // LANE, tm, LANE), F32),
                        pltpu.VMEM((len(outs), tm, LANE), F32),
                        pltpu.VMEM((tm, ffn[1].shape[1]), BF16)],
        compiler_params=_params("parallel"),
        name="even_tail",
    )(x, ya, *outs, *lses, glu_w, glu_b, w_out, *ffn)


def _odd_tail(x, att, w_out, ffn, final_norm):
    n, d = x.shape
    tm = TOKEN_TILE
    row = lambda i: (i, 0)
    return pl.pallas_call(
        functools.partial(_odd_tail_kernel, final_norm=final_norm),
        grid=(n // tm,),
        in_specs=[pl.BlockSpec((tm, d), row), pl.BlockSpec((tm, att.shape[1]), row),
                  _resident(w_out.shape), *_ffn_specs(d, ffn)],
        out_specs=pl.BlockSpec((tm, d), row),
        out_shape=jax.ShapeDtypeStruct((n, d), F32),
        scratch_shapes=[pltpu.VMEM((tm, ffn[1].shape[1]), BF16)],
        compiler_params=_params("parallel"),
        name="odd_tail",
    )(x, att, w_out, *ffn)


def kernel(x, rel_bias, norm_mix, norm_ffn, norm_final, ffn_w_gate, ffn_w_up, ffn_w_down, even_w_in, even_w_out, s5_lambda_re, s5_lambda_im, s5_log_dt, s5_b_re, s5_b_im, s5_c_re, s5_c_im, s5_d, s5_glu_w, s5_glu_b, odd_w_in, odd_w_out, diff_lambda_q1, diff_lambda_k1, diff_lambda_q2, diff_lambda_k2, diff_subln):
    batch, seq, d_model = x.shape
    depth = norm_mix.shape[0]
    dil_heads = even_w_in.shape[2] // 4 // HEAD_DIM
    table_dil = rel_bias[:, :dil_heads]
    table_diff = rel_bias[:, dil_heads:]
    dil_biases = [_dilated_bias(table_dil, w, r) for w, r in DIL_PATTERNS]
    diff_bias = _diff_bias(table_diff, ATTN_TILE)

    xs = x.reshape(batch * seq, d_model)
    for layer in range(depth):
        g_mix = norm_mix[layer][None, :]
        final = layer == depth - 1
        ffn = (norm_ffn[layer][None, :], ffn_w_gate[layer].astype(BF16), ffn_w_up[layer].astype(BF16),
               ffn_w_down[layer].astype(BF16), norm_final[None, :])
        if layer % 2 == 0:
            e = layer // 2
            u, *qkv = _even_in_proj(xs, g_mix, even_w_in[e].astype(BF16), batch,
                                    tuple(r for _, r in DIL_PATTERNS))
            mats = _s5_matrices(s5_lambda_re[e], s5_lambda_im[e], s5_log_dt[e], s5_b_re[e], s5_b_im[e],
                                s5_c_re[e], s5_c_im[e], s5_d[e])
            ya = _s5_mixer(u, mats, batch)
            outs, lses = [], []
            for p, bias in enumerate(dil_biases):
                o, lse = _dilated_pattern(*qkv[3 * p:3 * p + 3], bias)
                outs.append(o)
                lses.append(lse)
            xs = _even_tail(xs, ya, outs, lses, s5_glu_w[e].astype(BF16), s5_glu_b[e][None, :],
                            even_w_out[e].astype(BF16), ffn, final)
        else:
            o = layer // 2
            lam_init = 0.8 - 0.6 * math.exp(-0.3 * layer)
            lam = (jnp.exp(jnp.sum(diff_lambda_q1[o] * diff_lambda_k1[o]))
                   - jnp.exp(jnp.sum(diff_lambda_q2[o] * diff_lambda_k2[o])) + lam_init)
            qt, k, vt = _odd_in_proj(xs, g_mix, odd_w_in[o].astype(BF16), batch)
            att = _diff_attention(qt, k, vt, diff_bias, lam, diff_subln[o], 1.0 - lam_init)
            xs = _odd_tail(xs, att, odd_w_out[o].astype(BF16), ffn, final)
    return xs.reshape(batch, seq, d_model)
```

```python
import functools
import math

import jax
import jax.numpy as jnp
import numpy as np
from jax import lax
from jax.experimental import pallas as pl
from jax.experimental.pallas import tpu as pltpu

F32 = jnp.float32
BF16 = jnp.bfloat16

EPS = 1e-6
NEG_INF = -1e30
LANE = 128
VMEM_LIMIT = 56 * 1024 * 1024

HEAD_DIM = 64
S5_GROUP = 16
S5_STATE = 64
S5_CHUNK = 16
S5_LANE_GROUPS = LANE // S5_GROUP
DIL_PATTERNS = ((128, 1), (512, 4), (2048, 16))
BAND_BLOCK = 128
DIL_BLOCKS_PER_STEP = 16
DIL_LOOKAHEAD = 3
NUM_BUCKETS = 32
MAX_DISTANCE = 2048

TOKEN_TILE = 512
ATTN_TILE = 512
ATTN_HEADS_PER_STEP = 4
ONES_ROWS = 16
LOG2E = math.log2(math.e)
S5_ROW_TILE = 256
FFN_CHUNK = 256


def _params(*sem):
    return pltpu.CompilerParams(dimension_semantics=sem, vmem_limit_bytes=VMEM_LIMIT)


def _resident(shape):
    return pl.BlockSpec(shape, lambda *_: (0,) * len(shape), pipeline_mode=pl.Buffered(1))


def _rms_scale(x, g):
    inv = lax.rsqrt(jnp.mean(x * x, axis=-1, keepdims=True) + EPS)
    return (x * inv) * g


def _sigmoid(x):
    return 1.0 / (1.0 + jnp.exp(-x))


def _gelu_tanh(x):
    return 0.5 * x * (1.0 + jnp.tanh(math.sqrt(2.0 / math.pi) * (x + 0.044715 * (x * x * x))))


def _bucket_np(dist):
    max_exact = NUM_BUCKETS // 2
    d = np.maximum(dist, 0)
    scaled = (np.log(np.maximum(d, 1).astype(np.float64) / max_exact)
              / math.log(MAX_DISTANCE / max_exact) * (NUM_BUCKETS - max_exact))
    large = np.minimum(max_exact + scaled.astype(np.int64), NUM_BUCKETS - 1)
    return np.where(d < max_exact, d, large)


def _lookup(table, bucket):
    idx = jnp.asarray(bucket.astype(np.int8))[None]
    expand = (slice(None),) + (None,) * bucket.ndim
    out = jnp.zeros((table.shape[1],) + bucket.shape, F32)
    for b in np.unique(bucket):
        out = jnp.where(idx == b, table[int(b)].astype(F32)[expand], out)
    return out


def _even_in_kernel(x_ref, g_ref, w_ref, u_ref, *rest, width, scale, dilations):
    qkv_refs, z_scr = rest[:-1], rest[-1]
    h = _rms_scale(x_ref[...], g_ref[...]).astype(BF16)
    u_ref[...] = jnp.dot(h, w_ref[:, 0:width], preferred_element_type=F32)
    zq = jnp.dot(h, w_ref[:, width:2 * width], preferred_element_type=F32) * scale
    zkv = jnp.dot(h, w_ref[:, 2 * width:4 * width], preferred_element_type=F32)
    nb = width // LANE
    for c in range(nb):
        z_scr[c] = zq[:, c * LANE:(c + 1) * LANE]
    for c in range(2 * nb):
        z_scr[nb + c] = zkv[:, c * LANE:(c + 1) * LANE]
    tm = x_ref.shape[0]
    for p, r in enumerate(dilations):
        for res in range(r):
            rows = pl.ds(res, tm // r, stride=r) if r > 1 else slice(None)
            for c in range(3 * nb):
                qkv_refs[3 * p + c // nb][0, res, :, (c % nb) * LANE:(c % nb + 1) * LANE] = (
                    z_scr[c, rows, :].astype(BF16))


def _even_in_proj(x, g, w, batch, dilations):
    n, d = x.shape
    width = w.shape[1] // 4
    tm = TOKEN_TILE
    seq = n // batch
    per_seq = seq // tm
    row = lambda i: (i, 0)
    out_specs = [pl.BlockSpec((tm, width), row)]
    out_shape = [jax.ShapeDtypeStruct((n, width), F32)]
    for r in dilations:
        out_specs += [pl.BlockSpec((1, r, tm // r, width), lambda i: (i // per_seq, 0, i % per_seq, 0))] * 3
        out_shape += [jax.ShapeDtypeStruct((batch, r, seq // r, width), BF16)] * 3
    return pl.pallas_call(
        functools.partial(_even_in_kernel, width=width, scale=HEAD_DIM ** -0.5 * LOG2E, dilations=dilations),
        grid=(n // tm,),
        in_specs=[pl.BlockSpec((tm, d), row), _resident((1, d)), _resident(w.shape)],
        out_specs=out_specs,
        out_shape=out_shape,
        scratch_shapes=[pltpu.VMEM((3 * width // LANE, tm, LANE), F32)],
        compiler_params=_params("parallel"),
        name="even_in_proj",
    )(x, g, w)


_NT = (((1,), (1,)), ((), ()))


def _odd_in_kernel(x_ref, g_ref, wq_ref, wk_ref, wv_ref, qt_ref, k_ref, vt_ref, *, scale):
    h = _rms_scale(x_ref[...], g_ref[...]).astype(BF16)
    heads = qt_ref.shape[1]
    tm = h.shape[0]
    qt = lax.dot_general(wq_ref[...], h, _NT, preferred_element_type=F32) * scale
    qt_ref[0, :, 0] = qt.astype(BF16).reshape(heads, LANE, tm)
    k_ref[...] = jnp.dot(h, wk_ref[...], preferred_element_type=F32).astype(BF16)
    vt = lax.dot_general(wv_ref[...], h, _NT, preferred_element_type=F32)
    vt_ref[0, :, 0, 0:LANE, :] = vt.astype(BF16).reshape(heads, LANE, tm)
    vt_ref[0, :, 0, LANE:, :] = jnp.ones((heads, vt_ref.shape[3] - LANE, tm), BF16)


def _odd_in_proj(x, g, w, batch):
    n, d = x.shape
    width = w.shape[1] // 3
    heads = width // LANE
    tm = ATTN_TILE
    per_seq = n // batch // tm
    wq_t = w[:, 0:width].T
    wk = w[:, width:2 * width]
    wv_t = w[:, 2 * width:3 * width].T
    row = lambda i: (i, 0)
    tmap = lambda i: (i // per_seq, 0, i % per_seq, 0, 0)
    tshape = lambda rows: jax.ShapeDtypeStruct((batch, heads, per_seq, rows, tm), BF16)
    vrows = LANE + ONES_ROWS
    return pl.pallas_call(
        functools.partial(_odd_in_kernel, scale=HEAD_DIM ** -0.5 * LOG2E),
        grid=(n // tm,),
        in_specs=[pl.BlockSpec((tm, d), row), _resident((1, d)), _resident(wq_t.shape),
                  _resident(wk.shape), _resident(wv_t.shape)],
        out_specs=[pl.BlockSpec((1, heads, 1, LANE, tm), tmap), pl.BlockSpec((tm, width), row),
                   pl.BlockSpec((1, heads, 1, vrows, tm), tmap)],
        out_shape=[tshape(LANE), jax.ShapeDtypeStruct((n, width), BF16), tshape(vrows)],
        compiler_params=_params("parallel"),
        name="odd_in_proj",
    )(x, g, wq_t, wk, wv_t)


def _s5_matrices(lam_re, lam_im, log_dt, b_re, b_im, c_re, c_im, d_skip):
    hi = lax.Precision.HIGHEST
    L = S5_CHUNK
    G, P = lam_re.shape
    nj = G // S5_LANE_GROUPS
    dt = jnp.exp(log_dt)[:, None]
    steps = jnp.arange(L + 1, dtype=F32)[:, None, None]
    mag = jnp.exp(lam_re * dt * steps)
    ang = lam_im * dt * steps
    pr, pi = mag * jnp.cos(ang), mag * jnp.sin(ang)
    nr, ni = pr[1] - 1.0, pi[1]
    den = lam_re * lam_re + lam_im * lam_im
    cr = ((nr * lam_re + ni * lam_im) / den)[..., None]
    ci = ((ni * lam_re - nr * lam_im) / den)[..., None]
    bbr = cr * b_re - ci * b_im
    bbi = cr * b_im + ci * b_re
    car = c_re[None] * pr[:, :, None, :] - c_im[None] * pi[:, :, None, :]
    cai = c_re[None] * pi[:, :, None, :] + c_im[None] * pr[:, :, None, :]
    lg = S5_LANE_GROUPS
    ncol, nst = L * LANE, 2 * lg * P
    row_group = np.arange(lg)[:, None]
    same_col = jnp.asarray(row_group == ((np.arange(ncol) // S5_GROUP) % lg)[None, :])
    same_st = jnp.asarray(row_group == ((np.arange(nst) // P) % lg)[None, :])

    kg = (jnp.einsum('dgcp,gpe->dgce', car[:L], bbr, precision=hi)
          - jnp.einsum('dgcp,gpe->dgce', cai[:L], bbi, precision=hi))
    lag = np.arange(L)[None, :] - np.arange(L)[:, None]
    kt = jnp.where((lag >= 0)[:, :, None, None, None], kg[np.maximum(lag, 0)], 0.0)
    kt = kt.reshape(L, L, nj, lg, S5_GROUP, S5_GROUP).transpose(2, 0, 5, 1, 3, 4).reshape(nj, L, S5_GROUP, ncol)
    t_mat = jnp.where(same_col[None, None, :, None, :], kt[:, :, None, :, :], 0.0).astype(BF16)
    t_mat = t_mat.reshape(nj, ncol, ncol)

    rev = np.arange(L - 1, -1, -1)
    abr = pr[rev][..., None] * bbr[None] - pi[rev][..., None] * bbi[None]
    abi = pr[rev][..., None] * bbi[None] + pi[rev][..., None] * bbr[None]
    ab = jnp.stack([abr, abi]).reshape(2, L, nj, lg, P, S5_GROUP).transpose(2, 1, 5, 0, 3, 4)
    ab = ab.reshape(nj, L, S5_GROUP, nst)
    wst = jnp.where(same_st[None, None, :, None, :], ab[:, :, None, :, :], 0.0).astype(BF16)
    wst = wst.reshape(nj, ncol, nst)

    wo = jnp.stack([car[1:], -cai[1:]]).reshape(2, L, nj, lg, S5_GROUP, P).transpose(2, 0, 5, 1, 3, 4)
    wo = wo.reshape(nj, 2, P, ncol)
    wo = jnp.where(same_col[None, None, :, None, :], wo[:, :, None, :, :], 0.0).astype(BF16)
    wo = wo.reshape(nj, nst, ncol)

    a_chunk = jnp.stack([pr[L], pi[L]]).reshape(2, nj, S5_LANE_GROUPS * P)
    a_chunk = a_chunk.transpose(1, 0, 2).reshape(nj, 1, 2 * S5_LANE_GROUPS * P)
    skip = jnp.tile(d_skip.reshape(nj, 1, LANE), (1, 1, L))
    return t_mat, wst, wo, a_chunk, skip


def _s5_kernel(u_ref, t_ref, wst_ref, wo_ref, a_ref, skip_ref, y_ref, s_scr, h_scr, carry_scr, *, rows, half):
    @pl.when(pl.program_id(2) == 0)
    def _():
        carry_scr[...] = jnp.zeros_like(carry_scr)

    xf = jnp.concatenate([u_ref[pl.ds(t, rows, stride=S5_CHUNK), :] for t in range(S5_CHUNK)], axis=1)
    xb = xf.astype(BF16)
    s_scr[...] = jnp.dot(xb, wst_ref[0], preferred_element_type=F32)
    ar = a_ref[0, :, 0:half]
    ai = a_ref[0, :, half:2 * half]

    def step(i, carry):
        hr, hi = carry
        h_scr[pl.ds(i, 1), 0:half] = hr
        h_scr[pl.ds(i, 1), half:2 * half] = hi
        sr = s_scr[pl.ds(i, 1), 0:half]
        si = s_scr[pl.ds(i, 1), half:2 * half]
        return ar * hr - ai * hi + sr, ar * hi + ai * hr + si

    hr, hi = lax.fori_loop(0, rows, step, (carry_scr[:, 0:half], carry_scr[:, half:2 * half]), unroll=8)
    carry_scr[:, 0:half] = hr
    carry_scr[:, half:2 * half] = hi

    hb = h_scr[...].astype(BF16)
    step_cols = 2 * LANE
    for c0 in range(0, S5_CHUNK * LANE, step_cols):
        c1 = c0 + step_cols
        y = _gelu_tanh(jnp.dot(xb[:, 0:c1], t_ref[0, 0:c1, c0:c1], preferred_element_type=F32)
                       + jnp.dot(hb, wo_ref[0, :, c0:c1], preferred_element_type=F32)
                       + skip_ref[0, :, c0:c1] * xf[:, c0:c1])
        for t in range(c0 // LANE, c1 // LANE):
            y_ref[pl.ds(t, rows, stride=S5_CHUNK), :] = y[:, t * LANE - c0:(t + 1) * LANE - c0]


def _s5_mixer(u, mats, batch):
    t_mat, wst, wo, a_chunk, skip = mats
    n, width = u.shape
    nj = width // LANE
    cols = S5_CHUNK * LANE
    rows = S5_ROW_TILE
    tokens = rows * S5_CHUNK
    per_seq = n // batch // tokens
    nstate = wst.shape[2]
    tile = lambda j, b, k: (b * per_seq + k, j)
    per_j = lambda j, b, k: (j, 0, 0)
    return pl.pallas_call(
        functools.partial(_s5_kernel, rows=rows, half=nstate // 2),
        grid=(nj, batch, per_seq),
        in_specs=[pl.BlockSpec((tokens, LANE), tile),
                  pl.BlockSpec((1, cols, cols), per_j, pipeline_mode=pl.Buffered(1)),
                  pl.BlockSpec((1, cols, nstate), per_j, pipeline_mode=pl.Buffered(1)),
                  pl.BlockSpec((1, nstate, cols), per_j, pipeline_mode=pl.Buffered(1)),
                  pl.BlockSpec((1, 1, nstate), per_j),
                  pl.BlockSpec((1, 1, cols), per_j)],
        out_specs=pl.BlockSpec((tokens, LANE), tile),
        out_shape=jax.ShapeDtypeStruct((n, width), F32),
        scratch_shapes=[pltpu.VMEM((rows, nstate), F32), pltpu.VMEM((rows, nstate), F32),
                        pltpu.VMEM((1, nstate), F32)],
        compiler_params=_params("arbitrary", "arbitrary", "arbitrary"),
        name="s5_mixer",
    )(u, t_mat, wst, wo, a_chunk, skip)


def _dilated_bias(table, window, dilation):
    span = window // dilation
    kc = np.arange(2 * BAND_BLOCK)[:, None]
    qi = np.arange(BAND_BLOCK)[None, :]
    steps = BAND_BLOCK + qi - kc
    in_band = (steps >= 0) & (steps <= span)
    bias = _lookup(table * LOG2E, _bucket_np(np.clip(steps, 0, span) * dilation))
    rest = jnp.where(jnp.asarray(in_band)[None], bias, NEG_INF)
    first = jnp.where(jnp.asarray(in_band & (kc >= BAND_BLOCK))[None], bias, NEG_INF)
    tiles = jnp.stack([first, rest])
    return jnp.concatenate([tiles[:, 0::2], tiles[:, 1::2]], axis=3)


def _dilated_kernel(q_ref, kp_ref, kc_ref, vp_ref, vc_ref, bias_ref, o_ref, lse_ref, vt_scr, ot_scr, lt_scr,
                    s_scr, *, group, blocks_per_residue):
    first = (pl.program_id(1) * group) % blocks_per_residue == 0
    first_variant = jnp.where(first, 0, 1)
    kfull = jnp.concatenate([kp_ref[0], kc_ref[0]], axis=0)
    vfull = jnp.concatenate([vp_ref[0], vc_ref[0]], axis=0)
    vt_scr[...] = vfull.astype(F32).T.astype(BF16)
    lt_scr[...] = jnp.zeros_like(lt_scr)
    low = lax.broadcasted_iota(jnp.int32, (BAND_BLOCK, LANE), 1) < HEAD_DIM
    pairs = q_ref.shape[2] // LANE
    units = [(g, hp) for g in range(group) for hp in range(pairs)]

    def logits(g, hp):
        rows = slice(g * BAND_BLOCK, (g + 1) * BAND_BLOCK)
        cols = slice(hp * LANE, (hp + 1) * LANE)
        qq = q_ref[0, rows, cols]
        zero = jnp.zeros_like(qq)
        kk = kfull[g * BAND_BLOCK:(g + 2) * BAND_BLOCK, cols]
        return jnp.concatenate(
            [lax.dot_general(kk, jnp.where(low, qq, zero), _NT, preferred_element_type=F32),
             lax.dot_general(kk, jnp.where(low, zero, qq), _NT, preferred_element_type=F32)],
            axis=1)

    ahead = s_scr.shape[0] - 1
    for i in range(min(ahead, len(units))):
        s_scr[i] = logits(*units[i])
    for i, (g, hp) in enumerate(units):
        if i + ahead < len(units):
            s_scr[(i + ahead) % (ahead + 1)] = logits(*units[i + ahead])
        rows = slice(g * BAND_BLOCK, (g + 1) * BAND_BLOCK)
        s = s_scr[i % (ahead + 1)] + bias_ref[first_variant if g == 0 else 1, hp]
        m = jnp.max(s, axis=0, keepdims=True)
        p = jnp.exp2(s - m)
        den = jnp.sum(p, axis=0, keepdims=True)
        ot = jnp.dot(vt_scr[hp * LANE:(hp + 1) * LANE, g * BAND_BLOCK:(g + 2) * BAND_BLOCK], p.astype(BF16),
                     preferred_element_type=F32)
        inv = 1.0 / den
        ot_scr[hp * LANE:hp * LANE + HEAD_DIM, rows] = ot[0:HEAD_DIM, 0:BAND_BLOCK] * inv[:, 0:BAND_BLOCK]
        ot_scr[hp * LANE + HEAD_DIM:(hp + 1) * LANE, rows] = ot[HEAD_DIM:, BAND_BLOCK:] * inv[:, BAND_BLOCK:]
        lse2 = m + jnp.log2(den)
        lt_scr[2 * hp:2 * hp + 1, rows] = lse2[:, 0:BAND_BLOCK]
        lt_scr[2 * hp + 1:2 * hp + 2, rows] = lse2[:, BAND_BLOCK:]
    o_ref[0] = ot_scr[...].T.astype(o_ref.dtype)
    lse_ref[0] = lt_scr[...].T


def _dilated_pattern(q, k, v, bias):
    batch, dilation, length, w = q.shape
    group = min(DIL_BLOCKS_PER_STEP, length // BAND_BLOCK)
    rows = group * BAND_BLOCK
    flat = lambda t: t.reshape(batch, dilation * length, w)
    cur = lambda b, i: (b, i, 0)
    prev = lambda b, i: (b, jnp.maximum(i * group - 1, 0), 0)
    o, lse = pl.pallas_call(
        functools.partial(_dilated_kernel, group=group, blocks_per_residue=length // BAND_BLOCK),
        grid=(batch, dilation * length // rows),
        in_specs=[pl.BlockSpec((1, rows, w), cur),
                  pl.BlockSpec((1, BAND_BLOCK, w), prev), pl.BlockSpec((1, rows, w), cur),
                  pl.BlockSpec((1, BAND_BLOCK, w), prev), pl.BlockSpec((1, rows, w), cur),
                  _resident(bias.shape)],
        out_specs=[pl.BlockSpec((1, rows, w), cur), pl.BlockSpec((1, rows, LANE), cur)],
        out_shape=[jax.ShapeDtypeStruct((batch, dilation * length, w), BF16),
                   jax.ShapeDtypeStruct((batch, dilation * length, LANE), F32)],
        scratch_shapes=[pltpu.VMEM((w, rows + BAND_BLOCK), BF16), pltpu.VMEM((w, rows), F32),
                        pltpu.VMEM((LANE, rows), F32),
                        pltpu.VMEM((DIL_LOOKAHEAD + 1, 2 * BAND_BLOCK, 2 * BAND_BLOCK), F32)],
        compiler_params=_params("parallel", "parallel"),
        name=f"dilated_attn_d{dilation}",
    )(flat(q), flat(k), flat(k), flat(v), flat(v), bias)
    return o.reshape(batch, dilation, length, w), lse.reshape(batch, dilation, length, LANE)


def _even_mix(x_ref, ya_ref, o0_ref, o1_ref, o2_ref, l0_ref, l1_ref, l2_ref, gw_ref, gb_ref, w_ref,
              o_scr, l_scr):
    ya = ya_ref[...]
    gate = jnp.dot(ya.astype(BF16), gw_ref[...], preferred_element_type=F32) + gb_ref[...]
    ya = ya * _sigmoid(gate)

    tm = x_ref.shape[0]
    nb = o0_ref.shape[3] // LANE
    for p, (o_ref, l_ref) in enumerate(((o0_ref, l0_ref), (o1_ref, l1_ref), (o2_ref, l2_ref))):
        r = o_ref.shape[1]
        for res in range(r):
            dst = pl.ds(res, tm // r, stride=r) if r > 1 else slice(None)
            for c in range(nb):
                o_scr[p * nb + c, dst, :] = o_ref[0, res, :, c * LANE:(c + 1) * LANE].astype(F32)
            l_scr[p, dst, :] = l_ref[0, res]

    l0, l1, l2 = l_scr[0], l_scr[1], l_scr[2]
    m = jnp.maximum(jnp.maximum(l0, l1), l2)
    e0, e1, e2 = jnp.exp2(l0 - m), jnp.exp2(l1 - m), jnp.exp2(l2 - m)
    inv = 1.0 / (e0 + e1 + e2)
    alphas = (e0 * inv, e1 * inv, e2 * inv)
    low = lax.broadcasted_iota(jnp.int32, (tm, LANE), 1) < HEAD_DIM
    cols = []
    for c in range(nb):
        acc = jnp.zeros((tm, LANE), F32)
        for p, a in enumerate(alphas):
            weight = jnp.where(low, a[:, 2 * c:2 * c + 1], a[:, 2 * c + 1:2 * c + 2])
            acc = acc + weight * o_scr[p * nb + c]
        cols.append(acc)

    mixed = jnp.concatenate([ya] + cols, axis=1).astype(BF16)
    return x_ref[...] + jnp.dot(mixed, w_ref[...], preferred_element_type=F32)


def _diff_bias(table, tile):
    first_const = int(np.argmax(_bucket_np(np.arange(4 * MAX_DISTANCE)) == NUM_BUCKETS - 1))
    n_near = -(-(first_const + tile - 1) // tile)
    kr = np.arange(tile)[:, None]
    qc = np.arange(tile)[None, :]
    dist = np.arange(n_near)[:, None, None] * tile + qc[None] - kr[None]
    assert n_near * tile - (tile - 1) >= first_const
    rel = (table - table[NUM_BUCKETS - 1][None, :]) * LOG2E
    return jnp.where(jnp.asarray(dist >= 0)[None], _lookup(rel, _bucket_np(dist)), NEG_INF)


def _diff_attn_kernel(lam_ref, qt_ref, k_ref, vt_ref, bias_ref, g_ref, o_ref, acc_scr, s_scr, p_scr,
                      *, tile, n_near_max, out_scale):
    qi = pl.program_id(2)
    heads = qt_ref.shape[1]
    row = lax.broadcasted_iota(jnp.int32, (LANE, tile), 0)
    q_maps = []
    for h in range(heads):
        qt = qt_ref[0, h, 0]
        zero = jnp.zeros_like(qt)
        q_maps.append((jnp.where(row < HEAD_DIM, qt, zero), jnp.where(row >= HEAD_DIM, qt, zero)))

    def keys(h, j):
        return k_ref[0, pl.ds(pl.multiple_of(j * tile, tile), tile), h * LANE:(h + 1) * LANE]

    def softmax(sa, m_prev):
        m_new = jnp.maximum(m_prev, jnp.max(sa, axis=0, keepdims=True))
        return m_new, jnp.exp2(m_prev - m_new), jnp.exp2(sa - m_new).astype(BF16)

    def accumulate(h, a, j, p, alpha, l_prev):
        r = jnp.dot(vt_ref[0, h, j], p, preferred_element_type=F32)
        acc_scr[h, a] = alpha * acc_scr[h, a] + r[0:LANE]
        return alpha * l_prev + r[LANE:LANE + 1]

    def logits(h, a, j):
        s_scr[h, a] = jnp.dot(keys(h, j), q_maps[h][a], preferred_element_type=F32)

    def step(j, carry, with_bias, lookahead):
        out = []
        for h in range(heads):
            stats = list(carry[6 * h:6 * h + 6])
            for a in range(2):
                m, l, alpha = stats[3 * a:3 * a + 3]
                l = accumulate(h, a, jnp.maximum(j - 1, 0), p_scr[h, a], alpha, l)
                s = s_scr[h, a]
                if with_bias:
                    s = s + bias_ref[h, qi - j]
                m, alpha, p = softmax(s, m)
                p_scr[h, a] = p
                stats[3 * a:3 * a + 3] = [m, l, alpha]
            if lookahead:
                for a in range(2):
                    logits(h, a, j + 1)
            out += stats
        return tuple(out)

    acc_scr[...] = jnp.zeros_like(acc_scr)
    p_scr[...] = jnp.zeros_like(p_scr)
    for h in range(heads):
        for a in range(2):
            logits(h, a, 0)
    m0 = jnp.full((1, tile), 2.0 * NEG_INF, F32)
    l0 = jnp.zeros((1, tile), F32)
    carry = (m0, l0, jnp.ones((1, tile), F32)) * (2 * heads)
    n_far = qi + 1 - jnp.minimum(qi + 1, n_near_max)
    carry = lax.fori_loop(0, n_far, functools.partial(step, with_bias=False, lookahead=True), carry)
    carry = lax.fori_loop(n_far, qi, functools.partial(step, with_bias=True, lookahead=True), carry)
    carry = step(qi, carry, with_bias=True, lookahead=False)
    for h in range(heads):
        _, l1, alpha1, _, l2, alpha2 = carry[6 * h:6 * h + 6]
        l1 = accumulate(h, 0, qi, p_scr[h, 0], alpha1, l1)
        l2 = accumulate(h, 1, qi, p_scr[h, 1], alpha2, l2)
        att = acc_scr[h, 0] * (1.0 / l1) - lam_ref[...] * (acc_scr[h, 1] * (1.0 / l2))
        inv = lax.rsqrt(jnp.mean(att * att, axis=0, keepdims=True) + EPS)
        out_t = ((att * inv) * g_ref[...]) * out_scale
        o_ref[0, :, h * LANE:(h + 1) * LANE] = out_t.T.astype(o_ref.dtype)


def _diff_attention(qt, k, vt, bias, lam, subln, out_scale):
    batch, heads, per_seq, _, tile = qt.shape
    n, w = k.shape
    seq = n // batch
    n_near = bias.shape[1]
    hp = ATTN_HEADS_PER_STEP
    const = lambda h, b, i: (0, 0)
    once = pl.Buffered(1)
    out = pl.pallas_call(
        functools.partial(_diff_attn_kernel, tile=tile, n_near_max=n_near, out_scale=out_scale),
        grid=(heads // hp, batch, per_seq),
        in_specs=[pl.BlockSpec((1, tile), const),
                  pl.BlockSpec((1, hp, 1, LANE, tile), lambda h, b, i: (b, h, i, 0, 0)),
                  pl.BlockSpec((1, seq, hp * LANE), lambda h, b, i: (b, 0, h), pipeline_mode=once),
                  pl.BlockSpec((1, hp, per_seq, vt.shape[3], tile), lambda h, b, i: (b, h, 0, 0, 0),
                               pipeline_mode=once),
                  pl.BlockSpec((hp, n_near, tile, tile), lambda h, b, i: (h, 0, 0, 0), pipeline_mode=once),
                  pl.BlockSpec((LANE, tile), const)],
        out_specs=pl.BlockSpec((1, tile, hp * LANE), lambda h, b, i: (b, i, h)),
        out_shape=jax.ShapeDtypeStruct((batch, seq, w), BF16),
        scratch_shapes=[pltpu.VMEM((hp, 2, LANE, tile), F32), pltpu.VMEM((hp, 2, tile, tile), F32),
                        pltpu.VMEM((hp, 2, tile, tile), BF16)],
        compiler_params=_params("parallel", "parallel", "arbitrary"),
        name="diff_attention",
    )(jnp.full((1, tile), lam, F32), qt, k.reshape(batch, seq, w), vt, bias,
      jnp.broadcast_to(subln[:, None], (LANE, tile)))
    return out.reshape(n, w)


def _ffn_block(x, g_ref, wg_ref, wu_ref, wd_ref, gf_ref, act_scr, final_norm):
    h = _rms_scale(x, g_ref[...]).astype(BF16)
    for c in range(0, wg_ref.shape[1], FFN_CHUNK):
        gate = jnp.dot(h, wg_ref[:, c:c + FFN_CHUNK], preferred_element_type=F32)
        up = jnp.dot(h, wu_ref[:, c:c + FFN_CHUNK], preferred_element_type=F32)
        act_scr[:, c:c + FFN_CHUNK] = ((gate * _sigmoid(gate)) * up).astype(BF16)
    y = x + jnp.dot(act_scr[...], wd_ref[...], preferred_element_type=F32)
    return _rms_scale(y, gf_ref[...]) if final_norm else y


def _even_tail_kernel(x_ref, ya_ref, o0_ref, o1_ref, o2_ref, l0_ref, l1_ref, l2_ref, gw_ref, gb_ref, w_ref,
                      g_ref, wg_ref, wu_ref, wd_ref, gf_ref, out_ref, o_scr, l_scr, act_scr, *, final_norm):
    x = _even_mix(x_ref, ya_ref, o0_ref, o1_ref, o2_ref, l0_ref, l1_ref, l2_ref, gw_ref, gb_ref, w_ref,
                  o_scr, l_scr)
    out_ref[...] = _ffn_block(x, g_ref, wg_ref, wu_ref, wd_ref, gf_ref, act_scr, final_norm)


def _odd_tail_kernel(x_ref, a_ref, w_ref, g_ref, wg_ref, wu_ref, wd_ref, gf_ref, out_ref, act_scr,
                     *, final_norm):
    x = x_ref[...] + jnp.dot(a_ref[...], w_ref[...], preferred_element_type=F32)
    out_ref[...] = _ffn_block(x, g_ref, wg_ref, wu_ref, wd_ref, gf_ref, act_scr, final_norm)


def _ffn_specs(d, ffn):
    g, w_gate, w_up, w_down, g_final = ffn
    return [_resident((1, d)), _resident(w_gate.shape), _resident(w_up.shape), _resident(w_down.shape),
            _resident((1, d))]


def _even_tail(x, ya, outs, lses, glu_w, glu_b, w_out, ffn, final_norm):
    n, d = x.shape
    batch = outs[0].shape[0]
    width = outs[0].shape[3]
    tm = TOKEN_TILE
    per_seq = n // batch // tm
    row = lambda i: (i, 0)
    grouped = lambda t: pl.BlockSpec((1, t.shape[1], tm // t.shape[1], t.shape[3]),
                                     lambda i: (i // per_seq, 0, i % per_seq, 0))
    return pl.pallas_call(
        functools.partial(_even_tail_kernel, final_norm=final_norm),
        grid=(n // tm,),
        in_specs=[pl.BlockSpec((tm, d), row),
                  pl.BlockSpec((tm, ya.shape[1]), row),
                  *[grouped(t) for t in outs], *[grouped(t) for t in lses],
                  _resident(glu_w.shape), _resident(glu_b.shape), _resident(w_out.shape),
                  *_ffn_specs(d, ffn)],
        out_specs=pl.BlockSpec((tm, d), row),
        out_shape=jax.ShapeDtypeStruct((n, d), F32),
        scratch_shapes=[pltpu.VMEM((len(outs) * width // LANE, tm, LANE), F32),
                        pltpu.VMEM((len(outs), tm, LANE), F32),
                        pltpu.VMEM((tm, ffn[1].shape[1]), BF16)],
        compiler_params=_params("parallel"),
        name="even_tail",
    )(x, ya, *outs, *lses, glu_w, glu_b, w_out, *ffn)


def _odd_tail(x, att, w_out, ffn, final_norm):
    n, d = x.shape
    tm = TOKEN_TILE
    row = lambda i: (i, 0)
    return pl.pallas_call(
        functools.partial(_odd_tail_kernel, final_norm=final_norm),
        grid=(n // tm,),
        in_specs=[pl.BlockSpec((tm, d), row), pl.BlockSpec((tm, att.shape[1]), row),
                  _resident(w_out.shape), *_ffn_specs(d, ffn)],
        out_specs=pl.BlockSpec((tm, d), row),
        out_shape=jax.ShapeDtypeStruct((n, d), F32),
        scratch_shapes=[pltpu.VMEM((tm, ffn[1].shape[1]), BF16)],
        compiler_params=_params("parallel"),
        name="odd_tail",
    )(x, att, w_out, *ffn)


def kernel(x, rel_bias, norm_mix, norm_ffn, norm_final, ffn_w_gate, ffn_w_up, ffn_w_down, even_w_in, even_w_out, s5_lambda_re, s5_lambda_im, s5_log_dt, s5_b_re, s5_b_im, s5_c_re, s5_c_im, s5_d, s5_glu_w, s5_glu_b, odd_w_in, odd_w_out, diff_lambda_q1, diff_lambda_k1, diff_lambda_q2, diff_lambda_k2, diff_subln):
    batch, seq, d_model = x.shape
    depth = norm_mix.shape[0]
    dil_heads = even_w_in.shape[2] // 4 // HEAD_DIM
    table_dil = rel_bias[:, :dil_heads]
    table_diff = rel_bias[:, dil_heads:]
    dil_biases = [_dilated_bias(table_dil, w, r) for w, r in DIL_PATTERNS]
    diff_bias = _diff_bias(table_diff, ATTN_TILE)

    xs = x.reshape(batch * seq, d_model)
    for layer in range(depth):
        g_mix = norm_mix[layer][None, :]
        final = layer == depth - 1
        ffn = (norm_ffn[layer][None, :], ffn_w_gate[layer].astype(BF16), ffn_w_up[layer].astype(BF16),
               ffn_w_down[layer].astype(BF16), norm_final[None, :])
        if layer % 2 == 0:
            e = layer // 2
            u, *qkv = _even_in_proj(xs, g_mix, even_w_in[e].astype(BF16), batch,
                                    tuple(r for _, r in DIL_PATTERNS))
            mats = _s5_matrices(s5_lambda_re[e], s5_lambda_im[e], s5_log_dt[e], s5_b_re[e], s5_b_im[e],
                                s5_c_re[e], s5_c_im[e], s5_d[e])
            ya = _s5_mixer(u, mats, batch)
            outs, lses = [], []
            for p, bias in enumerate(dil_biases):
                o, lse = _dilated_pattern(*qkv[3 * p:3 * p + 3], bias)
                outs.append(o)
                lses.append(lse)
            xs = _even_tail(xs, ya, outs, lses, s5_glu_w[e].astype(BF16), s5_glu_b[e][None, :],
                            even_w_out[e].astype(BF16), ffn, final)
        else:
            o = layer // 2
            lam_init = 0.8 - 0.6 * math.exp(-0.3 * layer)
            lam = (jnp.exp(jnp.sum(diff_lambda_q1[o] * diff_lambda_k1[o]))
                   - jnp.exp(jnp.sum(diff_lambda_q2[o] * diff_lambda_k2[o])) + lam_init)
            qt, k, vt = _odd_in_proj(xs, g_mix, odd_w_in[o].astype(BF16), batch)
            att = _diff_attention(qt, k, vt, diff_bias, lam, diff_subln[o], 1.0 - lam_init)
            xs = _odd_tail(xs, att, odd_w_out[o].astype(BF16), ffn, final)
    return xs.reshape(batch, seq, d_model)
```

```python
import functools
import math

import jax
import jax.numpy as jnp
import numpy as np
from jax import lax
from jax.experimental import pallas as pl
from jax.experimental.pallas import tpu as pltpu

F32 = jnp.float32
BF16 = jnp.bfloat16

EPS = 1e-6
NEG_INF = -1e30
LANE = 128
VMEM_LIMIT = 56 * 1024 * 1024

HEAD_DIM = 64
S5_GROUP = 16
S5_STATE = 64
S5_CHUNK = 16
S5_LANE_GROUPS = LANE // S5_GROUP
DIL_PATTERNS = ((128, 1), (512, 4), (2048, 16))
BAND_BLOCK = 128
DIL_BLOCKS_PER_STEP = 16
DIL_LOOKAHEAD = 3
NUM_BUCKETS = 32
MAX_DISTANCE = 2048

TOKEN_TILE = 512
ATTN_TILE = 512
ATTN_HEADS_PER_STEP = 4
ONES_ROWS = 16
LOG2E = math.log2(math.e)
S5_ROW_TILE = 256
FFN_CHUNK = 256


def _params(*sem):
    return pltpu.CompilerParams(dimension_semantics=sem, vmem_limit_bytes=VMEM_LIMIT)


def _resident(shape):
    return pl.BlockSpec(shape, lambda *_: (0,) * len(shape), pipeline_mode=pl.Buffered(1))


def _rms_scale(x, g):
    inv = lax.rsqrt(jnp.mean(x * x, axis=-1, keepdims=True) + EPS)
    return (x * inv) * g


def _sigmoid(x):
    return 1.0 / (1.0 + jnp.exp(-x))


def _gelu_tanh(x):
    return 0.5 * x * (1.0 + jnp.tanh(math.sqrt(2.0 / math.pi) * (x + 0.044715 * (x * x * x))))


def _bucket_np(dist):
    max_exact = NUM_BUCKETS // 2
    d = np.maximum(dist, 0)
    scaled = (np.log(np.maximum(d, 1).astype(np.float64) / max_exact)
              / math.log(MAX_DISTANCE / max_exact) * (NUM_BUCKETS - max_exact))
    large = np.minimum(max_exact + scaled.astype(np.int64), NUM_BUCKETS - 1)
    return np.where(d < max_exact, d, large)


def _lookup(table, bucket):
    idx = jnp.asarray(bucket.astype(np.int8))[None]
    expand = (slice(None),) + (None,) * bucket.ndim
    out = jnp.zeros((table.shape[1],) + bucket.shape, F32)
    for b in np.unique(bucket):
        out = jnp.where(idx == b, table[int(b)].astype(F32)[expand], out)
    return out


def _even_in_kernel(x_ref, g_ref, w_ref, u_ref, *rest, width, scale, dilations):
    qkv_refs, z_scr = rest[:-1], rest[-1]
    h = _rms_scale(x_ref[...], g_ref[...]).astype(BF16)
    u_ref[...] = jnp.dot(h, w_ref[:, 0:width], preferred_element_type=F32)
    zq = jnp.dot(h, w_ref[:, width:2 * width], preferred_element_type=F32) * scale
    zkv = jnp.dot(h, w_ref[:, 2 * width:4 * width], preferred_element_type=F32)
    nb = width // LANE
    for c in range(nb):
        z_scr[c] = zq[:, c * LANE:(c + 1) * LANE]
    for c in range(2 * nb):
        z_scr[nb + c] = zkv[:, c * LANE:(c + 1) * LANE]
    tm = x_ref.shape[0]
    for p, r in enumerate(dilations):
        for res in range(r):
            rows = pl.ds(res, tm // r, stride=r) if r > 1 else slice(None)
            for c in range(3 * nb):
                qkv_refs[3 * p + c // nb][0, res, :, (c % nb) * LANE:(c % nb + 1) * LANE] = (
                    z_scr[c, rows, :].astype(BF16))


def _even_in_proj(x, g, w, batch, dilations):
    n, d = x.shape
    width = w.shape[1] // 4
    tm = TOKEN_TILE
    seq = n // batch
    per_seq = seq // tm
    row = lambda i: (i, 0)
    out_specs = [pl.BlockSpec((tm, width), row)]
    out_shape = [jax.ShapeDtypeStruct((n, width), F32)]
    for r in dilations:
        out_specs += [pl.BlockSpec((1, r, tm // r, width), lambda i: (i // per_seq, 0, i % per_seq, 0))] * 3
        out_shape += [jax.ShapeDtypeStruct((batch, r, seq // r, width), BF16)] * 3
    return pl.pallas_call(
        functools.partial(_even_in_kernel, width=width, scale=HEAD_DIM ** -0.5 * LOG2E, dilations=dilations),
        grid=(n // tm,),
        in_specs=[pl.BlockSpec((tm, d), row), _resident((1, d)), _resident(w.shape)],
        out_specs=out_specs,
        out_shape=out_shape,
        scratch_shapes=[pltpu.VMEM((3 * width // LANE, tm, LANE), F32)],
        compiler_params=_params("parallel"),
        name="even_in_proj",
    )(x, g, w)


_NT = (((1,), (1,)), ((), ()))


def _odd_in_kernel(x_ref, g_ref, wq_ref, wk_ref, wv_ref, qt_ref, k_ref, vt_ref, *, scale):
    h = _rms_scale(x_ref[...], g_ref[...]).astype(BF16)
    heads = qt_ref.shape[1]
    tm = h.shape[0]
    qt = lax.dot_general(wq_ref[...], h, _NT, preferred_element_type=F32) * scale
    qt_ref[0, :, 0] = qt.astype(BF16).reshape(heads, LANE, tm)
    k_ref[...] = jnp.dot(h, wk_ref[...], preferred_element_type=F32).astype(BF16)
    vt = lax.dot_general(wv_ref[...], h, _NT, preferred_element_type=F32)
    vt_ref[0, :, 0, 0:LANE, :] = vt.astype(BF16).reshape(heads, LANE, tm)
    vt_ref[0, :, 0, LANE:, :] = jnp.ones((heads, vt_ref.shape[3] - LANE, tm), BF16)


def _odd_in_proj(x, g, w, batch):
    n, d = x.shape
    width = w.shape[1] // 3
    heads = width // LANE
    tm = ATTN_TILE
    per_seq = n // batch // tm
    wq_t = w[:, 0:width].T
    wk = w[:, width:2 * width]
    wv_t = w[:, 2 * width:3 * width].T
    row = lambda i: (i, 0)
    tmap = lambda i: (i // per_seq, 0, i % per_seq, 0, 0)
    tshape = lambda rows: jax.ShapeDtypeStruct((batch, heads, per_seq, rows, tm), BF16)
    vrows = LANE + ONES_ROWS
    return pl.pallas_call(
        functools.partial(_odd_in_kernel, scale=HEAD_DIM ** -0.5 * LOG2E),
        grid=(n // tm,),
        in_specs=[pl.BlockSpec((tm, d), row), _resident((1, d)), _resident(wq_t.shape),
                  _resident(wk.shape), _resident(wv_t.shape)],
        out_specs=[pl.BlockSpec((1, heads, 1, LANE, tm), tmap), pl.BlockSpec((tm, width), row),
                   pl.BlockSpec((1, heads, 1, vrows, tm), tmap)],
        out_shape=[tshape(LANE), jax.ShapeDtypeStruct((n, width), BF16), tshape(vrows)],
        compiler_params=_params("parallel"),
        name="odd_in_proj",
    )(x, g, wq_t, wk, wv_t)


def _s5_matrices(lam_re, lam_im, log_dt, b_re, b_im, c_re, c_im, d_skip):
    hi = lax.Precision.HIGHEST
    L = S5_CHUNK
    G, P = lam_re.shape
    nj = G // S5_LANE_GROUPS
    dt = jnp.exp(log_dt)[:, None]
    steps = jnp.arange(L + 1, dtype=F32)[:, None, None]
    mag = jnp.exp(lam_re * dt * steps)
    ang = lam_im * dt * steps
    pr, pi = mag * jnp.cos(ang), mag * jnp.sin(ang)
    nr, ni = pr[1] - 1.0, pi[1]
    den = lam_re * lam_re + lam_im * lam_im
    cr = ((nr * lam_re + ni * lam_im) / den)[..., None]
    ci = ((ni * lam_re - nr * lam_im) / den)[..., None]
    bbr = cr * b_re - ci * b_im
    bbi = cr * b_im + ci * b_re
    car = c_re[None] * pr[:, :, None, :] - c_im[None] * pi[:, :, None, :]
    cai = c_re[None] * pi[:, :, None, :] + c_im[None] * pr[:, :, None, :]
    lg = S5_LANE_GROUPS
    ncol, nst = L * LANE, 2 * lg * P
    row_group = np.arange(lg)[:, None]
    same_col = jnp.asarray(row_group == ((np.arange(ncol) // S5_GROUP) % lg)[None, :])
    same_st = jnp.asarray(row_group == ((np.arange(nst) // P) % lg)[None, :])

    kg = (jnp.einsum('dgcp,gpe->dgce', car[:L], bbr, precision=hi)
          - jnp.einsum('dgcp,gpe->dgce', cai[:L], bbi, precision=hi))
    lag = np.arange(L)[None, :] - np.arange(L)[:, None]
    kt = jnp.where((lag >= 0)[:, :, None, None, None], kg[np.maximum(lag, 0)], 0.0)
    kt = kt.reshape(L, L, nj, lg, S5_GROUP, S5_GROUP).transpose(2, 0, 5, 1, 3, 4).reshape(nj, L, S5_GROUP, ncol)
    t_mat = jnp.where(same_col[None, None, :, None, :], kt[:, :, None, :, :], 0.0).astype(BF16)
    t_mat = t_mat.reshape(nj, ncol, ncol)

    rev = np.arange(L - 1, -1, -1)
    abr = pr[rev][..., None] * bbr[None] - pi[rev][..., None] * bbi[None]
    abi = pr[rev][..., None] * bbi[None] + pi[rev][..., None] * bbr[None]
    ab = jnp.stack([abr, abi]).reshape(2, L, nj, lg, P, S5_GROUP).transpose(2, 1, 5, 0, 3, 4)
    ab = ab.reshape(nj, L, S5_GROUP, nst)
    wst = jnp.where(same_st[None, None, :, None, :], ab[:, :, None, :, :], 0.0).astype(BF16)
    wst = wst.reshape(nj, ncol, nst)

    wo = jnp.stack([car[1:], -cai[1:]]).reshape(2, L, nj, lg, S5_GROUP, P).transpose(2, 0, 5, 1, 3, 4)
    wo = wo.reshape(nj, 2, P, ncol)
    wo = jnp.where(same_col[None, None, :, None, :], wo[:, :, None, :, :], 0.0).astype(BF16)
    wo = wo.reshape(nj, nst, ncol)

    a_chunk = jnp.stack([pr[L], pi[L]]).reshape(2, nj, S5_LANE_GROUPS * P)
    a_chunk = a_chunk.transpose(1, 0, 2).reshape(nj, 1, 2 * S5_LANE_GROUPS * P)
    skip = jnp.tile(d_skip.reshape(nj, 1, LANE), (1, 1, L))
    return t_mat, wst, wo, a_chunk, skip


def _s5_kernel(u_ref, t_ref, wst_ref, wo_ref, a_ref, skip_ref, y_ref, s_scr, h_scr, carry_scr, *, rows, half):
    @pl.when(pl.program_id(2) == 0)
    def _():
        carry_scr[...] = jnp.zeros_like(carry_scr)

    xf = jnp.concatenate([u_ref[pl.ds(t, rows, stride=S5_CHUNK), :] for t in range(S5_CHUNK)], axis=1)
    xb = xf.astype(BF16)
    s_scr[...] = jnp.dot(xb, wst_ref[0], preferred_element_type=F32)
    ar = a_ref[0, :, 0:half]
    ai = a_ref[0, :, half:2 * half]

    def step(i, carry):
        hr, hi = carry
        h_scr[pl.ds(i, 1), 0:half] = hr
        h_scr[pl.ds(i, 1), half:2 * half] = hi
        sr = s_scr[pl.ds(i, 1), 0:half]
        si = s_scr[pl.ds(i, 1), half:2 * half]
        return ar * hr - ai * hi + sr, ar * hi + ai * hr + si

    hr, hi = lax.fori_loop(0, rows, step, (carry_scr[:, 0:half], carry_scr[:, half:2 * half]), unroll=8)
    carry_scr[:, 0:half] = hr
    carry_scr[:, half:2 * half] = hi

    hb = h_scr[...].astype(BF16)
    step_cols = 2 * LANE
    for c0 in range(0, S5_CHUNK * LANE, step_cols):
        c1 = c0 + step_cols
        y = _gelu_tanh(jnp.dot(xb[:, 0:c1], t_ref[0, 0:c1, c0:c1], preferred_element_type=F32)
                       + jnp.dot(hb, wo_ref[0, :, c0:c1], preferred_element_type=F32)
                       + skip_ref[0, :, c0:c1] * xf[:, c0:c1])
        for t in range(c0 // LANE, c1 // LANE):
            y_ref[pl.ds(t, rows, stride=S5_CHUNK), :] = y[:, t * LANE - c0:(t + 1) * LANE - c0]


def _s5_mixer(u, mats, batch):
    t_mat, wst, wo, a_chunk, skip = mats
    n, width = u.shape
    nj = width // LANE
    cols = S5_CHUNK * LANE
    rows = S5_ROW_TILE
    tokens = rows * S5_CHUNK
    per_seq = n // batch // tokens
    nstate = wst.shape[2]
    tile = lambda j, b, k: (b * per_seq + k, j)
    per_j = lambda j, b, k: (j, 0, 0)
    return pl.pallas_call(
        functools.partial(_s5_kernel, rows=rows, half=nstate // 2),
        grid=(nj, batch, per_seq),
        in_specs=[pl.BlockSpec((tokens, LANE), tile),
                  pl.BlockSpec((1, cols, cols), per_j, pipeline_mode=pl.Buffered(1)),
                  pl.BlockSpec((1, cols, nstate), per_j, pipeline_mode=pl.Buffered(1)),
                  pl.BlockSpec((1, nstate, cols), per_j, pipeline_mode=pl.Buffered(1)),
                  pl.BlockSpec((1, 1, nstate), per_j),
                  pl.BlockSpec((1, 1, cols), per_j)],
        out_specs=pl.BlockSpec((tokens, LANE), tile),
        out_shape=jax.ShapeDtypeStruct((n, width), F32),
        scratch_shapes=[pltpu.VMEM((rows, nstate), F32), pltpu.VMEM((rows, nstate), F32),
                        pltpu.VMEM((1, nstate), F32)],
        compiler_params=_params("arbitrary", "arbitrary", "arbitrary"),
        name="s5_mixer",
    )(u, t_mat, wst, wo, a_chunk, skip)


def _dilated_bias(table, window, dilation):
    span = window // dilation
    kc = np.arange(2 * BAND_BLOCK)[:, None]
    qi = np.arange(BAND_BLOCK)[None, :]
    steps = BAND_BLOCK + qi - kc
    in_band = (steps >= 0) & (steps <= span)
    bias = _lookup(table * LOG2E, _bucket_np(np.clip(steps, 0, span) * dilation))
    rest = jnp.where(jnp.asarray(in_band)[None], bias, NEG_INF)
    first = jnp.where(jnp.asarray(in_band & (kc >= BAND_BLOCK))[None], bias, NEG_INF)
    tiles = jnp.stack([first, rest])
    return jnp.concatenate([tiles[:, 0::2], tiles[:, 1::2]], axis=3)


def _dilated_kernel(q_ref, kp_ref, kc_ref, vp_ref, vc_ref, bias_ref, o_ref, lse_ref, vt_scr, ot_scr, lt_scr,
                    s_scr, *, group, blocks_per_residue):
    first = (pl.program_id(1) * group) % blocks_per_residue == 0
    first_variant = jnp.where(first, 0, 1)
    kfull = jnp.concatenate([kp_ref[0], kc_ref[0]], axis=0)
    vfull = jnp.concatenate([vp_ref[0], vc_ref[0]], axis=0)
    vt_scr[...] = vfull.astype(F32).T.astype(BF16)
    lt_scr[...] = jnp.zeros_like(lt_scr)
    low = lax.broadcasted_iota(jnp.int32, (BAND_BLOCK, LANE), 1) < HEAD_DIM
    pairs = q_ref.shape[2] // LANE
    units = [(g, hp) for g in range(group) for hp in range(pairs)]

    def logits(g, hp):
        rows = slice(g * BAND_BLOCK, (g + 1) * BAND_BLOCK)
        cols = slice(hp * LANE, (hp + 1) * LANE)
        qq = q_ref[0, rows, cols]
        zero = jnp.zeros_like(qq)
        kk = kfull[g * BAND_BLOCK:(g + 2) * BAND_BLOCK, cols]
        return jnp.concatenate(
            [lax.dot_general(kk, jnp.where(low, qq, zero), _NT, preferred_element_type=F32),
             lax.dot_general(kk, jnp.where(low, zero, qq), _NT, preferred_element_type=F32)],
            axis=1)

    ahead = s_scr.shape[0] - 1
    for i in range(min(ahead, len(units))):
        s_scr[i] = logits(*units[i])
    for i, (g, hp) in enumerate(units):
        if i + ahead < len(units):
            s_scr[(i + ahead) % (ahead + 1)] = logits(*units[i + ahead])
        rows = slice(g * BAND_BLOCK, (g + 1) * BAND_BLOCK)
        s = s_scr[i % (ahead + 1)] + bias_ref[first_variant if g == 0 else 1, hp]
        m = jnp.max(s, axis=0, keepdims=True)
        p = jnp.exp2(s - m)
        den = jnp.sum(p, axis=0, keepdims=True)
        ot = jnp.dot(vt_scr[hp * LANE:(hp + 1) * LANE, g * BAND_BLOCK:(g + 2) * BAND_BLOCK], p.astype(BF16),
                     preferred_element_type=F32)
        inv = 1.0 / den
        ot_scr[hp * LANE:hp * LANE + HEAD_DIM, rows] = ot[0:HEAD_DIM, 0:BAND_BLOCK] * inv[:, 0:BAND_BLOCK]
        ot_scr[hp * LANE + HEAD_DIM:(hp + 1) * LANE, rows] = ot[HEAD_DIM:, BAND_BLOCK:] * inv[:, BAND_BLOCK:]
        lse2 = m + jnp.log2(den)
        lt_scr[2 * hp:2 * hp + 1, rows] = lse2[:, 0:BAND_BLOCK]
        lt_scr[2 * hp + 1:2 * hp + 2, rows] = lse2[:, BAND_BLOCK:]
    o_ref[0] = ot_scr[...].T.astype(o_ref.dtype)
    lse_ref[0] = lt_scr[...].T


def _dilated_pattern(q, k, v, bias):
    batch, dilation, length, w = q.shape
    group = min(DIL_BLOCKS_PER_STEP, length // BAND_BLOCK)
    rows = group * BAND_BLOCK
    flat = lambda t: t.reshape(batch, dilation * length, w)
    cur = lambda b, i: (b, i, 0)
    prev = lambda b, i: (b, jnp.maximum(i * group - 1, 0), 0)
    o, lse = pl.pallas_call(
        functools.partial(_dilated_kernel, group=group, blocks_per_residue=length // BAND_BLOCK),
        grid=(batch, dilation * length // rows),
        in_specs=[pl.BlockSpec((1, rows, w), cur),
                  pl.BlockSpec((1, BAND_BLOCK, w), prev), pl.BlockSpec((1, rows, w), cur),
                  pl.BlockSpec((1, BAND_BLOCK, w), prev), pl.BlockSpec((1, rows, w), cur),
                  _resident(bias.shape)],
        out_specs=[pl.BlockSpec((1, rows, w), cur), pl.BlockSpec((1, rows, LANE), cur)],
        out_shape=[jax.ShapeDtypeStruct((batch, dilation * length, w), BF16),
                   jax.ShapeDtypeStruct((batch, dilation * length, LANE), F32)],
        scratch_shapes=[pltpu.VMEM((w, rows + BAND_BLOCK), BF16), pltpu.VMEM((w, rows), F32),
                        pltpu.VMEM((LANE, rows), F32),
                        pltpu.VMEM((DIL_LOOKAHEAD + 1, 2 * BAND_BLOCK, 2 * BAND_BLOCK), F32)],
        compiler_params=_params("parallel", "parallel"),
        name=f"dilated_attn_d{dilation}",
    )(flat(q), flat(k), flat(k), flat(v), flat(v), bias)
    return o.reshape(batch, dilation, length, w), lse.reshape(batch, dilation, length, LANE)


def _even_mix(x_ref, ya_ref, o0_ref, o1_ref, o2_ref, l0_ref, l1_ref, l2_ref, gw_ref, gb_ref, w_ref,
              o_scr, l_scr):
    ya = ya_ref[...]
    gate = jnp.dot(ya.astype(BF16), gw_ref[...], preferred_element_type=F32) + gb_ref[...]
    ya = ya * _sigmoid(gate)

    tm = x_ref.shape[0]
    nb = o0_ref.shape[3] // LANE
    for p, (o_ref, l_ref) in enumerate(((o0_ref, l0_ref), (o1_ref, l1_ref), (o2_ref, l2_ref))):
        r = o_ref.shape[1]
        for res in range(r):
            dst = pl.ds(res, tm // r, stride=r) if r > 1 else slice(None)
            for c in range(nb):
                o_scr[p * nb + c, dst, :] = o_ref[0, res, :, c * LANE:(c + 1) * LANE].astype(F32)
            l_scr[p, dst, :] = l_ref[0, res]

    l0, l1, l2 = l_scr[0], l_scr[1], l_scr[2]
    m = jnp.maximum(jnp.maximum(l0, l1), l2)
    e0, e1, e2 = jnp.exp2(l0 - m), jnp.exp2(l1 - m), jnp.exp2(l2 - m)
    inv = 1.0 / (e0 + e1 + e2)
    alphas = (e0 * inv, e1 * inv, e2 * inv)
    low = lax.broadcasted_iota(jnp.int32, (tm, LANE), 1) < HEAD_DIM
    cols = []
    for c in range(nb):
        acc = jnp.zeros((tm, LANE), F32)
        for p, a in enumerate(alphas):
            weight = jnp.where(low, a[:, 2 * c:2 * c + 1], a[:, 2 * c + 1:2 * c + 2])
            acc = acc + weight * o_scr[p * nb + c]
        cols.append(acc)

    mixed = jnp.concatenate([ya] + cols, axis=1).astype(BF16)
    return x_ref[...] + jnp.dot(mixed, w_ref[...], preferred_element_type=F32)


def _diff_bias(table, tile):
    first_const = int(np.argmax(_bucket_np(np.arange(4 * MAX_DISTANCE)) == NUM_BUCKETS - 1))
    n_near = -(-(first_const + tile - 1) // tile)
    kr = np.arange(tile)[:, None]
    qc = np.arange(tile)[None, :]
    dist = np.arange(n_near)[:, None, None] * tile + qc[None] - kr[None]
    assert n_near * tile - (tile - 1) >= first_const
    rel = (table - table[NUM_BUCKETS - 1][None, :]) * LOG2E
    tiles = jnp.stack([_lookup(rel, _bucket_np(d)) for d in dist], axis=1)
    return jnp.where(jnp.asarray(dist >= 0)[None], tiles, NEG_INF)


def _diff_attn_kernel(lam_ref, qt_ref, k_ref, vt_ref, bias_ref, g_ref, o_ref, acc_scr, s_scr, p_scr,
                      *, tile, n_near_max, out_scale):
    qi = pl.program_id(2)
    heads = qt_ref.shape[1]
    row = lax.broadcasted_iota(jnp.int32, (LANE, tile), 0)
    q_maps = []
    for h in range(heads):
        qt = qt_ref[0, h, 0]
        zero = jnp.zeros_like(qt)
        q_maps.append((jnp.where(row < HEAD_DIM, qt, zero), jnp.where(row >= HEAD_DIM, qt, zero)))

    def keys(h, j):
        return k_ref[0, pl.ds(pl.multiple_of(j * tile, tile), tile), h * LANE:(h + 1) * LANE]

    def softmax(sa, m_prev):
        m_new = jnp.maximum(m_prev, jnp.max(sa, axis=0, keepdims=True))
        return m_new, jnp.exp2(m_prev - m_new), jnp.exp2(sa - m_new).astype(BF16)

    def accumulate(h, a, j, p, alpha, l_prev):
        r = jnp.dot(vt_ref[0, h, j], p, preferred_element_type=F32)
        acc_scr[h, a] = alpha * acc_scr[h, a] + r[0:LANE]
        return alpha * l_prev + r[LANE:LANE + 1]

    def logits(h, a, j):
        s_scr[h, a] = jnp.dot(keys(h, j), q_maps[h][a], preferred_element_type=F32)

    def step(j, carry, with_bias, lookahead):
        out = []
        for h in range(heads):
            stats = list(carry[6 * h:6 * h + 6])
            for a in range(2):
                m, l, alpha = stats[3 * a:3 * a + 3]
                l = accumulate(h, a, jnp.maximum(j - 1, 0), p_scr[h, a], alpha, l)
                s = s_scr[h, a]
                if with_bias:
                    s = s + bias_ref[h, qi - j]
                m, alpha, p = softmax(s, m)
                p_scr[h, a] = p
                stats[3 * a:3 * a + 3] = [m, l, alpha]
            if lookahead:
                for a in range(2):
                    logits(h, a, j + 1)
            out += stats
        return tuple(out)

    acc_scr[...] = jnp.zeros_like(acc_scr)
    p_scr[...] = jnp.zeros_like(p_scr)
    for h in range(heads):
        for a in range(2):
            logits(h, a, 0)
    m0 = jnp.full((1, tile), 2.0 * NEG_INF, F32)
    l0 = jnp.zeros((1, tile), F32)
    carry = (m0, l0, jnp.ones((1, tile), F32)) * (2 * heads)
    n_far = qi + 1 - jnp.minimum(qi + 1, n_near_max)
    carry = lax.fori_loop(0, n_far, functools.partial(step, with_bias=False, lookahead=True), carry)
    carry = lax.fori_loop(n_far, qi, functools.partial(step, with_bias=True, lookahead=True), carry)
    carry = step(qi, carry, with_bias=True, lookahead=False)
    for h in range(heads):
        _, l1, alpha1, _, l2, alpha2 = carry[6 * h:6 * h + 6]
        l1 = accumulate(h, 0, qi, p_scr[h, 0], alpha1, l1)
        l2 = accumulate(h, 1, qi, p_scr[h, 1], alpha2, l2)
        att = acc_scr[h, 0] * (1.0 / l1) - lam_ref[...] * (acc_scr[h, 1] * (1.0 / l2))
        inv = lax.rsqrt(jnp.mean(att * att, axis=0, keepdims=True) + EPS)
        out_t = ((att * inv) * g_ref[...]) * out_scale
        o_ref[0, :, h * LANE:(h + 1) * LANE] = out_t.T.astype(o_ref.dtype)


def _diff_attention(qt, k, vt, bias, lam, subln, out_scale):
    batch, heads, per_seq, _, tile = qt.shape
    n, w = k.shape
    seq = n // batch
    n_near = bias.shape[1]
    hp = ATTN_HEADS_PER_STEP
    const = lambda h, b, i: (0, 0)
    once = pl.Buffered(1)
    out = pl.pallas_call(
        functools.partial(_diff_attn_kernel, tile=tile, n_near_max=n_near, out_scale=out_scale),
        grid=(heads // hp, batch, per_seq),
        in_specs=[pl.BlockSpec((1, tile), const),
                  pl.BlockSpec((1, hp, 1, LANE, tile), lambda h, b, i: (b, h, i, 0, 0)),
                  pl.BlockSpec((1, seq, hp * LANE), lambda h, b, i: (b, 0, h), pipeline_mode=once),
                  pl.BlockSpec((1, hp, per_seq, vt.shape[3], tile), lambda h, b, i: (b, h, 0, 0, 0),
                               pipeline_mode=once),
                  pl.BlockSpec((hp, n_near, tile, tile), lambda h, b, i: (h, 0, 0, 0), pipeline_mode=once),
                  pl.BlockSpec((LANE, tile), const)],
        out_specs=pl.BlockSpec((1, tile, hp * LANE), lambda h, b, i: (b, i, h)),
        out_shape=jax.ShapeDtypeStruct((batch, seq, w), BF16),
        scratch_shapes=[pltpu.VMEM((hp, 2, LANE, tile), F32), pltpu.VMEM((hp, 2, tile, tile), F32),
                        pltpu.VMEM((hp, 2, tile, tile), BF16)],
        compiler_params=_params("parallel", "parallel", "arbitrary"),
        name="diff_attention",
    )(jnp.full((1, tile), lam, F32), qt, k.reshape(batch, seq, w), vt, bias,
      jnp.broadcast_to(subln[:, None], (LANE, tile)))
    return out.reshape(n, w)


def _ffn_block(x, g_ref, wg_ref, wu_ref, wd_ref, gf_ref, act_scr, final_norm):
    h = _rms_scale(x, g_ref[...]).astype(BF16)
    for c in range(0, wg_ref.shape[1], FFN_CHUNK):
        gate = jnp.dot(h, wg_ref[:, c:c + FFN_CHUNK], preferred_element_type=F32)
        up = jnp.dot(h, wu_ref[:, c:c + FFN_CHUNK], preferred_element_type=F32)
        act_scr[:, c:c + FFN_CHUNK] = ((gate * _sigmoid(gate)) * up).astype(BF16)
    y = x + jnp.dot(act_scr[...], wd_ref[...], preferred_element_type=F32)
    return _rms_scale(y, gf_ref[...]) if final_norm else y


def _even_tail_kernel(x_ref, ya_ref, o0_ref, o1_ref, o2_ref, l0_ref, l1_ref, l2_ref, gw_ref, gb_ref, w_ref,
                      g_ref, wg_ref, wu_ref, wd_ref, gf_ref, out_ref, o_scr, l_scr, act_scr, *, final_norm):
    x = _even_mix(x_ref, ya_ref, o0_ref, o1_ref, o2_ref, l0_ref, l1_ref, l2_ref, gw_ref, gb_ref, w_ref,
                  o_scr, l_scr)
    out_ref[...] = _ffn_block(x, g_ref, wg_ref, wu_ref, wd_ref, gf_ref, act_scr, final_norm)


def _odd_tail_kernel(x_ref, a_ref, w_ref, g_ref, wg_ref, wu_ref, wd_ref, gf_ref, out_ref, act_scr,
                     *, final_norm):
    x = x_ref[...] + jnp.dot(a_ref[...], w_ref[...], preferred_element_type=F32)
    out_ref[...] = _ffn_block(x, g_ref, wg_ref, wu_ref, wd_ref, gf_ref, act_scr, final_norm)


def _ffn_specs(d, ffn):
    g, w_gate, w_up, w_down, g_final = ffn
    return [_resident((1, d)), _resident(w_gate.shape), _resident(w_up.shape), _resident(w_down.shape),
            _resident((1, d))]


def _even_tail(x, ya, outs, lses, glu_w, glu_b, w_out, ffn, final_norm):
    n, d = x.shape
    batch = outs[0].shape[0]
    width = outs[0].shape[3]
    tm = TOKEN_TILE
    per_seq = n // batch // tm
    row = lambda i: (i, 0)
    grouped = lambda t: pl.BlockSpec((1, t.shape[1], tm // t.shape[1], t.shape[3]),
                                     lambda i: (i // per_seq, 0, i % per_seq, 0))
    return pl.pallas_call(
        functools.partial(_even_tail_kernel, final_norm=final_norm),
        grid=(n // tm,),
        in_specs=[pl.BlockSpec((tm, d), row),
                  pl.BlockSpec((tm, ya.shape[1]), row),
                  *[grouped(t) for t in outs], *[grouped(t) for t in lses],
                  _resident(glu_w.shape), _resident(glu_b.shape), _resident(w_out.shape),
                  *_ffn_specs(d, ffn)],
        out_specs=pl.BlockSpec((tm, d), row),
        out_shape=jax.ShapeDtypeStruct((n, d), F32),
        scratch_shapes=[pltpu.VMEM((len(outs) * width // LANE, tm, LANE), F32),
                        pltpu.VMEM((len(outs), tm, LANE), F32),
                        pltpu.VMEM((tm, ffn[1].shape[1]), BF16)],
        compiler_params=_params("parallel"),
        name="even_tail",
    )(x, ya, *outs, *lses, glu_w, glu_b, w_out, *ffn)


def _odd_tail(x, att, w_out, ffn, final_norm):
    n, d = x.shape
    tm = TOKEN_TILE
    row = lambda i: (i, 0)
    return pl.pallas_call(
        functools.partial(_odd_tail_kernel, final_norm=final_norm),
        grid=(n // tm,),
        in_specs=[pl.BlockSpec((tm, d), row), pl.BlockSpec((tm, att.shape[1]), row),
                  _resident(w_out.shape), *_ffn_specs(d, ffn)],
        out_specs=pl.BlockSpec((tm, d), row),
        out_shape=jax.ShapeDtypeStruct((n, d), F32),
        scratch_shapes=[pltpu.VMEM((tm, ffn[1].shape[1]), BF16)],
        compiler_params=_params("parallel"),
        name="odd_tail",
    )(x, att, w_out, *ffn)


def kernel(x, rel_bias, norm_mix, norm_ffn, norm_final, ffn_w_gate, ffn_w_up, ffn_w_down, even_w_in, even_w_out, s5_lambda_re, s5_lambda_im, s5_log_dt, s5_b_re, s5_b_im, s5_c_re, s5_c_im, s5_d, s5_glu_w, s5_glu_b, odd_w_in, odd_w_out, diff_lambda_q1, diff_lambda_k1, diff_lambda_q2, diff_lambda_k2, diff_subln):
    batch, seq, d_model = x.shape
    depth = norm_mix.shape[0]
    dil_heads = even_w_in.shape[2] // 4 // HEAD_DIM
    table_dil = rel_bias[:, :dil_heads]
    table_diff = rel_bias[:, dil_heads:]
    dil_biases = [_dilated_bias(table_dil, w, r) for w, r in DIL_PATTERNS]
    diff_bias = _diff_bias(table_diff, ATTN_TILE)

    xs = x.reshape(batch * seq, d_model)
    for layer in range(depth):
        g_mix = norm_mix[layer][None, :]
        final = layer == depth - 1
        ffn = (norm_ffn[layer][None, :], ffn_w_gate[layer].astype(BF16), ffn_w_up[layer].astype(BF16),
               ffn_w_down[layer].astype(BF16), norm_final[None, :])
        if layer % 2 == 0:
            e = layer // 2
            u, *qkv = _even_in_proj(xs, g_mix, even_w_in[e].astype(BF16), batch,
                                    tuple(r for _, r in DIL_PATTERNS))
            mats = _s5_matrices(s5_lambda_re[e], s5_lambda_im[e], s5_log_dt[e], s5_b_re[e], s5_b_im[e],
                                s5_c_re[e], s5_c_im[e], s5_d[e])
            ya = _s5_mixer(u, mats, batch)
            outs, lses = [], []
            for p, bias in enumerate(dil_biases):
                o, lse = _dilated_pattern(*qkv[3 * p:3 * p + 3], bias)
                outs.append(o)
                lses.append(lse)
            xs = _even_tail(xs, ya, outs, lses, s5_glu_w[e].astype(BF16), s5_glu_b[e][None, :],
                            even_w_out[e].astype(BF16), ffn, final)
        else:
            o = layer // 2
            lam_init = 0.8 - 0.6 * math.exp(-0.3 * layer)
            lam = (jnp.exp(jnp.sum(diff_lambda_q1[o] * diff_lambda_k1[o]))
                   - jnp.exp(jnp.sum(diff_lambda_q2[o] * diff_lambda_k2[o])) + lam_init)
            qt, k, vt = _odd_in_proj(xs, g_mix, odd_w_in[o].astype(BF16), batch)
            att = _diff_attention(qt, k, vt, diff_bias, lam, diff_subln[o], 1.0 - lam_init)
            xs = _odd_tail(xs, att, odd_w_out[o].astype(BF16), ffn, final)
    return xs.reshape(batch, seq, d_model)
```
